```python
import math
import jax
import jax.numpy as jnp
from jax import lax
import numpy as np

D_MODEL = 1024
BATCH = 8
SEQ = 4096
DEPTH = 4

GRID_W = 64
CTX_LEN = 256
N_EVEN = (DEPTH + 1) // 2
N_ODD = DEPTH // 2
EPS = 1e-6
NEG_INF = -1e30
N_MOD = 6
ROPE_THETA = 10000.0
ROT_AXIS = 32

A_HEADS = 4
A_HD = 64
A_VD = 2 * A_HD
A_QK = A_HEADS * 2 * A_HD
A_V = A_HEADS * A_VD
Q_BLOCK = 128

B_HEADS = 4
B_DK = 128
B_DV = 128
B_W = B_HEADS * B_DV
B_CONV = 4
B_CHUNK = 64

C_WIDTH = 512
C_BLOCKS = 8
C_BD = C_WIDTH // C_BLOCKS
C_CONV = 4
C_POW = 8.0

D_HEADS = 8
D_KV = 2
D_GROUP = D_HEADS // D_KV
D_HD = 64
WINDOW = 128
D_BLOCK = 128

FFN = 2816
FFN_CONV = 3

EVEN_SPLIT = (A_QK, A_QK, A_V, 3 * B_W, B_W, 2 * B_HEADS, 2 * B_HEADS)
EVEN_IN = sum(EVEN_SPLIT)
EVEN_OUT = A_V + B_W
ODD_SPLIT = (C_WIDTH, C_WIDTH, D_HEADS * D_HD, D_KV * D_HD, D_KV * D_HD)
ODD_IN = sum(ODD_SPLIT)
ODD_OUT = C_WIDTH + D_HEADS * D_HD

kernel_name = 'hybrid_diffusion_prefix_backbone'


def _split(z, widths):
    idx = np.cumsum(np.array(widths))[:-1].tolist()
    return jnp.split(z, idx, axis=-1)


def _rmsnorm(x, g):
    xf = x.astype(jnp.float32)
    y = xf * lax.rsqrt(jnp.mean(xf * xf, axis=-1, keepdims=True) + EPS)
    return (y * g.astype(jnp.float32)).astype(x.dtype)


def _l2norm(x):
    xf = x.astype(jnp.float32)
    return xf * lax.rsqrt(jnp.sum(xf * xf, axis=-1, keepdims=True) + EPS)


def _modulate(x, g, shift, scale):
    return _rmsnorm(x, g) * (1.0 + scale) + shift


def _dwconv(x, w):
    k, ch = w.shape
    pl = k // 2
    return lax.conv_general_dilated(x, w.reshape(k, 1, ch).astype(x.dtype), window_strides=(1,),
                                    padding=[(pl, k - 1 - pl)], dimension_numbers=('NWC', 'WIO', 'NWC'),
                                    feature_group_count=ch)


def _rope_tables(rows):
    row = jnp.repeat(jnp.arange(rows, dtype=jnp.float32), GRID_W)
    col = jnp.tile(jnp.arange(GRID_W, dtype=jnp.float32), rows)
    inv = ROPE_THETA ** (-jnp.arange(0, ROT_AXIS, 2, dtype=jnp.float32) / ROT_AXIS)
    ar = row[:, None] * inv
    ac = col[:, None] * inv
    return (jnp.cos(ar), jnp.sin(ar), jnp.cos(ac), jnp.sin(ac))


def _rope_1d(x, cos, sin):
    f = cos.shape[-1]
    x1, x2 = x[..., :f], x[..., f:]
    c = cos[:, None, :].astype(x.dtype)
    s = sin[:, None, :].astype(x.dtype)
    return jnp.concatenate([x1 * c - x2 * s, x2 * c + x1 * s], axis=-1)


def _rope_2d(x, tabs):
    cr, sr, cc, sc = tabs
    return jnp.concatenate([_rope_1d(x[..., :ROT_AXIS], cr, sr), _rope_1d(x[..., ROT_AXIS:], cc, sc)], axis=-1)


def _diff_attend(q, k, v, lam):
    s = jnp.einsum('bqhmd,bkhmd->bhmqk', q, k).astype(jnp.float32) * (A_HD ** -0.5)
    p = jax.nn.softmax(s, axis=-1)
    w = p[:, :, 0] - lam * p[:, :, 1]
    return jnp.einsum('bhqk,bkhe->bqhe', w.astype(v.dtype), v)


def _diff_attention(q_l, k_l, v_l, q_c, k_c, v_c, lam, with_ctx_out):
    bsz, n = q_l.shape[:2]
    nb = n // Q_BLOCK
    k_all = jnp.concatenate([k_c, k_l], axis=1)
    v_all = jnp.concatenate([v_c, v_l], axis=1)
    qb = jnp.moveaxis(q_l.reshape(bsz, nb, Q_BLOCK, A_HEADS, 2, A_HD), 1, 0)
    o = lax.map(lambda qi: _diff_attend(qi, k_all, v_all, lam), qb)
    o_l = jnp.moveaxis(o, 0, 1).reshape(bsz, n, A_HEADS, A_VD)
    o_c = _diff_attend(q_c, k_c, v_c, lam) if with_ctx_out else None
    return o_l, o_c


def _chunk(t):
    bsz, L, h = t.shape[:3]
    t = t.reshape((bsz, L // B_CHUNK, B_CHUNK, h) + t.shape[3:])
    return jnp.moveaxis(jnp.moveaxis(t, 3, 2), 1, 0)


def _unchunk(o):
    nc, bsz, h, cl, dv = o.shape
    return jnp.moveaxis(jnp.moveaxis(o, 0, 1), 2, 3).reshape(bsz, nc * cl, h, dv)


def _gdn_prepare(q, k, v, beta, g):
    q = _chunk(q) * (B_DK ** -0.5)
    k = _chunk(k)
    v = _chunk(v)
    beta = _chunk(beta)
    G = jnp.cumsum(_chunk(g), axis=-1)
    idx = jnp.arange(B_CHUNK)
    lower = idx[:, None] >= idx[None, :]
    strict = idx[:, None] > idx[None, :]
    dg = G[..., :, None] - G[..., None, :]
    decay = jnp.where(lower, jnp.exp(jnp.where(lower, dg, 0.0)), 0.0)
    kk = jnp.einsum('nbhid,nbhjd->nbhij', k, k)
    t_mat = jnp.where(strict, beta[..., :, None] * kk * decay, 0.0) + jnp.eye(B_CHUNK, dtype=jnp.float32)
    u = lax.linalg.triangular_solve(t_mat, beta[..., None] * v, left_side=True, lower=True)
    w = lax.linalg.triangular_solve(t_mat, (beta * jnp.exp(G))[..., None] * k, left_side=True, lower=True)
    qk = jnp.einsum('nbhid,nbhjd->nbhij', q, k) * decay
    q_dec = q * jnp.exp(G)[..., None]
    k_dec = k * jnp.exp(G[..., -1:] - G)[..., None]
    g_last = jnp.exp(G[..., -1])
    return (u, w, qk, q_dec, k_dec, g_last)


def _gdn_scan(chunks, s0):
    def step(s, xs):
        u, w, qk, q_dec, k_dec, g_last = xs
        v_new = u - jnp.einsum('bhcd,bhde->bhce', w, s)
        o = jnp.einsum('bhcd,bhde->bhce', q_dec, s) + jnp.einsum('bhij,bhje->bhie', qk, v_new)
        s = s * g_last[..., None, None] + jnp.einsum('bhcd,bhce->bhde', k_dec, v_new)
        return s, o
    s, o = lax.scan(step, s0, chunks)
    return o, s


def _gdn_direction(q_c, k_c, v_c, b_c, g_c, q_l, k_l, v_l, b_l, g_l, reverse):
    def flip(t):
        return jnp.flip(t, axis=1) if reverse else t
    s0 = jnp.zeros((q_c.shape[0], B_HEADS, B_DK, B_DV), jnp.float32)
    o_c, s_c = _gdn_scan(_gdn_prepare(flip(q_c), flip(k_c), flip(v_c), flip(b_c), flip(g_c)), s0)
    o_l, _ = _gdn_scan(_gdn_prepare(flip(q_l), flip(k_l), flip(v_l), flip(b_l), flip(g_l)), s_c)
    return flip(_unchunk(o_c)), flip(_unchunk(o_l))


def _gdn_bidirectional(ctx_in, lat_in):
    q_c, k_c, v_c, beta_c, g_c = ctx_in
    q_l, k_l, v_l, beta_l, g_l = lat_in
    o_c, o_l = 0.0, 0.0
    for d in range(2):
        oc, ol = _gdn_direction(q_c, k_c, v_c, beta_c[:, :, d], g_c[:, :, d],
                                q_l, k_l, v_l, beta_l[:, :, d], g_l[:, :, d], d == 1)
        o_c = o_c + oc
        o_l = o_l + ol
    return o_c, o_l


def _even_prep(h, w_in, conv_w, a_log, dt_bias, tabs):
    bsz, n, _ = h.shape
    qa, ka, va, qkv, gate, b_raw, a_raw = _split(h @ w_in, EVEN_SPLIT)
    qa = qa.reshape(bsz, n, 2 * A_HEADS, A_HD)
    ka = ka.reshape(bsz, n, 2 * A_HEADS, A_HD)
    if tabs is not None:
        qa = _rope_2d(qa, tabs)
        ka = _rope_2d(ka, tabs)
    qa = qa.reshape(bsz, n, A_HEADS, 2, A_HD)
    ka = ka.reshape(bsz, n, A_HEADS, 2, A_HD)
    va = va.reshape(bsz, n, A_HEADS, A_VD)
    qkv = jax.nn.silu(_dwconv(qkv, conv_w))
    qb, kb, vb = _split(qkv, (B_W, B_W, B_W))
    qb = _l2norm(qb.reshape(bsz, n, B_HEADS, B_DK))
    kb = _l2norm(kb.reshape(bsz, n, B_HEADS, B_DK))
    vb = vb.reshape(bsz, n, B_HEADS, B_DV).astype(jnp.float32)
    beta = jax.nn.sigmoid(b_raw.astype(jnp.float32)).reshape(bsz, n, 2, B_HEADS)
    g = -jnp.exp(a_log.astype(jnp.float32)) * jax.nn.softplus(
        a_raw.astype(jnp.float32).reshape(bsz, n, 2, B_HEADS) + dt_bias.astype(jnp.float32))
    gate = gate.reshape(bsz, n, B_HEADS, B_DV)
    return (qa, ka, va), (qb, kb, vb, beta, g, gate)


def _even_mixer(u_lat, u_ctx, tabs, w_in, w_out, lam_vec, subln_g, conv_w, a_log, dt_bias, out_g,
                lam_init, with_ctx_out):
    (qa_l, ka_l, va_l), gdn_l = _even_prep(u_lat, w_in, conv_w, a_log, dt_bias, tabs)
    (qa_c, ka_c, va_c), gdn_c = _even_prep(u_ctx, w_in, conv_w, a_log, dt_bias, None)
    lv = lam_vec.astype(jnp.float32)
    lam = jnp.exp(jnp.sum(lv[0] * lv[1])) - jnp.exp(jnp.sum(lv[2] * lv[3])) + lam_init
    oa_l, oa_c = _diff_attention(qa_l, ka_l, va_l, qa_c, ka_c, va_c, lam, with_ctx_out)
    ob_c, ob_l = _gdn_bidirectional(gdn_c[:5], gdn_l[:5])

    def merge(oa, ob, gate, ref):
        bsz, n = ref.shape[:2]
        ya = (_rmsnorm(oa, subln_g) * (1.0 - lam_init)).reshape(bsz, n, A_V)
        yb = (_rmsnorm(ob, out_g) * jax.nn.silu(gate)).reshape(bsz, n, B_W)
        return jnp.concatenate([ya.astype(ref.dtype), yb.astype(ref.dtype)], axis=-1) @ w_out

    y_lat = merge(oa_l, ob_l, gdn_l[5], u_lat)
    y_ctx = merge(oa_c, ob_c, gdn_c[5], u_ctx) if with_ctx_out else None
    return y_lat, y_ctx


def _rglru_coeffs(xc, wa, ba, wx, bx, lam):
    bsz, L, _ = xc.shape
    xb = xc.reshape(bsz, L, C_BLOCKS, C_BD)
    r = jax.nn.sigmoid(jnp.einsum('blhi,hij->blhj', xb, wa.astype(jnp.float32)).reshape(bsz, L, C_WIDTH)
                       + ba.astype(jnp.float32))
    i = jax.nn.sigmoid(jnp.einsum('blhi,hij->blhj', xb, wx.astype(jnp.float32)).reshape(bsz, L, C_WIDTH)
                       + bx.astype(jnp.float32))
    log_a = -C_POW * r * jax.nn.softplus(-lam.astype(jnp.float32))
    a = jnp.exp(log_a)
    b = jnp.sqrt(-jnp.expm1(2.0 * log_a)) * (i * xc)
    return a, b


def _combine(left, right):
    a_l, b_l = left
    a_r, b_r = right
    return a_l * a_r, a_r * b_l + b_r


def _linear_scan(a, b, h0):
    b = b.at[:, 0].add(a[:, 0] * h0)
    _, h = lax.associative_scan(_combine, (a, b), axis=1)
    return h


def _rglru_direction(x_c, x_l, wa, ba, wx, bx, lam, reverse):
    def flip(t):
        return jnp.flip(t, axis=1) if reverse else t
    a_c, b_c = _rglru_coeffs(flip(x_c), wa, ba, wx, bx, lam)
    h_c = _linear_scan(a_c, b_c, jnp.zeros_like(a_c[:, 0]))
    a_l, b_l = _rglru_coeffs(flip(x_l), wa, ba, wx, bx, lam)
    h_l = _linear_scan(a_l, b_l, h_c[:, -1])
    return flip(h_c), flip(h_l)


def _sink_softmax(s, sink):
    sk = sink[None, :, :, None, None]
    m = jnp.maximum(jnp.max(s, axis=-1, keepdims=True), sk)
    e = jnp.exp(s - m)
    return e / (jnp.sum(e, axis=-1, keepdims=True) + jnp.exp(sk - m))


def _window_attention(q_l, k_l, v_l, q_c, k_c, v_c, sink, with_ctx_out):
    bsz, n = q_l.shape[:2]
    lc = k_c.shape[1]
    sink = sink.astype(jnp.float32).reshape(D_KV, D_GROUP)

    def attend(q, k, v, mask):
        s = jnp.einsum('bqhgd,bkhd->bhgqk', q, k).astype(jnp.float32) * (D_HD ** -0.5)
        if mask is not None:
            s = jnp.where(mask, s, NEG_INF)
        p = _sink_softmax(s, sink)
        return jnp.einsum('bhgqk,bkhd->bqhgd', p.astype(v.dtype), v)

    q_l = q_l.reshape(bsz, n, D_KV, D_GROUP, D_HD)
    pad = ((0, 0), (D_BLOCK, D_BLOCK), (0, 0), (0, 0))
    kp = jnp.pad(k_l, pad)
    vp = jnp.pad(v_l, pad)

    def block(i):
        qi = lax.dynamic_slice_in_dim(q_l, i * D_BLOCK, D_BLOCK, axis=1)
        ki = lax.dynamic_slice_in_dim(kp, i * D_BLOCK, 3 * D_BLOCK, axis=1)
        vi = lax.dynamic_slice_in_dim(vp, i * D_BLOCK, 3 * D_BLOCK, axis=1)
        qpos = i * D_BLOCK + jnp.arange(D_BLOCK)
        kpos = (i - 1) * D_BLOCK + jnp.arange(3 * D_BLOCK)
        rel = kpos[None, :] - qpos[:, None]
        local = (jnp.abs(rel) <= WINDOW) & (kpos[None, :] >= 0) & (kpos[None, :] < n)
        mask = jnp.concatenate([jnp.ones((D_BLOCK, lc), bool), local], axis=-1)
        return attend(qi, jnp.concatenate([k_c, ki], axis=1), jnp.concatenate([v_c, vi], axis=1), mask)

    o = lax.map(block, jnp.arange(n // D_BLOCK))
    o_l = jnp.moveaxis(o, 0, 1).reshape(bsz, n, D_HEADS * D_HD)
    o_c = None
    if with_ctx_out:
        o_c = attend(q_c.reshape(bsz, lc, D_KV, D_GROUP, D_HD), k_c, v_c, None).reshape(bsz, lc, D_HEADS * D_HD)
    return o_l, o_c


def _odd_prep(h, w_in, conv_w, conv_b, tabs):
    bsz, n, _ = h.shape
    xr, gate, qd, kd, vd = _split(h @ w_in, ODD_SPLIT)
    xr = (_dwconv(xr, conv_w) + conv_b).astype(jnp.float32)
    qd = qd.reshape(bsz, n, D_HEADS, D_HD)
    kd = kd.reshape(bsz, n, D_KV, D_HD)
    vd = vd.reshape(bsz, n, D_KV, D_HD)
    if tabs is not None:
        qd = _rope_2d(qd, tabs)
        kd = _rope_2d(kd, tabs)
    return xr, gate, qd, kd, vd


def _odd_mixer(u_lat, u_ctx, tabs, w_in, w_out, conv_w, conv_b, wa, ba, wx, bx, lam, sink, with_ctx_out):
    xr_l, gate_l, qd_l, kd_l, vd_l = _odd_prep(u_lat, w_in, conv_w, conv_b, tabs)
    xr_c, gate_c, qd_c, kd_c, vd_c = _odd_prep(u_ctx, w_in, conv_w, conv_b, None)
    h_c, h_l = 0.0, 0.0
    for d in range(2):
        hc, hl = _rglru_direction(xr_c, xr_l, wa[d], ba[d], wx[d], bx[d], lam[d], d == 1)
        h_c = h_c + hc
        h_l = h_l + hl
    od_l, od_c = _window_attention(qd_l, kd_l, vd_l, qd_c, kd_c, vd_c, sink, with_ctx_out)

    def merge(h, gate, od, ref):
        yc = (h * jax.nn.gelu(gate.astype(jnp.float32))).astype(ref.dtype)
        return jnp.concatenate([yc, od.astype(ref.dtype)], axis=-1) @ w_out

    y_lat = merge(h_l, gate_l, od_l, u_lat)
    y_ctx = merge(h_c, gate_c, od_c, u_ctx) if with_ctx_out else None
    return y_lat, y_ctx


def _conv_ffn(h, w_up, conv_w, w_down):
    u = _dwconv(h @ w_up, conv_w)
    gate, val = jnp.split(u, 2, axis=-1)
    return (jax.nn.silu(gate) * val) @ w_down


def setup_inputs(seed: int = 0) -> dict:
    key = jax.random.key(seed)
    ks = iter(jax.random.split(key, 40))

    def nrm(shape, scale):
        return jax.random.normal(next(ks), shape, jnp.float32) * scale

    def unif(shape, lo, hi):
        return jax.random.uniform(next(ks), shape, jnp.float32, lo, hi)

    d = D_MODEL
    dt = jnp.exp(unif((N_EVEN, 2, B_HEADS), math.log(1e-3), math.log(1e-1)))
    a_lru = unif((N_ODD, 2, C_WIDTH), 0.9, 0.999)
    return {
        'x': nrm((BATCH, SEQ, d), 1.0),
        'c': nrm((BATCH, d), 1.0),
        'ctx': nrm((BATCH, CTX_LEN, d), 1.0),
        'c_ctx': nrm((d,), 1.0),
        'w_ada': nrm((DEPTH, d, N_MOD * d), d ** -0.5),
        'b_ada': nrm((DEPTH, N_MOD * d), 0.02),
        'norm_mix': 1.0 + nrm((DEPTH, d), 0.1),
        'norm_ffn': 1.0 + nrm((DEPTH, d), 0.1),
        'ffn_w_up': nrm((DEPTH, d, 2 * FFN), d ** -0.5),
        'ffn_conv': nrm((DEPTH, FFN_CONV, 2 * FFN), FFN_CONV ** -0.5),
        'ffn_w_down': nrm((DEPTH, FFN, d), FFN ** -0.5),
        'final_norm': 1.0 + nrm((d,), 0.1),
        'ev_w_in': nrm((N_EVEN, d, EVEN_IN), d ** -0.5),
        'ev_w_out': nrm((N_EVEN, EVEN_OUT, d), EVEN_OUT ** -0.5),
        'diff_lambda': nrm((N_EVEN, 4, A_HD), 0.1),
        'diff_subln': 1.0 + nrm((N_EVEN, A_VD), 0.1),
        'gdn_conv': nrm((N_EVEN, B_CONV, 3 * B_W), B_CONV ** -0.5),
        'gdn_a_log': jnp.log(unif((N_EVEN, 2, B_HEADS), 1.0, 16.0)),
        'gdn_dt_bias': dt + jnp.log(-jnp.expm1(-dt)),
        'gdn_norm': 1.0 + nrm((N_EVEN, B_DV), 0.1),
        'od_w_in': nrm((N_ODD, d, ODD_IN), d ** -0.5),
        'od_w_out': nrm((N_ODD, ODD_OUT, d), ODD_OUT ** -0.5),
        'lru_conv': nrm((N_ODD, C_CONV, C_WIDTH), C_CONV ** -0.5),
        'lru_conv_b': nrm((N_ODD, C_WIDTH), 0.02),
        'lru_wa': nrm((N_ODD, 2, C_BLOCKS, C_BD, C_BD), C_BD ** -0.5),
        'lru_ba': nrm((N_ODD, 2, C_WIDTH), 0.1),
        'lru_wx': nrm((N_ODD, 2, C_BLOCKS, C_BD, C_BD), C_BD ** -0.5),
        'lru_bx': nrm((N_ODD, 2, C_WIDTH), 0.1),
        'lru_lambda': jnp.log(a_lru) - jnp.log1p(-a_lru),
        'swa_sink': nrm((N_ODD, D_HEADS), 0.5),
    }


def reference(x, c, ctx, c_ctx, w_ada, b_ada, norm_mix, norm_ffn, ffn_w_up, ffn_conv, ffn_w_down, final_norm,
              ev_w_in, ev_w_out, diff_lambda, diff_subln, gdn_conv, gdn_a_log, gdn_dt_bias, gdn_norm,
              od_w_in, od_w_out, lru_conv, lru_conv_b, lru_wa, lru_ba, lru_wx, lru_bx, lru_lambda, swa_sink):
    rows = x.shape[1] // GRID_W
    tabs = _rope_tables(rows)
    x_lat, x_ctx = x, ctx
    s_lat = jax.nn.silu(c)
    s_ctx = jax.nn.silu(c_ctx)[None]
    for layer in range(DEPTH):
        ctx_out = layer < DEPTH - 1
        j = layer // 2
        m_lat = jnp.split((s_lat @ w_ada[layer] + b_ada[layer])[:, None, :], N_MOD, axis=-1)
        m_ctx = jnp.split((s_ctx @ w_ada[layer] + b_ada[layer])[:, None, :], N_MOD, axis=-1)
        u_lat = _modulate(x_lat, norm_mix[layer], m_lat[0], m_lat[1])
        u_ctx = _modulate(x_ctx, norm_mix[layer], m_ctx[0], m_ctx[1])
        if layer % 2 == 0:
            lam_init = 0.8 - 0.6 * math.exp(-0.3 * layer)
            y_lat, y_ctx = _even_mixer(u_lat, u_ctx, tabs, ev_w_in[j], ev_w_out[j], diff_lambda[j], diff_subln[j],
                                       gdn_conv[j], gdn_a_log[j], gdn_dt_bias[j], gdn_norm[j], lam_init, ctx_out)
        else:
            y_lat, y_ctx = _odd_mixer(u_lat, u_ctx, tabs, od_w_in[j], od_w_out[j], lru_conv[j], lru_conv_b[j],
                                      lru_wa[j], lru_ba[j], lru_wx[j], lru_bx[j], lru_lambda[j], swa_sink[j], ctx_out)
        x_lat = x_lat + m_lat[2] * y_lat
        x_lat = x_lat + m_lat[5] * _conv_ffn(_modulate(x_lat, norm_ffn[layer], m_lat[3], m_lat[4]),
                                             ffn_w_up[layer], ffn_conv[layer], ffn_w_down[layer])
        if ctx_out:
            x_ctx = x_ctx + m_ctx[2] * y_ctx
            x_ctx = x_ctx + m_ctx[5] * _conv_ffn(_modulate(x_ctx, norm_ffn[layer], m_ctx[3], m_ctx[4]),
                                                 ffn_w_up[layer], ffn_conv[layer], ffn_w_down[layer])
    return _rmsnorm(x_lat, final_norm)
```

```python
import functools
import math

import jax
import jax.numpy as jnp
import numpy as np
from jax import lax
from jax.experimental import pallas as pl
from jax.experimental.pallas import tpu as pltpu

F32 = jnp.float32
BF16 = jnp.bfloat16
HIGHEST = lax.Precision.HIGHEST

D_MODEL = 1024
GRID_W = 64
EPS = 1e-6
NEG_INF = -1e30
N_MOD = 6
ROPE_THETA = 10000.0
ROT_AXIS = 32
A_HEADS = 4
A_HD = 64
A_VD = 128
B_HEADS = 4
B_DK = 128
B_W = 512
B_CONV = 4
B_CHUNK = 64
C_WIDTH = 512
C_BLOCKS = 8
C_BD = 64
C_CONV = 4
C_POW = 8.0
D_HEADS = 8
D_KV = 2
D_HD = 64
WINDOW = 128
FFN = 2816
FFN_CONV = 3

TM = 256
TQ = 128
TT = 64
HALO = 8
FC = 256
MOD_ROWS = 16
VMEM_LIMIT = 56 * 1024 * 1024


def _cparams(sem):
    return pltpu.CompilerParams(dimension_semantics=sem, vmem_limit_bytes=VMEM_LIMIT)


def _sigmoid(x):
    return 1.0 / (1.0 + jnp.exp(-x))


def _silu(x):
    return x * _sigmoid(x)


def _softplus(x):
    return jnp.maximum(x, 0.0) + jnp.log(1.0 + jnp.exp(-jnp.abs(x)))


def _gelu_tanh(x):
    return 0.5 * x * (1.0 + jnp.tanh(math.sqrt(2.0 / math.pi) * (x + 0.044715 * (x * x * x))))


def _dot(a, b):
    return jnp.dot(a, b, preferred_element_type=F32)


def _dot_hi(a, b):
    return jnp.dot(a, b, preferred_element_type=F32, precision=HIGHEST)


def _dot_nt(a, b):
    return lax.dot_general(a, b, (((1,), (1,)), ((), ())), preferred_element_type=F32)


def _dot_tn(a, b):
    return lax.dot_general(a, b, (((0,), (0,)), ((), ())), preferred_element_type=F32)


def _rms(x, g):
    return x * lax.rsqrt(jnp.mean(x * x, axis=-1, keepdims=True) + EPS) * g


def _modulate(x, g, shift, scale):
    return _rms(x, g) * (1.0 + scale) + shift


def _rope128(z, c, sa, sb):
    return z * c + pltpu.roll(z, 112, 1) * sa + pltpu.roll(z, 16, 1) * sb


def _rope(z, c, sa, sb):
    n = z.shape[1] // 128
    return jnp.concatenate([_rope128(z[:, i * 128:(i + 1) * 128], c, sa, sb) for i in range(n)], axis=1)


def _mod_kernel(s_ref, w_ref, b_ref, o_ref):
    s = _silu(s_ref[...])
    o_ref[...] = _dot_hi(s, w_ref[...]) + b_ref[...]


def _modulation(c_all, w_ada, b_ada):
    depth, d, n = w_ada.shape
    tn = 1536
    return pl.pallas_call(
        _mod_kernel,
        grid=(depth, n // tn),
        in_specs=[pl.BlockSpec((MOD_ROWS, d), lambda l, j: (0, 0)),
                  pl.BlockSpec((None, d, tn), lambda l, j: (l, 0, j)),
                  pl.BlockSpec((None, 1, tn), lambda l, j: (l, 0, j))],
        out_specs=pl.BlockSpec((None, MOD_ROWS, tn), lambda l, j: (l, 0, j)),
        out_shape=jax.ShapeDtypeStruct((depth, MOD_ROWS, n), F32),
        compiler_params=_cparams(("arbitrary", "arbitrary")),
        name="modulation",
    )(c_all, w_ada, b_ada.reshape(depth, 1, n))


def _mod_spec(nb):
    return pl.BlockSpec((None, 1, N_MOD * D_MODEL), lambda b, j: (jnp.where(j == 0, nb, b), 0, 0))


def _full(shape):
    nd = len(shape)
    return pl.BlockSpec(shape, lambda *_: (0,) * nd)


def _even_in_kernel(x_ref, mod_ref, g_ref, c_ref, sa_ref, sb_ref, wa_ref, wg_ref, wgate_ref, wba_ref,
                    alog_ref, dtb_ref, q_ref, k_ref, v_ref, qkv_ref, gate_ref, bg_ref):
    d = D_MODEL
    mod = mod_ref[...]
    u = _modulate(x_ref[...], g_ref[...], mod[:, 0:d], mod[:, d:2 * d]).astype(BF16)
    c, sa, sb = c_ref[...], sa_ref[...], sb_ref[...]
    q = _dot(u, wa_ref[:, 0:512])
    q_ref[...] = (_rope(q, c, sa, sb) * (A_HD ** -0.5)).astype(BF16)
    k = _dot(u, wa_ref[:, 512:1024])
    k_ref[...] = _rope(k, c, sa, sb).astype(BF16)
    v_ref[...] = _dot(u, wa_ref[:, 1024:1536]).astype(BF16)
    qkv_ref[...] = _dot(u, wg_ref[...])
    gate_ref[...] = _dot(u, wgate_ref[...])
    z = _dot(u, wba_ref[...])
    lane = lax.broadcasted_iota(jnp.int32, z.shape, 1)
    beta = _sigmoid(z)
    gdec = -jnp.exp(alog_ref[...]) * _softplus(z + dtb_ref[...])
    bg_ref[...] = jnp.where(lane < 2 * B_HEADS, beta, jnp.where(lane < 4 * B_HEADS, gdec, 0.0))


def _even_in(x, mod, g, tabs, wa, wg, wgate, wba, alog, dtb):
    nb, t, d = x.shape
    nt = t // TM
    tile = lambda w: pl.BlockSpec((None, TM, w), lambda b, j: (b, j, 0))
    tab = pl.BlockSpec((TM, 128), lambda b, j: (j, 0))
    outs = [jax.ShapeDtypeStruct((nb, t, 512), BF16)] * 3 + [
        jax.ShapeDtypeStruct((nb, t, 1536), F32), jax.ShapeDtypeStruct((nb, t, 512), F32),
        jax.ShapeDtypeStruct((nb, t, 128), F32)]
    return pl.pallas_call(
        _even_in_kernel,
        grid=(nb, nt),
        in_specs=[tile(d), _mod_spec(nb), _full((1, d)), tab, tab, tab, _full(wa.shape), _full(wg.shape),
                  _full(wgate.shape), _full(wba.shape), _full((1, 128)), _full((1, 128))],
        out_specs=[tile(512), tile(512), tile(512), tile(1536), tile(512), tile(128)],
        out_shape=outs,
        compiler_params=_cparams(("parallel", "parallel")),
        name="even_in",
    )(x, mod, g, *tabs, wa, wg, wgate, wba, alog, dtb)


def _diff_attn_kernel(q_ref, k_ref, v_ref, lv_ref, g_ref, o_ref, *, lam_init, n_ctx):
    i = pl.program_id(2)
    lv = lv_ref[...]
    lam = (jnp.exp(jnp.sum(lv[0:1] * lv[1:2], axis=-1, keepdims=True))
           - jnp.exp(jnp.sum(lv[2:3] * lv[3:4], axis=-1, keepdims=True)) + lam_init)
    q = q_ref[...]
    lane = lax.broadcasted_iota(jnp.int32, q.shape, 1)
    zero = jnp.zeros_like(q)
    qz = jnp.concatenate([jnp.where(lane < A_HD, q, zero), jnp.where(lane >= A_HD, q, zero)], axis=0)

    def attend(k, v):
        s = _dot_nt(qz, k)
        m = jnp.max(s, axis=-1, keepdims=True)
        e = jnp.exp(s - m)
        r = 1.0 / jnp.sum(e, axis=-1, keepdims=True)
        w = e[0:TQ] * r[0:TQ] - e[TQ:2 * TQ] * (lam * r[TQ:2 * TQ])
        o = _dot(w.astype(BF16), v)
        o_ref[...] = (_rms(o, g_ref[...]) * (1.0 - lam_init)).astype(o_ref.dtype)

    @pl.when(i < n_ctx // TQ)
    def _():
        attend(k_ref[0:n_ctx, :], v_ref[0:n_ctx, :])

    @pl.when(i >= n_ctx // TQ)
    def _():
        attend(k_ref[...], v_ref[...])


def _diff_attn(q, k, v, lam_vec, subln, lam_init, n_ctx):
    nb, t, _ = q.shape
    kv = pl.BlockSpec((None, t, 128), lambda b, h, i: (b, 0, h))
    qo = pl.BlockSpec((None, TQ, 128), lambda b, h, i: (b, i, h))
    return pl.pallas_call(
        functools.partial(_diff_attn_kernel, lam_init=lam_init, n_ctx=n_ctx),
        grid=(nb, A_HEADS, t // TQ),
        in_specs=[qo, kv, kv, _full((4, A_HD)), _full((1, A_VD))],
        out_specs=qo,
        out_shape=jax.ShapeDtypeStruct((nb, t, A_HEADS * A_VD), BF16),
        compiler_params=_cparams(("parallel", "parallel", "arbitrary")),
        name="diff_attn",
    )(q, k, v, lam_vec, subln)


def _gdn_tile_index(j, nt, rev):
    return jnp.where(j == 0, 0, nt - j) if rev else j


def _gdn_kernel(qkv_ref, prev_ref, next_ref, bg_ref, cw_ref, o_ref, s_ref, ext_ref, vn_ref, *, rev, nt):
    j = pl.program_id(1)
    jj = _gdn_tile_index(j, nt, rev)
    dirn = 1 if rev else 0
    nch = TM // B_CHUNK

    @pl.when(j == 0)
    def _():
        s_ref[...] = jnp.zeros_like(s_ref)

    lflag = (jj >= 2).astype(F32)
    rflag = jnp.logical_and(jj >= 1, jj <= nt - 2).astype(F32)
    ext_ref[0:HALO, :] = prev_ref[...] * lflag
    ext_ref[HALO:HALO + TM, :] = qkv_ref[...]
    ext_ref[HALO + TM:2 * HALO + TM, :] = next_ref[...] * rflag
    acc = None
    for kk in range(B_CONV):
        term = cw_ref[kk:kk + 1, :] * ext_ref[pl.ds(HALO - 2 + kk, TM), :]
        acc = term if acc is None else acc + term
    act = _silu(acc)

    bg = bg_ref[...]
    ri = lax.broadcasted_iota(jnp.int32, (TM, TM), 0)
    ci = lax.broadcasted_iota(jnp.int32, (TM, TM), 1)
    same = (ri // B_CHUNK) == (ci // B_CHUNK)
    incl = jnp.logical_and(same, (ri <= ci) if rev else (ri >= ci))
    strict = jnp.logical_and(same, (ri < ci) if rev else (ri > ci))
    eye = (ri == ci).astype(F32)
    gcum = _dot_hi(incl.astype(F32), bg)
    gcum_t = gcum.T
    vn_ref[...] = jnp.zeros_like(vn_ref)

    for h in range(B_HEADS):
        def l2n(z):
            return z * lax.rsqrt(jnp.sum(z * z, axis=-1, keepdims=True) + EPS)
        q = l2n(act[:, h * 128:(h + 1) * 128]) * (B_DK ** -0.5)
        k = l2n(act[:, B_W + h * 128:B_W + (h + 1) * 128])
        v = act[:, 2 * B_W + h * 128:2 * B_W + (h + 1) * 128]
        cb = dirn * B_HEADS + h
        cg = 2 * B_HEADS + cb
        beta = bg[:, cb:cb + 1]
        gcol = gcum[:, cg:cg + 1]
        grow = gcum_t[cg:cg + 1, :]
        decay = jnp.where(incl, jnp.exp(jnp.where(incl, gcol - grow, 0.0)), 0.0)
        kb = k.astype(BF16)
        kkm = _dot_nt(kb, kb)
        qkm = _dot_nt(q.astype(BF16), kb) * decay
        a = jnp.where(strict, beta * kkm * decay, 0.0)
        p = eye - a
        pw = a
        for _ in range(5):
            pw = _dot_hi(pw, pw)
            p = p + _dot_hi(p, pw)
        eg = jnp.exp(gcol)
        uw = _dot_hi(p, jnp.concatenate([beta * v, (beta * eg) * k], axis=1))
        u, w = uw[:, 0:128], uw[:, 128:256]
        qdec = q * eg
        for step in range(nch):
            c = nch - 1 - step if rev else step
            r0, r1 = c * B_CHUNK, (c + 1) * B_CHUNK
            last = r0 if rev else r1 - 1
            glast = gcol[last:last + 1, :]
            kdec = k[r0:r1] * jnp.exp(glast - gcol[r0:r1])
            sh = s_ref[h]
            ws = _dot(jnp.concatenate([w[r0:r1], qdec[r0:r1]], axis=0).astype(BF16), sh.astype(BF16))
            vnew = u[r0:r1] - ws[0:B_CHUNK]
            vn_ref[r0:r1, :] = vnew
            o = ws[B_CHUNK:2 * B_CHUNK] + _dot(qkm[r0:r1].astype(BF16), vn_ref[...].astype(BF16))
            o_ref[r0:r1, h * 128:(h + 1) * 128] = o
            s_ref[h] = sh * jnp.exp(glast) + _dot_tn(kdec.astype(BF16), vnew.astype(BF16))


def _gdn(qkv, bg, conv_w, rev):
    nb, t, w = qkv.shape
    nt = t // TM
    hb = TM // HALO
    idx = lambda j: _gdn_tile_index(j, nt, rev)
    main = lambda ww: pl.BlockSpec((None, TM, ww), lambda b, j: (b, idx(j), 0))
    prev = pl.BlockSpec((None, HALO, w), lambda b, j: (b, jnp.maximum(idx(j) * hb - 1, 0), 0))
    nxt = pl.BlockSpec((None, HALO, w), lambda b, j: (b, jnp.minimum((idx(j) + 1) * hb, t // HALO - 1), 0))
    return pl.pallas_call(
        functools.partial(_gdn_kernel, rev=rev, nt=nt),
        grid=(nb, nt),
        in_specs=[main(w), prev, nxt, main(128), _full(conv_w.shape)],
        out_specs=main(B_W),
        out_shape=jax.ShapeDtypeStruct((nb, t, B_W), F32),
        scratch_shapes=[pltpu.VMEM((B_HEADS, B_DK, B_DK), F32), pltpu.VMEM((TM + 2 * HALO, w), F32),
                        pltpu.VMEM((TM, B_DK), F32)],
        compiler_params=_cparams(("parallel", "arbitrary")),
        name="gdn_rev" if rev else "gdn_fwd",
    )(qkv, qkv, qkv, bg, conv_w)


def _even_out_kernel(x_ref, mod_ref, ya_ref, of_ref, or_ref, gate_ref, og_ref, w_ref, o_ref):
    d = D_MODEL
    ob = of_ref[...] + or_ref[...]
    gate = gate_ref[...]
    yb = jnp.concatenate(
        [_rms(ob[:, h * 128:(h + 1) * 128], og_ref[...]) * _silu(gate[:, h * 128:(h + 1) * 128])
         for h in range(B_HEADS)], axis=1)
    y = _dot(ya_ref[...], w_ref[0:512, :]) + _dot(yb.astype(BF16), w_ref[512:1024, :])
    o_ref[...] = x_ref[...] + mod_ref[:, 2 * d:3 * d] * y


def _even_out(x, mod, ya, of, orv, gate, og, w):
    nb, t, d = x.shape
    tile = lambda ww: pl.BlockSpec((None, TM, ww), lambda b, j: (b, j, 0))
    return pl.pallas_call(
        _even_out_kernel,
        grid=(nb, t // TM),
        in_specs=[tile(d), _mod_spec(nb), tile(512), tile(512), tile(512), tile(512), _full((1, 128)),
                  _full(w.shape)],
        out_specs=tile(d),
        out_shape=jax.ShapeDtypeStruct(x.shape, F32),
        compiler_params=_cparams(("parallel", "parallel")),
        name="even_out",
    )(x, mod, ya, of, orv, gate, og, w)


def _ffn_kernel(x_ref, prev_ref, next_ref, mod_ref, g_ref, wup_ref, cw_ref, wdn_ref, o_ref,
                hg_ref, hv_ref, acc_ref, *, nt):
    d = D_MODEL
    j = pl.program_id(1)
    mod = mod_ref[...]
    shift, scale = mod[:, 3 * d:4 * d], mod[:, 4 * d:5 * d]
    g = g_ref[...]
    lflag = (j >= 2).astype(F32)
    rflag = jnp.logical_and(j >= 1, j <= nt - 2).astype(F32)
    x = x_ref[...]
    u = jnp.concatenate([_modulate(prev_ref[...], g, shift, scale) * lflag,
                         _modulate(x, g, shift, scale),
                         _modulate(next_ref[...], g, shift, scale) * rflag], axis=0).astype(BF16)
    for c in range(FFN // FC):
        c0 = c * FC
        hg_ref[...] = _dot(u, wup_ref[:, c0:c0 + FC])
        hv_ref[...] = _dot(u, wup_ref[:, FFN + c0:FFN + c0 + FC])
        cg = None
        cv = None
        for kk in range(FFN_CONV):
            tg = cw_ref[kk:kk + 1, c0:c0 + FC] * hg_ref[pl.ds(HALO - 1 + kk, TM), :]
            tv = cw_ref[kk:kk + 1, FFN + c0:FFN + c0 + FC] * hv_ref[pl.ds(HALO - 1 + kk, TM), :]
            cg = tg if cg is None else cg + tg
            cv = tv if cv is None else cv + tv
        part = _dot((_silu(cg) * cv).astype(BF16), wdn_ref[c0:c0 + FC, :])
        if c == 0:
            acc_ref[...] = part
        else:
            acc_ref[...] += part
    o_ref[...] = x + mod[:, 5 * d:6 * d] * acc_ref[...]


def _ffn(x, mod, g, wup, cw, wdn):
    nb, t, d = x.shape
    nt = t // TM
    hb = TM // HALO
    tile = pl.BlockSpec((None, TM, d), lambda b, j: (b, j, 0))
    prev = pl.BlockSpec((None, HALO, d), lambda b, j: (b, jnp.maximum(j * hb - 1, 0), 0))
    nxt = pl.BlockSpec((None, HALO, d), lambda b, j: (b, jnp.minimum((j + 1) * hb, t // HALO - 1), 0))
    resident = lambda shape: pl.BlockSpec(shape, lambda b, j: (0, 0), pipeline_mode=pl.Buffered(1))
    return pl.pallas_call(
        functools.partial(_ffn_kernel, nt=nt),
        grid=(nb, nt),
        in_specs=[tile, prev, nxt, _mod_spec(nb), _full((1, d)), resident(wup.shape), _full(cw.shape),
                  resident(wdn.shape)],
        out_specs=tile,
        out_shape=jax.ShapeDtypeStruct(x.shape, F32),
        scratch_shapes=[pltpu.VMEM((TM + 2 * HALO, FC), F32), pltpu.VMEM((TM + 2 * HALO, FC), F32),
                        pltpu.VMEM((TM, d), F32)],
        compiler_params=_cparams(("parallel", "parallel")),
        name="ffn",
    )(x, x, x, mod, g, wup, cw, wdn)


def _odd_in_kernel(x_ref, mod_ref, g_ref, c_ref, sa_ref, sb_ref, wr_ref, wq_ref, wkv_ref,
                   xr_ref, gate_ref, q_ref, k_ref, v_ref):
    d = D_MODEL
    mod = mod_ref[...]
    u = _modulate(x_ref[...], g_ref[...], mod[:, 0:d], mod[:, d:2 * d]).astype(BF16)
    c, sa, sb = c_ref[...], sa_ref[...], sb_ref[...]
    xr_ref[...] = _dot(u, wr_ref[:, 0:C_WIDTH])
    gate_ref[...] = _dot(u, wr_ref[:, C_WIDTH:2 * C_WIDTH])
    q_ref[...] = (_rope(_dot(u, wq_ref[...]), c, sa, sb) * (D_HD ** -0.5)).astype(BF16)
    k_ref[...] = _rope(_dot(u, wkv_ref[:, 0:256]), c, sa, sb).astype(BF16)
    v_ref[...] = _dot(u, wkv_ref[:, 256:512]).astype(BF16)


def _odd_in(x, mod, g, tabs, wr, wq, wkv):
    nb, t, d = x.shape
    tile = lambda w: pl.BlockSpec((None, TM, w), lambda b, j: (b, j, 0))
    tmaj = pl.BlockSpec((TM, C_WIDTH), lambda b, j: (j, b))
    tab = pl.BlockSpec((TM, 128), lambda b, j: (j, 0))
    outs = [jax.ShapeDtypeStruct((t, nb * C_WIDTH), F32)] * 2 + [
        jax.ShapeDtypeStruct((nb, t, 512), BF16), jax.ShapeDtypeStruct((nb, t, 256), BF16),
        jax.ShapeDtypeStruct((nb, t, 256), BF16)]
    return pl.pallas_call(
        _odd_in_kernel,
        grid=(nb, t // TM),
        in_specs=[tile(d), _mod_spec(nb), _full((1, d)), tab, tab, tab, _full(wr.shape), _full(wq.shape),
                  _full(wkv.shape)],
        out_specs=[tmaj, tmaj, tile(512), tile(256), tile(256)],
        out_shape=outs,
        compiler_params=_cparams(("parallel", "parallel")),
        name="odd_in",
    )(x, mod, g, *tabs, wr, wq, wkv)


def _lru_tile_index(j, ntt, nct, rev):
    return jnp.where(j < nct, nct - 1 - j, ntt + nct - 1 - j) if rev else j


def _lru_kernel(x_ref, prev_ref, next_ref, cw_ref, cb_ref, w_ref, b_ref, lam_ref, o_ref,
                h_ref, a_ref, bc_ref, *, rev, ntt, nct):
    j = pl.program_id(0)
    jj = _lru_tile_index(j, ntt, nct, rev)
    nb = x_ref.shape[1]

    @pl.when(j == 0)
    def _():
        h_ref[...] = jnp.zeros_like(h_ref)

    lflag = jnp.logical_and(jj != 0, jj != nct).astype(F32)
    rflag = jnp.logical_and(jj != nct - 1, jj != ntt - 1).astype(F32)
    ext = jnp.concatenate([prev_ref[...] * lflag, x_ref[...], next_ref[...] * rflag], axis=0)
    xc = cb_ref[...]
    for kk in range(C_CONV):
        xc = xc + cw_ref[kk:kk + 1, :] * ext[kk:kk + TT]
    xc2 = xc.reshape(TT * nb, C_WIDTH)
    z = _dot(xc2.astype(BF16), w_ref[...]) + b_ref[...]
    r = _sigmoid(z[:, 0:C_WIDTH])
    gi = _sigmoid(z[:, C_WIDTH:2 * C_WIDTH])
    log_a = -C_POW * r * _softplus(-lam_ref[...])
    a = jnp.exp(log_a)
    bc = jnp.sqrt(1.0 - jnp.exp(2.0 * log_a)) * (gi * xc2)
    a_ref[...] = a.reshape(TT, nb, C_WIDTH)
    bc_ref[...] = bc.reshape(TT, nb, C_WIDTH)

    def body(s, h):
        t = TT - 1 - s if rev else s
        h = a_ref[t] * h + bc_ref[t]
        o_ref[t] = h
        return h

    h_ref[...] = lax.fori_loop(0, TT, body, h_ref[...], unroll=8)


def _lru(xr, conv_w, conv_b, w, b, lam, n_ctx, rev):
    t, nb, c = xr.shape
    ntt, nct = t // TT, n_ctx // TT
    idx = lambda j: _lru_tile_index(j, ntt, nct, rev)
    main = pl.BlockSpec((TT, nb, c), lambda j: (idx(j), 0, 0))
    prev = pl.BlockSpec((2, nb, c), lambda j: (jnp.maximum(idx(j) * (TT // 2) - 1, 0), 0, 0))
    nxt = pl.BlockSpec((1, nb, c), lambda j: (jnp.minimum((idx(j) + 1) * TT, t - 1), 0, 0))
    return pl.pallas_call(
        functools.partial(_lru_kernel, rev=rev, ntt=ntt, nct=nct),
        grid=(ntt,),
        in_specs=[main, prev, nxt, _full(conv_w.shape), _full(conv_b.shape), _full(w.shape), _full(b.shape),
                  _full(lam.shape)],
        out_specs=main,
        out_shape=jax.ShapeDtypeStruct(xr.shape, F32),
        scratch_shapes=[pltpu.VMEM((nb, c), F32), pltpu.VMEM((TT, nb, c), F32), pltpu.VMEM((TT, nb, c), F32)],
        compiler_params=_cparams(("arbitrary",)),
        name="lru_rev" if rev else "lru_fwd",
    )(xr, xr, xr, conv_w, conv_b, w, b, lam)


def _win_attn_kernel(q_ref, k_ref, v_ref, sink_ref, o_ref, *, n_ctx, n_lat):
    i = pl.program_id(1)
    nctx_tiles = n_ctx // TQ
    nlb = n_lat // TQ
    grp = D_HEADS // D_KV
    lane = lax.broadcasted_iota(jnp.int32, (TQ, 128), 1)
    low = lane < D_HD

    def stacked_q(g):
        parts = []
        for sl in range(grp // 2):
            slab = q_ref[:, (g * (grp // 2) + sl) * 128:(g * (grp // 2) + sl + 1) * 128]
            zero = jnp.zeros_like(slab)
            parts += [jnp.where(low, slab, zero), jnp.where(low, zero, slab)]
        return jnp.concatenate(parts, axis=0)

    def sink_col(g):
        return jnp.concatenate(
            [jnp.broadcast_to(sink_ref[g * grp + hh:g * grp + hh + 1, 0:1], (TQ, 1)) for hh in range(grp)], axis=0)

    def store(g, o):
        for sl in range(grp // 2):
            a = o[(2 * sl) * TQ:(2 * sl + 1) * TQ]
            b = o[(2 * sl + 1) * TQ:(2 * sl + 2) * TQ]
            col = (g * (grp // 2) + sl) * 128
            o_ref[:, col:col + 128] = jnp.where(low, a, b).astype(o_ref.dtype)

    @pl.when(i < nctx_tiles)
    def _():
        for g in range(D_KV):
            qz, sk = stacked_q(g), sink_col(g)
            kc = k_ref[0:n_ctx, g * 128:(g + 1) * 128]
            vc = v_ref[0:n_ctx, g * 128:(g + 1) * 128]
            s = _dot_nt(qz, kc)
            m = jnp.maximum(jnp.max(s, axis=-1, keepdims=True), sk)
            e = jnp.exp(s - m)
            den = jnp.sum(e, axis=-1, keepdims=True) + jnp.exp(sk - m)
            store(g, _dot(e.astype(BF16), vc) / den)

    @pl.when(i >= nctx_tiles)
    def _():
        il = i - nctx_tiles
        kb = jnp.clip(il - 1, 0, nlb - 3)
        start = pl.multiple_of(n_ctx + kb * TQ, TQ)
        nloc = 3 * TQ
        qpos = il * TQ + (lax.broadcasted_iota(jnp.int32, (grp * TQ, nloc), 0) % TQ)
        kpos = kb * TQ + lax.broadcasted_iota(jnp.int32, (grp * TQ, nloc), 1)
        mask = jnp.abs(kpos - qpos) <= WINDOW
        for g in range(D_KV):
            qz, sk = stacked_q(g), sink_col(g)
            kc = k_ref[0:n_ctx, g * 128:(g + 1) * 128]
            vc = v_ref[0:n_ctx, g * 128:(g + 1) * 128]
            kl = k_ref[pl.ds(start, nloc), g * 128:(g + 1) * 128]
            vl = v_ref[pl.ds(start, nloc), g * 128:(g + 1) * 128]
            sc = _dot_nt(qz, kc)
            sl_ = jnp.where(mask, _dot_nt(qz, kl), NEG_INF)
            m = jnp.maximum(jnp.maximum(jnp.max(sc, axis=-1, keepdims=True),
                                        jnp.max(sl_, axis=-1, keepdims=True)), sk)
            ec = jnp.exp(sc - m)
            el = jnp.exp(sl_ - m)
            den = (jnp.sum(ec, axis=-1, keepdims=True) + jnp.sum(el, axis=-1, keepdims=True)
                   + jnp.exp(sk - m))
            store(g, (_dot(ec.astype(BF16), vc) + _dot(el.astype(BF16), vl)) / den)


def _win_attn(q, k, v, sink, n_ctx):
    nb, t, _ = q.shape
    kv = pl.BlockSpec((None, t, 256), lambda b, i: (b, 0, 0))
    qo = pl.BlockSpec((None, TQ, 512), lambda b, i: (b, i, 0))
    return pl.pallas_call(
        functools.partial(_win_attn_kernel, n_ctx=n_ctx, n_lat=t - n_ctx),
        grid=(nb, t // TQ),
        in_specs=[qo, kv, kv, _full(sink.shape)],
        out_specs=qo,
        out_shape=jax.ShapeDtypeStruct((nb, t, D_HEADS * D_HD), BF16),
        compiler_params=_cparams(("parallel", "arbitrary")),
        name="win_attn",
    )(q, k, v, sink)


def _odd_out_kernel(x_ref, mod_ref, hf_ref, hr_ref, gate_ref, od_ref, w_ref, o_ref):
    d = D_MODEL
    yc = (hf_ref[...] + hr_ref[...]) * _gelu_tanh(gate_ref[...])
    y = _dot(yc.astype(BF16), w_ref[0:512, :]) + _dot(od_ref[...], w_ref[512:1024, :])
    o_ref[...] = x_ref[...] + mod_ref[:, 2 * d:3 * d] * y


def _odd_out(x, mod, hf, hr, gate, od, w):
    nb, t, d = x.shape
    tile = lambda ww: pl.BlockSpec((None, TM, ww), lambda b, j: (b, j, 0))
    tmaj = pl.BlockSpec((TM, C_WIDTH), lambda b, j: (j, b))
    return pl.pallas_call(
        _odd_out_kernel,
        grid=(nb, t // TM),
        in_specs=[tile(d), _mod_spec(nb), tmaj, tmaj, tmaj, tile(512), _full(w.shape)],
        out_specs=tile(d),
        out_shape=jax.ShapeDtypeStruct(x.shape, F32),
        compiler_params=_cparams(("parallel", "parallel")),
        name="odd_out",
    )(x, mod, hf, hr, gate, od, w)


def _final_kernel(x_ref, g_ref, o_ref):
    o_ref[...] = _rms(x_ref[...], g_ref[...])


def _final(x, g, n_ctx):
    nb, t, d = x.shape
    off = n_ctx // TM
    return pl.pallas_call(
        _final_kernel,
        grid=(nb, (t - n_ctx) // TM),
        in_specs=[pl.BlockSpec((None, TM, d), lambda b, j: (b, j + off, 0)), _full((1, d))],
        out_specs=pl.BlockSpec((None, TM, d), lambda b, j: (b, j, 0)),
        out_shape=jax.ShapeDtypeStruct((nb, t - n_ctx, d), F32),
        compiler_params=_cparams(("parallel", "parallel")),
        name="final_norm",
    )(x, g)


def _rope_tables(n_ctx, n_lat):
    pos = np.arange(n_lat)
    inv = ROPE_THETA ** (-np.arange(0, ROT_AXIS, 2, dtype=np.float64) / ROT_AXIS)
    ang_r = (pos // GRID_W)[:, None] * inv
    ang_c = (pos % GRID_W)[:, None] * inv
    lane = np.arange(128) % 64
    seg, f = lane // 16, lane % 16
    ang = np.where(seg[None, :] < 2, ang_r[:, f], ang_c[:, f])
    c = np.cos(ang)
    s = np.sin(ang)
    sa = np.where((seg % 2 == 0)[None, :], -s, 0.0)
    sb = np.where((seg % 2 == 1)[None, :], s, 0.0)
    pad = lambda a, fill: np.concatenate([np.full((n_ctx, 128), fill), a], axis=0).astype(np.float32)
    return jnp.asarray(pad(c, 1.0)), jnp.asarray(pad(sa, 0.0)), jnp.asarray(pad(sb, 0.0))


def _block_diag(w):
    eye = jnp.eye(C_BLOCKS, dtype=w.dtype)
    return jnp.einsum('hij,hg->higj', w, eye).reshape(C_WIDTH, C_WIDTH)


def kernel(x, c, ctx, c_ctx, w_ada, b_ada, norm_mix, norm_ffn, ffn_w_up, ffn_conv, ffn_w_down, final_norm,
           ev_w_in, ev_w_out, diff_lambda, diff_subln, gdn_conv, gdn_a_log, gdn_dt_bias, gdn_norm,
           od_w_in, od_w_out, lru_conv, lru_conv_b, lru_wa, lru_ba, lru_wx, lru_bx, lru_lambda, swa_sink):
    nb, n_lat, d = x.shape
    n_ctx = ctx.shape[1]
    depth = w_ada.shape[0]
    assert d == D_MODEL and n_ctx == TM and n_lat % TM == 0 and n_lat // TQ >= 3 and nb < MOD_ROWS
    t = n_ctx + n_lat

    xa = jnp.concatenate([ctx, x], axis=1)
    c_all = jnp.zeros((MOD_ROWS, d), F32).at[0:nb].set(c).at[nb].set(c_ctx)
    mod_all = _modulation(c_all, w_ada, b_ada).reshape(depth, MOD_ROWS, 1, N_MOD * d)
    tabs = _rope_tables(n_ctx, n_lat)
    row = lambda v: v.reshape(1, -1).astype(F32)

    for layer in range(depth):
        jx = layer // 2
        mod = mod_all[layer]
        if layer % 2 == 0:
            lam_init = 0.8 - 0.6 * math.exp(-0.3 * layer)
            w_in = ev_w_in[jx]
            wa = w_in[:, 0:1536].astype(BF16)
            wg = w_in[:, 1536:3072].astype(BF16)
            wgate = w_in[:, 3072:3584].astype(BF16)
            wba = jnp.pad(w_in[:, 3584:3600], ((0, 0), (0, 112))).astype(BF16)
            pad16 = lambda v: jnp.pad(v.reshape(1, 8).astype(F32), ((0, 0), (8, 112)))
            q, k, v, qkv, gate, bg = _even_in(xa, mod, row(norm_mix[layer]), tabs, wa, wg, wgate, wba,
                                              pad16(gdn_a_log[jx]), pad16(gdn_dt_bias[jx]))
            ya = _diff_attn(q, k, v, diff_lambda[jx].astype(F32), row(diff_subln[jx]), lam_init, n_ctx)
            of = _gdn(qkv, bg, gdn_conv[jx].astype(F32), rev=False)
            orv = _gdn(qkv, bg, gdn_conv[jx].astype(F32), rev=True)
            xa = _even_out(xa, mod, ya, of, orv, gate, row(gdn_norm[jx]), ev_w_out[jx].astype(BF16))
        else:
            w_in = od_w_in[jx]
            wr = w_in[:, 0:1024].astype(BF16)
            wq = w_in[:, 1024:1536].astype(BF16)
            dup = lambda w: jnp.concatenate([w[:, 0:64], w[:, 0:64], w[:, 64:128], w[:, 64:128]], axis=1)
            wkv = jnp.concatenate([dup(w_in[:, 1536:1664]), dup(w_in[:, 1664:1792])], axis=1).astype(BF16)
            xr, gate, q, k, v = _odd_in(xa, mod, row(norm_mix[layer]), tabs, wr, wq, wkv)
            xr3 = xr.reshape(t, nb, C_WIDTH)
            hs = []
            for dd in range(2):
                wbig = jnp.concatenate([_block_diag(lru_wa[jx, dd]), _block_diag(lru_wx[jx, dd])], axis=1)
                bbig = jnp.concatenate([lru_ba[jx, dd], lru_bx[jx, dd]]).reshape(1, -1).astype(F32)
                hs.append(_lru(xr3, lru_conv[jx].astype(F32), row(lru_conv_b[jx]), wbig.astype(BF16), bbig,
                               row(lru_lambda[jx, dd]), n_ctx, rev=(dd == 1)).reshape(t, nb * C_WIDTH))
            sink = jnp.broadcast_to(swa_sink[jx].astype(F32)[:, None], (D_HEADS, 128))
            od = _win_attn(q, k, v, sink, n_ctx)
            xa = _odd_out(xa, mod, hs[0], hs[1], gate, od, od_w_out[jx].astype(BF16))
        xa = _ffn(xa, mod, row(norm_ffn[layer]), ffn_w_up[layer].astype(BF16), ffn_conv[layer].astype(F32),
                  ffn_w_down[layer].astype(BF16))
    return _final(xa, row(final_norm), n_ctx)
```

```python
import functools
import math

import jax
import jax.numpy as jnp
import numpy as np
from jax import lax
from jax.experimental import pallas as pl
from jax.experimental.pallas import tpu as pltpu

F32 = jnp.float32
BF16 = jnp.bfloat16
HIGHEST = lax.Precision.HIGHEST

D_MODEL = 1024
GRID_W = 64
EPS = 1e-6
NEG_INF = -1e30
N_MOD = 6
ROPE_THETA = 10000.0
ROT_AXIS = 32
A_HEADS = 4
A_HD = 64
A_VD = 128
B_HEADS = 4
B_DK = 128
B_W = 512
B_CONV = 4
B_CHUNK = 64
C_WIDTH = 512
C_BLOCKS = 8
C_BD = 64
C_CONV = 4
C_POW = 8.0
D_HEADS = 8
D_KV = 2
D_HD = 64
WINDOW = 128
FFN = 2816
FFN_CONV = 3

TM = 256
TQ = 128
TQA = 256
KB = 256
ONES_ROWS = 16
LOG2E = math.log2(math.e)
TT = 64
HALO = 8
FC = 256
TTF = 64
MOD_ROWS = 16
VMEM_LIMIT = 56 * 1024 * 1024


def _cparams(sem):
    return pltpu.CompilerParams(dimension_semantics=sem, vmem_limit_bytes=VMEM_LIMIT)


def _sigmoid(x):
    return 1.0 / (1.0 + jnp.exp(-x))


def _silu(x):
    return x * _sigmoid(x)


def _softplus(x):
    return jnp.maximum(x, 0.0) + jnp.log(1.0 + jnp.exp(-jnp.abs(x)))


def _gelu_tanh(x):
    return 0.5 * x * (1.0 + jnp.tanh(math.sqrt(2.0 / math.pi) * (x + 0.044715 * (x * x * x))))


def _dot(a, b):
    return jnp.dot(a, b, preferred_element_type=F32)


def _dot_hi(a, b):
    return jnp.dot(a, b, preferred_element_type=F32, precision=HIGHEST)


def _dot3(a, b):
    ah = a.astype(BF16)
    al = (a - ah.astype(F32)).astype(BF16)
    bh = b.astype(BF16)
    bl = (b - bh.astype(F32)).astype(BF16)
    return _dot(ah, bh) + (_dot(ah, bl) + _dot(al, bh))


def _dot_nt(a, b):
    return lax.dot_general(a, b, (((1,), (1,)), ((), ())), preferred_element_type=F32)


def _dot_tn(a, b):
    return lax.dot_general(a, b, (((0,), (0,)), ((), ())), preferred_element_type=F32)


def _rms(x, g):
    return x * lax.rsqrt(jnp.mean(x * x, axis=-1, keepdims=True) + EPS) * g


def _modulate(x, g, shift, scale):
    return _rms(x, g) * (1.0 + scale) + shift


def _rope128(z, c, sa, sb):
    return z * c + pltpu.roll(z, 112, 1) * sa + pltpu.roll(z, 16, 1) * sb


def _rope(z, c, sa, sb):
    n = z.shape[1] // 128
    return jnp.concatenate([_rope128(z[:, i * 128:(i + 1) * 128], c, sa, sb) for i in range(n)], axis=1)


def _mod_kernel(s_ref, w_ref, b_ref, o_ref):
    s = _silu(s_ref[...])
    o_ref[...] = _dot_hi(s, w_ref[...]) + b_ref[...]


def _modulation(c_all, w_ada, b_ada):
    depth, d, n = w_ada.shape
    tn = 1536
    return pl.pallas_call(
        _mod_kernel,
        grid=(depth, n // tn),
        in_specs=[pl.BlockSpec((MOD_ROWS, d), lambda l, j: (0, 0)),
                  pl.BlockSpec((None, d, tn), lambda l, j: (l, 0, j)),
                  pl.BlockSpec((None, 1, tn), lambda l, j: (l, 0, j))],
        out_specs=pl.BlockSpec((None, MOD_ROWS, tn), lambda l, j: (l, 0, j)),
        out_shape=jax.ShapeDtypeStruct((depth, MOD_ROWS, n), F32),
        compiler_params=_cparams(("arbitrary", "arbitrary")),
        name="modulation",
    )(c_all, w_ada, b_ada.reshape(depth, 1, n))


def _mod_spec(nb):
    return pl.BlockSpec((None, 1, N_MOD * D_MODEL), lambda b, j: (jnp.where(j == 0, nb, b), 0, 0))


def _full(shape):
    nd = len(shape)
    return pl.BlockSpec(shape, lambda *_: (0,) * nd)


def _even_in_kernel(x_ref, mod_ref, g_ref, c_ref, sa_ref, sb_ref, wa_ref, wg_ref, wgate_ref, wba_ref,
                    alog_ref, dtb_ref, q_ref, k_ref, v_ref, qkv_ref, gate_ref, bg_ref):
    d = D_MODEL
    mod = mod_ref[...]
    u = _modulate(x_ref[...], g_ref[...], mod[:, 0:d], mod[:, d:2 * d]).astype(BF16)
    c, sa, sb = c_ref[...], sa_ref[...], sb_ref[...]
    q = _dot(u, wa_ref[:, 0:512])
    q_ref[...] = (_rope(q, c, sa, sb) * (A_HD ** -0.5 * LOG2E)).astype(BF16)
    k = _dot(u, wa_ref[:, 512:1024])
    k_ref[...] = _rope(k, c, sa, sb).astype(BF16)
    v_ref[...] = _dot(u, wa_ref[:, 1024:1536]).astype(BF16)
    qkv_ref[...] = _dot(u, wg_ref[...])
    gate_ref[...] = _dot(u, wgate_ref[...])
    z = _dot(u, wba_ref[...])
    lane = lax.broadcasted_iota(jnp.int32, z.shape, 1)
    beta = _sigmoid(z)
    gdec = -jnp.exp(alog_ref[...]) * _softplus(z + dtb_ref[...])
    bg_ref[...] = jnp.where(lane < 2 * B_HEADS, beta, jnp.where(lane < 4 * B_HEADS, gdec, 0.0))


def _even_in(x, mod, g, tabs, wa, wg, wgate, wba, alog, dtb):
    nb, t, d = x.shape
    nt = t // TM
    tile = lambda w: pl.BlockSpec((None, TM, w), lambda b, j: (b, j, 0))
    tab = pl.BlockSpec((TM, 128), lambda b, j: (j, 0))
    outs = [jax.ShapeDtypeStruct((nb, t, 512), BF16)] * 3 + [
        jax.ShapeDtypeStruct((nb, t, 1536), F32), jax.ShapeDtypeStruct((nb, t, 512), F32),
        jax.ShapeDtypeStruct((nb, t, 128), F32)]
    return pl.pallas_call(
        _even_in_kernel,
        grid=(nb, nt),
        in_specs=[tile(d), _mod_spec(nb), _full((1, d)), tab, tab, tab, _full(wa.shape), _full(wg.shape),
                  _full(wgate.shape), _full(wba.shape), _full((1, 128)), _full((1, 128))],
        out_specs=[tile(512), tile(512), tile(512), tile(1536), tile(512), tile(128)],
        out_shape=outs,
        compiler_params=_cparams(("parallel", "parallel")),
        name="even_in",
    )(x, mod, g, *tabs, wa, wg, wgate, wba, alog, dtb)


def _diff_attn_kernel(q_ref, k_ref, v_ref, lv_ref, g_ref, o_ref, vt_ref, s_ref, e_ref, *, lam_init, n_ctx):
    i = pl.program_id(2)
    t = k_ref.shape[0]

    @pl.when(i == 0)
    def _():
        vt_ref[0:A_VD, :] = v_ref[...].astype(F32).T.astype(BF16)
        orow = lax.broadcasted_iota(jnp.int32, (ONES_ROWS, t), 0)
        vt_ref[A_VD:A_VD + ONES_ROWS, :] = jnp.where(orow == 0, 1.0, 0.0).astype(BF16)

    lv = lv_ref[...]
    lam = (jnp.exp(jnp.sum(lv[0:1] * lv[1:2], axis=-1, keepdims=True))
           - jnp.exp(jnp.sum(lv[2:3] * lv[3:4], axis=-1, keepdims=True)) + lam_init)
    q = q_ref[...]
    lane = lax.broadcasted_iota(jnp.int32, q.shape, 1)
    zero = jnp.zeros_like(q)
    qz = jnp.concatenate([jnp.where(lane < A_HD, q, zero), jnp.where(lane >= A_HD, q, zero)], axis=0)

    def scores(kb):
        return _dot_nt(k_ref[kb * KB:(kb + 1) * KB, :], qz)

    def attend(nk):
        nblk = nk // KB
        mrun = None
        for kb in range(nblk):
            sblk = scores(kb)
            s_ref[kb * KB:(kb + 1) * KB, :] = sblk
            part = jnp.max(sblk.reshape(KB // 8, 8, 2 * TQA), axis=0)
            mrun = part if mrun is None else jnp.maximum(mrun, part)
        m = jnp.max(mrun, axis=0, keepdims=True)
        for kb in range(nblk):
            e_ref[kb * KB:(kb + 1) * KB, :] = jnp.exp2(s_ref[kb * KB:(kb + 1) * KB, :] - m).astype(BF16)
        oe = _dot(vt_ref[:, 0:nk], e_ref[0:nk, :])
        on = oe[0:A_VD] / oe[A_VD:A_VD + 1]
        od = on[:, 0:TQA] - lam * on[:, TQA:2 * TQA]
        y = od * lax.rsqrt(jnp.mean(od * od, axis=0, keepdims=True) + EPS) * (g_ref[...] * (1.0 - lam_init))
        o_ref[...] = y.T.astype(o_ref.dtype)

    @pl.when(i < n_ctx // TQA)
    def _():
        attend(n_ctx)

    @pl.when(i >= n_ctx // TQA)
    def _():
        attend(t)


def _diff_attn(q, k, v, lam_vec, subln, lam_init, n_ctx):
    nb, t, _ = q.shape
    kv = pl.BlockSpec((None, t, 128), lambda b, h, i: (b, 0, h))
    qo = pl.BlockSpec((None, TQA, 128), lambda b, h, i: (b, i, h))
    return pl.pallas_call(
        functools.partial(_diff_attn_kernel, lam_init=lam_init, n_ctx=n_ctx),
        grid=(nb, A_HEADS, t // TQA),
        in_specs=[qo, kv, kv, _full((4, A_HD)), _full((A_VD, 1))],
        out_specs=qo,
        out_shape=jax.ShapeDtypeStruct((nb, t, A_HEADS * A_VD), BF16),
        scratch_shapes=[pltpu.VMEM((A_VD + ONES_ROWS, t), BF16), pltpu.VMEM((t, 2 * TQA), F32),
                        pltpu.VMEM((t, 2 * TQA), BF16)],
        compiler_params=_cparams(("parallel", "parallel", "arbitrary")),
        name="diff_attn",
    )(q, k, v, lam_vec, subln)


def _gdn_tile_index(j, nt, rev):
    return jnp.where(j == 0, 0, nt - j) if rev else j


def _gdn_kernel(qkv_ref, prev_ref, next_ref, bg_ref, cw_ref, o_ref, s_ref, ext_ref, *, rev, nt):
    j = pl.program_id(1)
    jj = _gdn_tile_index(j, nt, rev)
    dirn = 1 if rev else 0
    nch = TM // B_CHUNK

    @pl.when(j == 0)
    def _():
        s_ref[...] = jnp.zeros_like(s_ref)

    lflag = (jj >= 2).astype(F32)
    rflag = jnp.logical_and(jj >= 1, jj <= nt - 2).astype(F32)
    ext_ref[0:HALO, :] = prev_ref[...] * lflag
    ext_ref[HALO:HALO + TM, :] = qkv_ref[...]
    ext_ref[HALO + TM:2 * HALO + TM, :] = next_ref[...] * rflag
    acc = None
    for kk in range(B_CONV):
        term = cw_ref[kk:kk + 1, :] * ext_ref[pl.ds(HALO - 2 + kk, TM), :]
        acc = term if acc is None else acc + term
    act = _silu(acc)

    bg = bg_ref[...]
    ri = lax.broadcasted_iota(jnp.int32, (TM, TM), 0)
    ci = lax.broadcasted_iota(jnp.int32, (TM, TM), 1)
    same = (ri // B_CHUNK) == (ci // B_CHUNK)
    incl = jnp.logical_and(same, (ri <= ci) if rev else (ri >= ci))
    strict = jnp.logical_and(same, (ri < ci) if rev else (ri > ci))
    eye = (ri == ci).astype(F32)
    gcum = _dot_hi(incl.astype(F32), bg)
    gcum_t = gcum.T
    rchunk = lax.broadcasted_iota(jnp.int32, (TM, B_DK), 0) // B_CHUNK

    def by_chunk(z):
        return jnp.concatenate([jnp.where(rchunk == c, z, 0.0) for c in range(nch)], axis=1).astype(BF16)

    def l2n(z):
        return z * lax.rsqrt(jnp.sum(z * z, axis=-1, keepdims=True) + EPS)

    heads = range(B_HEADS)
    lasts = [c * B_CHUNK if rev else (c + 1) * B_CHUNK - 1 for c in range(nch)]
    q, k, v, beta, gcol, eg, qkm, p, pw = ([None] * B_HEADS for _ in range(9))
    for h in heads:
        q[h] = l2n(act[:, h * 128:(h + 1) * 128]) * (B_DK ** -0.5)
        k[h] = l2n(act[:, B_W + h * 128:B_W + (h + 1) * 128])
        v[h] = act[:, 2 * B_W + h * 128:2 * B_W + (h + 1) * 128]
        cb = dirn * B_HEADS + h
        cg = 2 * B_HEADS + cb
        beta[h] = bg[:, cb:cb + 1]
        gcol[h] = gcum[:, cg:cg + 1]
        grow = gcum_t[cg:cg + 1, :]
        eg[h] = jnp.exp(gcol[h])
        decay = jnp.where(incl, jnp.exp(jnp.where(incl, gcol[h] - grow, 0.0)), 0.0)
        kb = k[h].astype(BF16)
        qkm[h] = _dot_nt(q[h].astype(BF16), kb) * decay
        pw[h] = jnp.where(strict, beta[h] * _dot_nt(kb, kb) * decay, 0.0)

    xr = ri ^ ci
    for lvl in range(6):
        joins = (xr >> lvl) == 1
        for h in heads:
            l_s = jnp.where(joins, pw[h], 0.0)
            if lvl == 0:
                p[h] = eye - l_s
            else:
                pb = p[h].astype(BF16)
                p[h] = p[h] - _dot(pb, _dot(l_s.astype(BF16), pb).astype(BF16))

    qku, qeff, mn = ([None] * B_HEADS for _ in range(3))
    for h in heads:
        uw = _dot3(p[h], jnp.concatenate([beta[h] * v[h], (beta[h] * eg[h]) * k[h]], axis=1))
        qkuw = _dot(qkm[h].astype(BF16), uw.astype(BF16))
        qku[h] = qkuw[:, 0:128]
        qeff[h] = (q[h] * eg[h] - qkuw[:, 128:256]).astype(BF16)
        glast = jnp.concatenate(
            [jnp.broadcast_to(gcol[h][r:r + 1, :], (B_CHUNK, 1)) for r in lasts], axis=0)
        kdec = (k[h] * jnp.exp(glast - gcol[h])).astype(BF16)
        mn[h] = _dot_tn(kdec, jnp.concatenate([by_chunk(uw[:, 128:256]), by_chunk(uw[:, 0:128])], axis=1))

    for step in range(nch):
        c = nch - 1 - step if rev else step
        r0, r1 = c * B_CHUNK, (c + 1) * B_CHUNK
        for h in heads:
            sh = s_ref[h]
            shb = sh.astype(BF16)
            o_ref[r0:r1, h * 128:(h + 1) * 128] = _dot(qeff[h][r0:r1], shb) + qku[h][r0:r1]
            mc = mn[h][:, c * 128:(c + 1) * 128].astype(BF16)
            nc = mn[h][:, (nch + c) * 128:(nch + c + 1) * 128]
            gl = jnp.exp(gcol[h][lasts[c]:lasts[c] + 1, :])
            s_ref[h] = sh * gl - _dot(mc, shb) + nc


def _gdn(qkv, bg, conv_w, rev):
    nb, t, w = qkv.shape
    nt = t // TM
    hb = TM // HALO
    idx = lambda j: _gdn_tile_index(j, nt, rev)
    main = lambda ww: pl.BlockSpec((None, TM, ww), lambda b, j: (b, idx(j), 0))
    prev = pl.BlockSpec((None, HALO, w), lambda b, j: (b, jnp.maximum(idx(j) * hb - 1, 0), 0))
    nxt = pl.BlockSpec((None, HALO, w), lambda b, j: (b, jnp.minimum((idx(j) + 1) * hb, t // HALO - 1), 0))
    return pl.pallas_call(
        functools.partial(_gdn_kernel, rev=rev, nt=nt),
        grid=(nb, nt),
        in_specs=[main(w), prev, nxt, main(128), _full(conv_w.shape)],
        out_specs=main(B_W),
        out_shape=jax.ShapeDtypeStruct((nb, t, B_W), F32),
        scratch_shapes=[pltpu.VMEM((B_HEADS, B_DK, B_DK), F32), pltpu.VMEM((TM + 2 * HALO, w), F32)],
        compiler_params=_cparams(("parallel", "arbitrary")),
        name="gdn_rev" if rev else "gdn_fwd",
    )(qkv, qkv, qkv, bg, conv_w)


def _even_out_kernel(x_ref, mod_ref, ya_ref, of_ref, or_ref, gate_ref, og_ref, w_ref, o_ref):
    d = D_MODEL
    ob = of_ref[...] + or_ref[...]
    gate = gate_ref[...]
    yb = jnp.concatenate(
        [_rms(ob[:, h * 128:(h + 1) * 128], og_ref[...]) * _silu(gate[:, h * 128:(h + 1) * 128])
         for h in range(B_HEADS)], axis=1)
    y = _dot(ya_ref[...], w_ref[0:512, :]) + _dot(yb.astype(BF16), w_ref[512:1024, :])
    o_ref[...] = x_ref[...] + mod_ref[:, 2 * d:3 * d] * y


def _even_out(x, mod, ya, of, orv, gate, og, w):
    nb, t, d = x.shape
    tile = lambda ww: pl.BlockSpec((None, TM, ww), lambda b, j: (b, j, 0))
    return pl.pallas_call(
        _even_out_kernel,
        grid=(nb, t // TM),
        in_specs=[tile(d), _mod_spec(nb), tile(512), tile(512), tile(512), tile(512), _full((1, 128)),
                  _full(w.shape)],
        out_specs=tile(d),
        out_shape=jax.ShapeDtypeStruct(x.shape, F32),
        compiler_params=_cparams(("parallel", "parallel")),
        name="even_out",
    )(x, mod, ya, of, orv, gate, og, w)


def _ffn_kernel(x_ref, prev_ref, next_ref, mod_ref, g_ref, wup_ref, cw_ref, wdn_ref, o_ref, acc_ref,
                *, ntt, nct):
    d = D_MODEL
    nb = x_ref.shape[0]
    j = pl.program_id(0)
    mod = mod_ref[...]
    shift, scale, gate = mod[:, 3 * d:4 * d], mod[:, 4 * d:5 * d], mod[:, 5 * d:6 * d]
    g = g_ref[...]
    lflag = jnp.logical_and(j != 0, j != nct).astype(F32)
    rflag = jnp.logical_and(j != nct - 1, j != ntt - 1).astype(F32)
    x3 = jnp.stack([x_ref[:, t * d:(t + 1) * d] for t in range(TTF)], axis=0)
    u3 = jnp.concatenate([(_modulate(prev_ref[...], g, shift, scale) * lflag)[None],
                          _modulate(x3, g, shift, scale),
                          (_modulate(next_ref[...], g, shift, scale) * rflag)[None]], axis=0)
    u = u3.reshape((TTF + 2) * nb, d).astype(BF16)
    rows = TTF * nb
    for c in range(FFN // FC):
        c0 = c * FC
        hg = _dot(u, wup_ref[:, c0:c0 + FC])
        hv = _dot(u, wup_ref[:, FFN + c0:FFN + c0 + FC])
        cg = None
        cv = None
        for kk in range(FFN_CONV):
            tg = cw_ref[kk:kk + 1, c0:c0 + FC] * hg[kk * nb:kk * nb + rows]
            tv = cw_ref[kk:kk + 1, FFN + c0:FFN + c0 + FC] * hv[kk * nb:kk * nb + rows]
            cg = tg if cg is None else cg + tg
            cv = tv if cv is None else cv + tv
        part = _dot((_silu(cg) * cv).astype(BF16), wdn_ref[c0:c0 + FC, :])
        if c == 0:
            acc_ref[...] = part
        else:
            acc_ref[...] += part
    out3 = x3 + gate * acc_ref[...].reshape(TTF, nb, d)
    for t in range(TTF):
        o_ref[:, t * d:(t + 1) * d] = out3[t]


def _ffn(x, mod2, g, wup, cw, wdn, n_ctx):
    nb, t, d = x.shape
    ntt, nct = t // TTF, n_ctx // TTF
    x2 = x.reshape(nb, t * d)
    tile = pl.BlockSpec((nb, TTF * d), lambda j: (0, j))
    prev = pl.BlockSpec((nb, d), lambda j: (0, jnp.maximum(j * TTF - 1, 0)))
    nxt = pl.BlockSpec((nb, d), lambda j: (0, jnp.minimum((j + 1) * TTF, t - 1)))
    modspec = pl.BlockSpec((None, nb, N_MOD * d), lambda j: (jnp.where(j < nct, 0, 1), 0, 0))
    resident = lambda shape: pl.BlockSpec(shape, lambda j: (0, 0), pipeline_mode=pl.Buffered(1))
    out = pl.pallas_call(
        functools.partial(_ffn_kernel, ntt=ntt, nct=nct),
        grid=(ntt,),
        in_specs=[tile, prev, nxt, modspec, _full((1, d)), resident(wup.shape), _full(cw.shape),
                  resident(wdn.shape)],
        out_specs=tile,
        out_shape=jax.ShapeDtypeStruct(x2.shape, F32),
        scratch_shapes=[pltpu.VMEM((TTF * nb, d), F32)],
        compiler_params=_cparams(("parallel",)),
        name="ffn",
    )(x2, x2, x2, mod2, g, wup, cw, wdn)
    return out.reshape(nb, t, d)


def _odd_in_kernel(x_ref, mod_ref, g_ref, c_ref, sa_ref, sb_ref, wr_ref, wq_ref, wkv_ref,
                   xr_ref, gate_ref, q_ref, k_ref, v_ref):
    d = D_MODEL
    mod = mod_ref[...]
    u = _modulate(x_ref[...], g_ref[...], mod[:, 0:d], mod[:, d:2 * d]).astype(BF16)
    c, sa, sb = c_ref[...], sa_ref[...], sb_ref[...]
    xr_ref[...] = _dot(u, wr_ref[:, 0:C_WIDTH])
    gate_ref[...] = _dot(u, wr_ref[:, C_WIDTH:2 * C_WIDTH])
    q_ref[...] = (_rope(_dot(u, wq_ref[...]), c, sa, sb) * (D_HD ** -0.5)).astype(BF16)
    k_ref[...] = _rope(_dot(u, wkv_ref[:, 0:256]), c, sa, sb).astype(BF16)
    v_ref[...] = _dot(u, wkv_ref[:, 256:512]).astype(BF16)


def _odd_in(x, mod, g, tabs, wr, wq, wkv):
    nb, t, d = x.shape
    tile = lambda w: pl.BlockSpec((None, TM, w), lambda b, j: (b, j, 0))
    tmaj = pl.BlockSpec((TM, C_WIDTH), lambda b, j: (j, b))
    tab = pl.BlockSpec((TM, 128), lambda b, j: (j, 0))
    outs = [jax.ShapeDtypeStruct((t, nb * C_WIDTH), F32)] * 2 + [
        jax.ShapeDtypeStruct((nb, t, 512), BF16), jax.ShapeDtypeStruct((nb, t, 256), BF16),
        jax.ShapeDtypeStruct((nb, t, 256), BF16)]
    return pl.pallas_call(
        _odd_in_kernel,
        grid=(nb, t // TM),
        in_specs=[tile(d), _mod_spec(nb), _full((1, d)), tab, tab, tab, _full(wr.shape), _full(wq.shape),
                  _full(wkv.shape)],
        out_specs=[tmaj, tmaj, tile(512), tile(256), tile(256)],
        out_shape=outs,
        compiler_params=_cparams(("parallel", "parallel")),
        name="odd_in",
    )(x, mod, g, *tabs, wr, wq, wkv)


def _lru_tile_index(j, ntt, nct, rev):
    return jnp.where(j < nct, nct - 1 - j, ntt + nct - 1 - j) if rev else j


def _lru_kernel(x_ref, prev_ref, next_ref, cw_ref, cb_ref, w_ref, b_ref, lam_ref, o_ref,
                h_ref, a_ref, bc_ref, *, rev, ntt, nct):
    j = pl.program_id(0)
    jj = _lru_tile_index(j, ntt, nct, rev)
    nb = x_ref.shape[1]

    @pl.when(j == 0)
    def _():
        h_ref[...] = jnp.zeros_like(h_ref)

    lflag = jnp.logical_and(jj != 0, jj != nct).astype(F32)
    rflag = jnp.logical_and(jj != nct - 1, jj != ntt - 1).astype(F32)
    ext = jnp.concatenate([prev_ref[...] * lflag, x_ref[...], next_ref[...] * rflag], axis=0)
    xc = cb_ref[...]
    for kk in range(C_CONV):
        xc = xc + cw_ref[kk:kk + 1, :] * ext[kk:kk + TT]
    xc2 = xc.reshape(TT * nb, C_WIDTH)
    z = _dot(xc2.astype(BF16), w_ref[...]) + b_ref[...]
    r = _sigmoid(z[:, 0:C_WIDTH])
    gi = _sigmoid(z[:, C_WIDTH:2 * C_WIDTH])
    log_a = -C_POW * r * _softplus(-lam_ref[...])
    a = jnp.exp(log_a)
    bc = jnp.sqrt(1.0 - jnp.exp(2.0 * log_a)) * (gi * xc2)
    a_ref[...] = a.reshape(TT, nb, C_WIDTH)
    bc_ref[...] = bc.reshape(TT, nb, C_WIDTH)

    def body(s, h):
        t = TT - 1 - s if rev else s
        h = a_ref[t] * h + bc_ref[t]
        o_ref[t] = h
        return h

    h_ref[...] = lax.fori_loop(0, TT, body, h_ref[...], unroll=8)


def _lru(xr, conv_w, conv_b, w, b, lam, n_ctx, rev):
    t, nb, c = xr.shape
    ntt, nct = t // TT, n_ctx // TT
    idx = lambda j: _lru_tile_index(j, ntt, nct, rev)
    main = pl.BlockSpec((TT, nb, c), lambda j: (idx(j), 0, 0))
    prev = pl.BlockSpec((2, nb, c), lambda j: (jnp.maximum(idx(j) * (TT // 2) - 1, 0), 0, 0))
    nxt = pl.BlockSpec((1, nb, c), lambda j: (jnp.minimum((idx(j) + 1) * TT, t - 1), 0, 0))
    return pl.pallas_call(
        functools.partial(_lru_kernel, rev=rev, ntt=ntt, nct=nct),
        grid=(ntt,),
        in_specs=[main, prev, nxt, _full(conv_w.shape), _full(conv_b.shape), _full(w.shape), _full(b.shape),
                  _full(lam.shape)],
        out_specs=main,
        out_shape=jax.ShapeDtypeStruct(xr.shape, F32),
        scratch_shapes=[pltpu.VMEM((nb, c), F32), pltpu.VMEM((TT, nb, c), F32), pltpu.VMEM((TT, nb, c), F32)],
        compiler_params=_cparams(("arbitrary",)),
        name="lru_rev" if rev else "lru_fwd",
    )(xr, xr, xr, conv_w, conv_b, w, b, lam)


def _win_attn_kernel(q_ref, k_ref, v_ref, sink_ref, o_ref, *, n_ctx, n_lat):
    i = pl.program_id(1)
    nctx_tiles = n_ctx // TQ
    nlb = n_lat // TQ
    grp = D_HEADS // D_KV
    lane = lax.broadcasted_iota(jnp.int32, (TQ, 128), 1)
    low = lane < D_HD

    def stacked_q(g):
        parts = []
        for sl in range(grp // 2):
            slab = q_ref[:, (g * (grp // 2) + sl) * 128:(g * (grp // 2) + sl + 1) * 128]
            zero = jnp.zeros_like(slab)
            parts += [jnp.where(low, slab, zero), jnp.where(low, zero, slab)]
        return jnp.concatenate(parts, axis=0)

    def sink_col(g):
        return jnp.concatenate(
            [jnp.broadcast_to(sink_ref[g * grp + hh:g * grp + hh + 1, 0:1], (TQ, 1)) for hh in range(grp)], axis=0)

    def store(g, o):
        for sl in range(grp // 2):
            a = o[(2 * sl) * TQ:(2 * sl + 1) * TQ]
            b = o[(2 * sl + 1) * TQ:(2 * sl + 2) * TQ]
            col = (g * (grp // 2) + sl) * 128
            o_ref[:, col:col + 128] = jnp.where(low, a, b).astype(o_ref.dtype)

    @pl.when(i < nctx_tiles)
    def _():
        for g in range(D_KV):
            qz, sk = stacked_q(g), sink_col(g)
            kc = k_ref[0:n_ctx, g * 128:(g + 1) * 128]
            vc = v_ref[0:n_ctx, g * 128:(g + 1) * 128]
            s = _dot_nt(qz, kc)
            m = jnp.maximum(jnp.max(s, axis=-1, keepdims=True), sk)
            e = jnp.exp(s - m)
            den = jnp.sum(e, axis=-1, keepdims=True) + jnp.exp(sk - m)
            store(g, _dot(e.astype(BF16), vc) / den)

    @pl.when(i >= nctx_tiles)
    def _():
        il = i - nctx_tiles
        kb = jnp.clip(il - 1, 0, nlb - 3)
        start = pl.multiple_of(n_ctx + kb * TQ, TQ)
        nloc = 3 * TQ
        qpos = il * TQ + (lax.broadcasted_iota(jnp.int32, (grp * TQ, nloc), 0) % TQ)
        kpos = kb * TQ + lax.broadcasted_iota(jnp.int32, (grp * TQ, nloc), 1)
        mask = jnp.abs(kpos - qpos) <= WINDOW
        for g in range(D_KV):
            qz, sk = stacked_q(g), sink_col(g)
            kc = k_ref[0:n_ctx, g * 128:(g + 1) * 128]
            vc = v_ref[0:n_ctx, g * 128:(g + 1) * 128]
            kl = k_ref[pl.ds(start, nloc), g * 128:(g + 1) * 128]
            vl = v_ref[pl.ds(start, nloc), g * 128:(g + 1) * 128]
            sc = _dot_nt(qz, kc)
            sl_ = jnp.where(mask, _dot_nt(qz, kl), NEG_INF)
            m = jnp.maximum(jnp.maximum(jnp.max(sc, axis=-1, keepdims=True),
                                        jnp.max(sl_, axis=-1, keepdims=True)), sk)
            ec = jnp.exp(sc - m)
            el = jnp.exp(sl_ - m)
            den = (jnp.sum(ec, axis=-1, keepdims=True) + jnp.sum(el, axis=-1, keepdims=True)
                   + jnp.exp(sk - m))
            store(g, (_dot(ec.astype(BF16), vc) + _dot(el.astype(BF16), vl)) / den)


def _win_attn(q, k, v, sink, n_ctx):
    nb, t, _ = q.shape
    kv = pl.BlockSpec((None, t, 256), lambda b, i: (b, 0, 0))
    qo = pl.BlockSpec((None, TQ, 512), lambda b, i: (b, i, 0))
    return pl.pallas_call(
        functools.partial(_win_attn_kernel, n_ctx=n_ctx, n_lat=t - n_ctx),
        grid=(nb, t // TQ),
        in_specs=[qo, kv, kv, _full(sink.shape)],
        out_specs=qo,
        out_shape=jax.ShapeDtypeStruct((nb, t, D_HEADS * D_HD), BF16),
        compiler_params=_cparams(("parallel", "arbitrary")),
        name="win_attn",
    )(q, k, v, sink)


def _odd_out_kernel(x_ref, mod_ref, hf_ref, hr_ref, gate_ref, od_ref, w_ref, o_ref):
    d = D_MODEL
    yc = (hf_ref[...] + hr_ref[...]) * _gelu_tanh(gate_ref[...])
    y = _dot(yc.astype(BF16), w_ref[0:512, :]) + _dot(od_ref[...], w_ref[512:1024, :])
    o_ref[...] = x_ref[...] + mod_ref[:, 2 * d:3 * d] * y


def _odd_out(x, mod, hf, hr, gate, od, w):
    nb, t, d = x.shape
    tile = lambda ww: pl.BlockSpec((None, TM, ww), lambda b, j: (b, j, 0))
    tmaj = pl.BlockSpec((TM, C_WIDTH), lambda b, j: (j, b))
    return pl.pallas_call(
        _odd_out_kernel,
        grid=(nb, t // TM),
        in_specs=[tile(d), _mod_spec(nb), tmaj, tmaj, tmaj, tile(512), _full(w.shape)],
        out_specs=tile(d),
        out_shape=jax.ShapeDtypeStruct(x.shape, F32),
        compiler_params=_cparams(("parallel", "parallel")),
        name="odd_out",
    )(x, mod, hf, hr, gate, od, w)


def _final_kernel(x_ref, g_ref, o_ref):
    o_ref[...] = _rms(x_ref[...], g_ref[...])


def _final(x, g, n_ctx):
    nb, t, d = x.shape
    off = n_ctx // TM
    return pl.pallas_call(
        _final_kernel,
        grid=(nb, (t - n_ctx) // TM),
        in_specs=[pl.BlockSpec((None, TM, d), lambda b, j: (b, j + off, 0)), _full((1, d))],
        out_specs=pl.BlockSpec((None, TM, d), lambda b, j: (b, j, 0)),
        out_shape=jax.ShapeDtypeStruct((nb, t - n_ctx, d), F32),
        compiler_params=_cparams(("parallel", "parallel")),
        name="final_norm",
    )(x, g)


def _rope_tables(n_ctx, n_lat):
    pos = np.arange(n_lat)
    inv = ROPE_THETA ** (-np.arange(0, ROT_AXIS, 2, dtype=np.float64) / ROT_AXIS)
    ang_r = (pos // GRID_W)[:, None] * inv
    ang_c = (pos % GRID_W)[:, None] * inv
    lane = np.arange(128) % 64
    seg, f = lane // 16, lane % 16
    ang = np.where(seg[None, :] < 2, ang_r[:, f], ang_c[:, f])
    c = np.cos(ang)
    s = np.sin(ang)
    sa = np.where((seg % 2 == 0)[None, :], -s, 0.0)
    sb = np.where((seg % 2 == 1)[None, :], s, 0.0)
    pad = lambda a, fill: np.concatenate([np.full((n_ctx, 128), fill), a], axis=0).astype(np.float32)
    return jnp.asarray(pad(c, 1.0)), jnp.asarray(pad(sa, 0.0)), jnp.asarray(pad(sb, 0.0))


def _block_diag(w):
    eye = jnp.eye(C_BLOCKS, dtype=w.dtype)
    return jnp.einsum('hij,hg->higj', w, eye).reshape(C_WIDTH, C_WIDTH)


def kernel(x, c, ctx, c_ctx, w_ada, b_ada, norm_mix, norm_ffn, ffn_w_up, ffn_conv, ffn_w_down, final_norm,
           ev_w_in, ev_w_out, diff_lambda, diff_subln, gdn_conv, gdn_a_log, gdn_dt_bias, gdn_norm,
           od_w_in, od_w_out, lru_conv, lru_conv_b, lru_wa, lru_ba, lru_wx, lru_bx, lru_lambda, swa_sink):
    nb, n_lat, d = x.shape
    n_ctx = ctx.shape[1]
    depth = w_ada.shape[0]
    assert d == D_MODEL and n_ctx == TM and n_lat % TM == 0 and n_lat // TQ >= 3 and nb < MOD_ROWS
    t = n_ctx + n_lat

    xa = jnp.concatenate([ctx, x], axis=1)
    c_all = jnp.zeros((MOD_ROWS, d), F32).at[0:nb].set(c).at[nb].set(c_ctx)
    mod_all = _modulation(c_all, w_ada, b_ada).reshape(depth, MOD_ROWS, 1, N_MOD * d)
    tabs = _rope_tables(n_ctx, n_lat)
    row = lambda v: v.reshape(1, -1).astype(F32)

    for layer in range(depth):
        jx = layer // 2
        mod = mod_all[layer]
        mod2 = jnp.stack([jnp.broadcast_to(mod[nb], (nb, N_MOD * d)), mod[0:nb, 0]], axis=0)
        if layer % 2 == 0:
            lam_init = 0.8 - 0.6 * math.exp(-0.3 * layer)
            w_in = ev_w_in[jx]
            wa = w_in[:, 0:1536].astype(BF16)
            wg = w_in[:, 1536:3072].astype(BF16)
            wgate = w_in[:, 3072:3584].astype(BF16)
            wba = jnp.pad(w_in[:, 3584:3600], ((0, 0), (0, 112))).astype(BF16)
            pad16 = lambda v: jnp.pad(v.reshape(1, 8).astype(F32), ((0, 0), (8, 112)))
            q, k, v, qkv, gate, bg = _even_in(xa, mod, row(norm_mix[layer]), tabs, wa, wg, wgate, wba,
                                              pad16(gdn_a_log[jx]), pad16(gdn_dt_bias[jx]))
            ya = _diff_attn(q, k, v, diff_lambda[jx].astype(F32), diff_subln[jx].astype(F32).reshape(-1, 1),
                            lam_init, n_ctx)
            of = _gdn(qkv, bg, gdn_conv[jx].astype(F32), rev=False)
            orv = _gdn(qkv, bg, gdn_conv[jx].astype(F32), rev=True)
            xa = _even_out(xa, mod, ya, of, orv, gate, row(gdn_norm[jx]), ev_w_out[jx].astype(BF16))
        else:
            w_in = od_w_in[jx]
            wr = w_in[:, 0:1024].astype(BF16)
            wq = w_in[:, 1024:1536].astype(BF16)
            dup = lambda w: jnp.concatenate([w[:, 0:64], w[:, 0:64], w[:, 64:128], w[:, 64:128]], axis=1)
            wkv = jnp.concatenate([dup(w_in[:, 1536:1664]), dup(w_in[:, 1664:1792])], axis=1).astype(BF16)
            xr, gate, q, k, v = _odd_in(xa, mod, row(norm_mix[layer]), tabs, wr, wq, wkv)
            xr3 = xr.reshape(t, nb, C_WIDTH)
            hs = []
            for dd in range(2):
                wbig = jnp.concatenate([_block_diag(lru_wa[jx, dd]), _block_diag(lru_wx[jx, dd])], axis=1)
                bbig = jnp.concatenate([lru_ba[jx, dd], lru_bx[jx, dd]]).reshape(1, -1).astype(F32)
                hs.append(_lru(xr3, lru_conv[jx].astype(F32), row(lru_conv_b[jx]), wbig.astype(BF16), bbig,
                               row(lru_lambda[jx, dd]), n_ctx, rev=(dd == 1)).reshape(t, nb * C_WIDTH))
            sink = jnp.broadcast_to(swa_sink[jx].astype(F32)[:, None], (D_HEADS, 128))
            od = _win_attn(q, k, v, sink, n_ctx)
            xa = _odd_out(xa, mod, hs[0], hs[1], gate, od, od_w_out[jx].astype(BF16))
        xa = _ffn(xa, mod2, row(norm_ffn[layer]), ffn_w_up[layer].astype(BF16), ffn_conv[layer].astype(F32),
                  ffn_w_down[layer].astype(BF16), n_ctx)
    return _final(xa, row(final_norm), n_ctx)
```

```python
import functools
import math

import jax
import jax.numpy as jnp
import numpy as np
from jax import lax
from jax.experimental import pallas as pl
from jax.experimental.pallas import tpu as pltpu

F32 = jnp.float32
BF16 = jnp.bfloat16
HIGHEST = lax.Precision.HIGHEST

D_MODEL = 1024
GRID_W = 64
EPS = 1e-6
NEG_INF = -1e30
N_MOD = 6
ROPE_THETA = 10000.0
ROT_AXIS = 32
A_HEADS = 4
A_HD = 64
A_VD = 128
B_HEADS = 4
B_DK = 128
B_W = 512
B_CONV = 4
B_CHUNK = 64
C_WIDTH = 512
C_BLOCKS = 8
C_BD = 64
C_CONV = 4
C_POW = 8.0
D_HEADS = 8
D_KV = 2
D_HD = 64
WINDOW = 128
FFN = 2816
FFN_CONV = 3

TM = 256
TQ = 128
TQA = 256
KB = 256
ONES_ROWS = 16
SINK_ROWS = 16
LOG2E = math.log2(math.e)
TT = 64
HALO = 8
FC = 256
TTF = 64
MOD_ROWS = 16
VMEM_LIMIT = 56 * 1024 * 1024


def _cparams(sem):
    return pltpu.CompilerParams(dimension_semantics=sem, vmem_limit_bytes=VMEM_LIMIT)


def _sigmoid(x):
    return 1.0 / (1.0 + jnp.exp(-x))


def _silu(x):
    return x * _sigmoid(x)


def _softplus(x):
    return jnp.maximum(x, 0.0) + jnp.log(1.0 + jnp.exp(-jnp.abs(x)))


def _gelu_tanh(x):
    return 0.5 * x * (1.0 + jnp.tanh(math.sqrt(2.0 / math.pi) * (x + 0.044715 * (x * x * x))))


def _dot(a, b):
    return jnp.dot(a, b, preferred_element_type=F32)


def _dot_hi(a, b):
    return jnp.dot(a, b, preferred_element_type=F32, precision=HIGHEST)


def _dot3(a, b):
    ah = a.astype(BF16)
    al = (a - ah.astype(F32)).astype(BF16)
    bh = b.astype(BF16)
    bl = (b - bh.astype(F32)).astype(BF16)
    return _dot(ah, bh) + (_dot(ah, bl) + _dot(al, bh))


def _dot_nt(a, b):
    return lax.dot_general(a, b, (((1,), (1,)), ((), ())), preferred_element_type=F32)


def _dot_tn(a, b):
    return lax.dot_general(a, b, (((0,), (0,)), ((), ())), preferred_element_type=F32)


def _rms(x, g):
    return x * lax.rsqrt(jnp.mean(x * x, axis=-1, keepdims=True) + EPS) * g


def _modulate(x, g, shift, scale):
    return _rms(x, g) * (1.0 + scale) + shift


def _rope128(z, c, sa, sb):
    return z * c + pltpu.roll(z, 112, 1) * sa + pltpu.roll(z, 16, 1) * sb


def _rope(z, c, sa, sb):
    n = z.shape[1] // 128
    return jnp.concatenate([_rope128(z[:, i * 128:(i + 1) * 128], c, sa, sb) for i in range(n)], axis=1)


def _mod_kernel(s_ref, w_ref, b_ref, o_ref):
    s = _silu(s_ref[...])
    o_ref[...] = _dot_hi(s, w_ref[...]) + b_ref[...]


def _modulation(c_all, w_ada, b_ada):
    depth, d, n = w_ada.shape
    tn = 1536
    return pl.pallas_call(
        _mod_kernel,
        grid=(depth, n // tn),
        in_specs=[pl.BlockSpec((MOD_ROWS, d), lambda l, j: (0, 0)),
                  pl.BlockSpec((None, d, tn), lambda l, j: (l, 0, j)),
                  pl.BlockSpec((None, 1, tn), lambda l, j: (l, 0, j))],
        out_specs=pl.BlockSpec((None, MOD_ROWS, tn), lambda l, j: (l, 0, j)),
        out_shape=jax.ShapeDtypeStruct((depth, MOD_ROWS, n), F32),
        compiler_params=_cparams(("arbitrary", "arbitrary")),
        name="modulation",
    )(c_all, w_ada, b_ada.reshape(depth, 1, n))


def _mod_spec(nb):
    return pl.BlockSpec((None, 1, N_MOD * D_MODEL), lambda b, j: (jnp.where(j == 0, nb, b), 0, 0))


def _full(shape):
    nd = len(shape)
    return pl.BlockSpec(shape, lambda *_: (0,) * nd)


def _even_in_kernel(x_ref, mod_ref, g_ref, c_ref, sa_ref, sb_ref, wa_ref, wg_ref, wgate_ref, wba_ref,
                    alog_ref, dtb_ref, q_ref, k_ref, v_ref, qkv_ref, gate_ref, bg_ref):
    d = D_MODEL
    mod = mod_ref[...]
    u = _modulate(x_ref[...], g_ref[...], mod[:, 0:d], mod[:, d:2 * d]).astype(BF16)
    c, sa, sb = c_ref[...], sa_ref[...], sb_ref[...]
    q = _dot(u, wa_ref[:, 0:512])
    q_ref[...] = (_rope(q, c, sa, sb) * (A_HD ** -0.5 * LOG2E)).astype(BF16)
    k = _dot(u, wa_ref[:, 512:1024])
    k_ref[...] = _rope(k, c, sa, sb).astype(BF16)
    v_ref[...] = _dot(u, wa_ref[:, 1024:1536]).astype(BF16)
    qkv_ref[...] = _dot(u, wg_ref[...])
    gate_ref[...] = _dot(u, wgate_ref[...])
    z = _dot(u, wba_ref[...])
    lane = lax.broadcasted_iota(jnp.int32, z.shape, 1)
    beta = _sigmoid(z)
    gdec = -jnp.exp(alog_ref[...]) * _softplus(z + dtb_ref[...])
    bg_ref[...] = jnp.where(lane < 2 * B_HEADS, beta, jnp.where(lane < 4 * B_HEADS, gdec, 0.0))


def _even_in(x, mod, g, tabs, wa, wg, wgate, wba, alog, dtb):
    nb, t, d = x.shape
    nt = t // TM
    tile = lambda w: pl.BlockSpec((None, TM, w), lambda b, j: (b, j, 0))
    tab = pl.BlockSpec((TM, 128), lambda b, j: (j, 0))
    outs = [jax.ShapeDtypeStruct((nb, t, 512), BF16)] * 3 + [
        jax.ShapeDtypeStruct((nb, t, 1536), F32), jax.ShapeDtypeStruct((nb, t, 512), F32),
        jax.ShapeDtypeStruct((nb, t, 128), F32)]
    return pl.pallas_call(
        _even_in_kernel,
        grid=(nb, nt),
        in_specs=[tile(d), _mod_spec(nb), _full((1, d)), tab, tab, tab, _full(wa.shape), _full(wg.shape),
                  _full(wgate.shape), _full(wba.shape), _full((1, 128)), _full((1, 128))],
        out_specs=[tile(512), tile(512), tile(512), tile(1536), tile(512), tile(128)],
        out_shape=outs,
        compiler_params=_cparams(("parallel", "parallel")),
        name="even_in",
    )(x, mod, g, *tabs, wa, wg, wgate, wba, alog, dtb)


def _diff_attn_kernel(q_ref, k_ref, v_ref, lv_ref, g_ref, o_ref, vt_ref, s_ref, m_ref, e_ref, *, lam_init, n_ctx):
    i = pl.program_id(2)
    t = k_ref.shape[0]
    nblk = t // KB
    slot = i % 2

    @pl.when(i == 0)
    def _():
        vt_ref[0:A_VD, :] = v_ref[...].astype(F32).T.astype(BF16)
        orow = lax.broadcasted_iota(jnp.int32, (ONES_ROWS, t), 0)
        vt_ref[A_VD:A_VD + ONES_ROWS, :] = jnp.where(orow == 0, 1.0, 0.0).astype(BF16)
        s_ref[1] = jnp.zeros(s_ref.shape[1:], F32)
        m_ref[1] = jnp.zeros(m_ref.shape[1:], F32)

    is_ctx = jnp.minimum(i, pl.num_programs(2) - 2) < n_ctx // TQA
    q = q_ref[...]
    lane = lax.broadcasted_iota(jnp.int32, q.shape, 1)
    zero = jnp.zeros_like(q)
    qz = jnp.concatenate([jnp.where(lane < A_HD, q, zero), jnp.where(lane >= A_HD, q, zero)], axis=0)
    mrun = None
    for kb in range(nblk):
        sblk = _dot_nt(k_ref[kb * KB:(kb + 1) * KB, :], qz)
        if kb >= n_ctx // KB:
            sblk = jnp.where(is_ctx, NEG_INF, sblk)
        s_ref[slot, kb * KB:(kb + 1) * KB, :] = sblk
        part = jnp.max(sblk.reshape(KB // 8, 8, 2 * TQA), axis=0)
        mrun = part if mrun is None else jnp.maximum(mrun, part)
    m_ref[slot] = mrun

    lv = lv_ref[...]
    lam = (jnp.exp(jnp.sum(lv[0:1] * lv[1:2], axis=-1, keepdims=True))
           - jnp.exp(jnp.sum(lv[2:3] * lv[3:4], axis=-1, keepdims=True)) + lam_init)
    m = jnp.max(m_ref[1 - slot], axis=0, keepdims=True)
    for kb in range(nblk):
        e_ref[kb * KB:(kb + 1) * KB, :] = jnp.exp2(s_ref[1 - slot, kb * KB:(kb + 1) * KB, :] - m).astype(BF16)
    oe = _dot(vt_ref[...], e_ref[...])
    on = oe[0:A_VD] / oe[A_VD:A_VD + 1]
    od = on[:, 0:TQA] - lam * on[:, TQA:2 * TQA]
    y = od * lax.rsqrt(jnp.mean(od * od, axis=0, keepdims=True) + EPS) * (g_ref[...] * (1.0 - lam_init))
    o_ref[...] = y.T.astype(o_ref.dtype)


def _diff_attn(q, k, v, lam_vec, subln, lam_init, n_ctx):
    nb, t, _ = q.shape
    nq = t // TQA
    kv = pl.BlockSpec((None, t, 128), lambda b, h, i: (b, 0, h))
    qin = pl.BlockSpec((None, TQA, 128), lambda b, h, i: (b, jnp.minimum(i, nq - 1), h))
    out = pl.BlockSpec((None, TQA, 128), lambda b, h, i: (b, jnp.maximum(i - 1, 0), h))
    return pl.pallas_call(
        functools.partial(_diff_attn_kernel, lam_init=lam_init, n_ctx=n_ctx),
        grid=(nb, A_HEADS, nq + 1),
        in_specs=[qin, kv, kv, _full((4, A_HD)), _full((A_VD, 1))],
        out_specs=out,
        out_shape=jax.ShapeDtypeStruct((nb, t, A_HEADS * A_VD), BF16),
        scratch_shapes=[pltpu.VMEM((A_VD + ONES_ROWS, t), BF16), pltpu.VMEM((2, t, 2 * TQA), F32),
                        pltpu.VMEM((2, 8, 2 * TQA), F32), pltpu.VMEM((t, 2 * TQA), BF16)],
        compiler_params=_cparams(("parallel", "parallel", "arbitrary")),
        name="diff_attn",
    )(q, k, v, lam_vec, subln)


def _gdn_tile_index(j, nt, rev):
    return jnp.where(j == 0, 0, nt - j) if rev else j


def _gdn_kernel(qkv_ref, prev_ref, next_ref, bg_ref, cw_ref, o_ref, s_ref, ext_ref, *, rev, nt):
    j = pl.program_id(1)
    jj = _gdn_tile_index(j, nt, rev)
    dirn = 1 if rev else 0
    nch = TM // B_CHUNK

    @pl.when(j == 0)
    def _():
        s_ref[...] = jnp.zeros_like(s_ref)

    lflag = (jj >= 2).astype(F32)
    rflag = jnp.logical_and(jj >= 1, jj <= nt - 2).astype(F32)
    ext_ref[0:HALO, :] = prev_ref[...] * lflag
    ext_ref[HALO:HALO + TM, :] = qkv_ref[...]
    ext_ref[HALO + TM:2 * HALO + TM, :] = next_ref[...] * rflag
    acc = None
    for kk in range(B_CONV):
        term = cw_ref[kk:kk + 1, :] * ext_ref[pl.ds(HALO - 2 + kk, TM), :]
        acc = term if acc is None else acc + term
    act = _silu(acc)

    bg = bg_ref[...]
    ri = lax.broadcasted_iota(jnp.int32, (TM, TM), 0)
    ci = lax.broadcasted_iota(jnp.int32, (TM, TM), 1)
    same = (ri // B_CHUNK) == (ci // B_CHUNK)
    incl = jnp.logical_and(same, (ri <= ci) if rev else (ri >= ci))
    strict = jnp.logical_and(same, (ri < ci) if rev else (ri > ci))
    eye = (ri == ci).astype(F32)
    gcum = _dot_hi(incl.astype(F32), bg)
    gcum_t = gcum.T
    rchunk = lax.broadcasted_iota(jnp.int32, (TM, B_DK), 0) // B_CHUNK

    def by_chunk(z):
        return jnp.concatenate([jnp.where(rchunk == c, z, 0.0) for c in range(nch)], axis=1).astype(BF16)

    def l2n(z):
        return z * lax.rsqrt(jnp.sum(z * z, axis=-1, keepdims=True) + EPS)

    heads = range(B_HEADS)
    lasts = [c * B_CHUNK if rev else (c + 1) * B_CHUNK - 1 for c in range(nch)]
    q, k, v, beta, gcol, eg, qkm, p, pw = ([None] * B_HEADS for _ in range(9))
    for h in heads:
        q[h] = l2n(act[:, h * 128:(h + 1) * 128]) * (B_DK ** -0.5)
        k[h] = l2n(act[:, B_W + h * 128:B_W + (h + 1) * 128])
        v[h] = act[:, 2 * B_W + h * 128:2 * B_W + (h + 1) * 128]
        cb = dirn * B_HEADS + h
        cg = 2 * B_HEADS + cb
        beta[h] = bg[:, cb:cb + 1]
        gcol[h] = gcum[:, cg:cg + 1]
        grow = gcum_t[cg:cg + 1, :]
        eg[h] = jnp.exp(gcol[h])
        decay = jnp.where(incl, jnp.exp(jnp.where(incl, gcol[h] - grow, 0.0)), 0.0)
        kb = k[h].astype(BF16)
        qkm[h] = _dot_nt(q[h].astype(BF16), kb) * decay
        pw[h] = jnp.where(strict, beta[h] * _dot_nt(kb, kb) * decay, 0.0)

    xr = ri ^ ci
    for lvl in range(6):
        joins = (xr >> lvl) == 1
        for h in heads:
            l_s = jnp.where(joins, pw[h], 0.0)
            if lvl == 0:
                p[h] = eye - l_s
            else:
                pb = p[h].astype(BF16)
                p[h] = p[h] - _dot(pb, _dot(l_s.astype(BF16), pb).astype(BF16))

    qku, qeff, mn = ([None] * B_HEADS for _ in range(3))
    for h in heads:
        uw = _dot3(p[h], jnp.concatenate([beta[h] * v[h], (beta[h] * eg[h]) * k[h]], axis=1))
        qkuw = _dot(qkm[h].astype(BF16), uw.astype(BF16))
        qku[h] = qkuw[:, 0:128]
        qeff[h] = (q[h] * eg[h] - qkuw[:, 128:256]).astype(BF16)
        glast = jnp.concatenate(
            [jnp.broadcast_to(gcol[h][r:r + 1, :], (B_CHUNK, 1)) for r in lasts], axis=0)
        kdec = (k[h] * jnp.exp(glast - gcol[h])).astype(BF16)
        mn[h] = _dot_tn(kdec, jnp.concatenate([by_chunk(uw[:, 128:256]), by_chunk(uw[:, 0:128])], axis=1))

    for step in range(nch):
        c = nch - 1 - step if rev else step
        r0, r1 = c * B_CHUNK, (c + 1) * B_CHUNK
        for h in heads:
            sh = s_ref[h]
            shb = sh.astype(BF16)
            o_ref[r0:r1, h * 128:(h + 1) * 128] = _dot(qeff[h][r0:r1], shb) + qku[h][r0:r1]
            mc = mn[h][:, c * 128:(c + 1) * 128].astype(BF16)
            nc = mn[h][:, (nch + c) * 128:(nch + c + 1) * 128]
            gl = jnp.exp(gcol[h][lasts[c]:lasts[c] + 1, :])
            s_ref[h] = sh * gl - _dot(mc, shb) + nc


def _gdn(qkv, bg, conv_w, rev):
    nb, t, w = qkv.shape
    nt = t // TM
    hb = TM // HALO
    idx = lambda j: _gdn_tile_index(j, nt, rev)
    main = lambda ww: pl.BlockSpec((None, TM, ww), lambda b, j: (b, idx(j), 0))
    prev = pl.BlockSpec((None, HALO, w), lambda b, j: (b, jnp.maximum(idx(j) * hb - 1, 0), 0))
    nxt = pl.BlockSpec((None, HALO, w), lambda b, j: (b, jnp.minimum((idx(j) + 1) * hb, t // HALO - 1), 0))
    return pl.pallas_call(
        functools.partial(_gdn_kernel, rev=rev, nt=nt),
        grid=(nb, nt),
        in_specs=[main(w), prev, nxt, main(128), _full(conv_w.shape)],
        out_specs=main(B_W),
        out_shape=jax.ShapeDtypeStruct((nb, t, B_W), F32),
        scratch_shapes=[pltpu.VMEM((B_HEADS, B_DK, B_DK), F32), pltpu.VMEM((TM + 2 * HALO, w), F32)],
        compiler_params=_cparams(("parallel", "arbitrary")),
        name="gdn_rev" if rev else "gdn_fwd",
    )(qkv, qkv, qkv, bg, conv_w)


def _even_out_kernel(x_ref, mod_ref, ya_ref, of_ref, or_ref, gate_ref, og_ref, w_ref, o_ref):
    d = D_MODEL
    ob = of_ref[...] + or_ref[...]
    gate = gate_ref[...]
    yb = jnp.concatenate(
        [_rms(ob[:, h * 128:(h + 1) * 128], og_ref[...]) * _silu(gate[:, h * 128:(h + 1) * 128])
         for h in range(B_HEADS)], axis=1)
    y = _dot(ya_ref[...], w_ref[0:512, :]) + _dot(yb.astype(BF16), w_ref[512:1024, :])
    o_ref[...] = x_ref[...] + mod_ref[:, 2 * d:3 * d] * y


def _even_out(x, mod, ya, of, orv, gate, og, w):
    nb, t, d = x.shape
    tile = lambda ww: pl.BlockSpec((None, TM, ww), lambda b, j: (b, j, 0))
    return pl.pallas_call(
        _even_out_kernel,
        grid=(nb, t // TM),
        in_specs=[tile(d), _mod_spec(nb), tile(512), tile(512), tile(512), tile(512), _full((1, 128)),
                  _full(w.shape)],
        out_specs=tile(d),
        out_shape=jax.ShapeDtypeStruct(x.shape, F32),
        compiler_params=_cparams(("parallel", "parallel")),
        name="even_out",
    )(x, mod, ya, of, orv, gate, og, w)


def _ffn_kernel(x_ref, prev_ref, next_ref, mod_ref, g_ref, wup_ref, cw_ref, wdn_ref, o_ref, acc_ref,
                *, ntt, nct):
    d = D_MODEL
    nb = x_ref.shape[0]
    j = pl.program_id(0)
    mod = mod_ref[...]
    shift, scale, gate = mod[:, 3 * d:4 * d], mod[:, 4 * d:5 * d], mod[:, 5 * d:6 * d]
    g = g_ref[...]
    lflag = jnp.logical_and(j != 0, j != nct).astype(F32)
    rflag = jnp.logical_and(j != nct - 1, j != ntt - 1).astype(F32)
    x3 = jnp.stack([x_ref[:, t * d:(t + 1) * d] for t in range(TTF)], axis=0)
    u3 = jnp.concatenate([(_modulate(prev_ref[...], g, shift, scale) * lflag)[None],
                          _modulate(x3, g, shift, scale),
                          (_modulate(next_ref[...], g, shift, scale) * rflag)[None]], axis=0)
    u = u3.reshape((TTF + 2) * nb, d).astype(BF16)
    rows = TTF * nb
    for c in range(FFN // FC):
        c0 = c * FC
        hg = _dot(u, wup_ref[:, c0:c0 + FC])
        hv = _dot(u, wup_ref[:, FFN + c0:FFN + c0 + FC])
        cg = None
        cv = None
        for kk in range(FFN_CONV):
            tg = cw_ref[kk:kk + 1, c0:c0 + FC] * hg[kk * nb:kk * nb + rows]
            tv = cw_ref[kk:kk + 1, FFN + c0:FFN + c0 + FC] * hv[kk * nb:kk * nb + rows]
            cg = tg if cg is None else cg + tg
            cv = tv if cv is None else cv + tv
        part = _dot((_silu(cg) * cv).astype(BF16), wdn_ref[c0:c0 + FC, :])
        if c == 0:
            acc_ref[...] = part
        else:
            acc_ref[...] += part
    out3 = x3 + gate * acc_ref[...].reshape(TTF, nb, d)
    for t in range(TTF):
        o_ref[:, t * d:(t + 1) * d] = out3[t]


def _ffn(x, mod2, g, wup, cw, wdn, n_ctx):
    nb, t, d = x.shape
    ntt, nct = t // TTF, n_ctx // TTF
    x2 = x.reshape(nb, t * d)
    tile = pl.BlockSpec((nb, TTF * d), lambda j: (0, j))
    prev = pl.BlockSpec((nb, d), lambda j: (0, jnp.maximum(j * TTF - 1, 0)))
    nxt = pl.BlockSpec((nb, d), lambda j: (0, jnp.minimum((j + 1) * TTF, t - 1)))
    modspec = pl.BlockSpec((None, nb, N_MOD * d), lambda j: (jnp.where(j < nct, 0, 1), 0, 0))
    resident = lambda shape: pl.BlockSpec(shape, lambda j: (0, 0), pipeline_mode=pl.Buffered(1))
    out = pl.pallas_call(
        functools.partial(_ffn_kernel, ntt=ntt, nct=nct),
        grid=(ntt,),
        in_specs=[tile, prev, nxt, modspec, _full((1, d)), resident(wup.shape), _full(cw.shape),
                  resident(wdn.shape)],
        out_specs=tile,
        out_shape=jax.ShapeDtypeStruct(x2.shape, F32),
        scratch_shapes=[pltpu.VMEM((TTF * nb, d), F32)],
        compiler_params=_cparams(("parallel",)),
        name="ffn",
    )(x2, x2, x2, mod2, g, wup, cw, wdn)
    return out.reshape(nb, t, d)


def _odd_in_kernel(x_ref, mod_ref, g_ref, c_ref, sa_ref, sb_ref, wr_ref, wq_ref, wkv_ref,
                   xr_ref, gate_ref, q_ref, k_ref, v_ref):
    d = D_MODEL
    mod = mod_ref[...]
    u = _modulate(x_ref[...], g_ref[...], mod[:, 0:d], mod[:, d:2 * d]).astype(BF16)
    c, sa, sb = c_ref[...], sa_ref[...], sb_ref[...]
    xr_ref[...] = _dot(u, wr_ref[:, 0:C_WIDTH])
    gate_ref[...] = _dot(u, wr_ref[:, C_WIDTH:2 * C_WIDTH])
    q_ref[...] = (_rope(_dot(u, wq_ref[...]), c, sa, sb) * (D_HD ** -0.5 * LOG2E)).astype(BF16)
    k_ref[...] = _rope(_dot(u, wkv_ref[:, 0:256]), c, sa, sb).astype(BF16)
    v = _dot(u, wkv_ref[:, 256:512])
    vlane = lax.broadcasted_iota(jnp.int32, v.shape, 1)
    v_ref[...] = jnp.where(vlane % 128 == D_HD, 1.0, v).astype(BF16)


def _odd_in(x, mod, g, tabs, wr, wq, wkv):
    nb, t, d = x.shape
    tile = lambda w: pl.BlockSpec((None, TM, w), lambda b, j: (b, j, 0))
    tmaj = pl.BlockSpec((TM, C_WIDTH), lambda b, j: (j, b))
    tab = pl.BlockSpec((TM, 128), lambda b, j: (j, 0))
    outs = [jax.ShapeDtypeStruct((t, nb * C_WIDTH), F32)] * 2 + [
        jax.ShapeDtypeStruct((nb, t, 512), BF16), jax.ShapeDtypeStruct((nb, t, 256), BF16),
        jax.ShapeDtypeStruct((nb, t, 256), BF16)]
    return pl.pallas_call(
        _odd_in_kernel,
        grid=(nb, t // TM),
        in_specs=[tile(d), _mod_spec(nb), _full((1, d)), tab, tab, tab, _full(wr.shape), _full(wq.shape),
                  _full(wkv.shape)],
        out_specs=[tmaj, tmaj, tile(512), tile(256), tile(256)],
        out_shape=outs,
        compiler_params=_cparams(("parallel", "parallel")),
        name="odd_in",
    )(x, mod, g, *tabs, wr, wq, wkv)


def _lru_tile_index(j, ntt, nct, rev):
    return jnp.where(j < nct, nct - 1 - j, ntt + nct - 1 - j) if rev else j


def _lru_kernel(x_ref, prev_ref, next_ref, cw_ref, cb_ref, w_ref, b_ref, lam_ref, o_ref,
                h_ref, a_ref, bc_ref, *, rev, ntt, nct):
    j = pl.program_id(0)
    jj = _lru_tile_index(j, ntt, nct, rev)
    nb = x_ref.shape[1]

    @pl.when(j == 0)
    def _():
        h_ref[...] = jnp.zeros_like(h_ref)

    lflag = jnp.logical_and(jj != 0, jj != nct).astype(F32)
    rflag = jnp.logical_and(jj != nct - 1, jj != ntt - 1).astype(F32)
    ext = jnp.concatenate([prev_ref[...] * lflag, x_ref[...], next_ref[...] * rflag], axis=0)
    xc = cb_ref[...]
    for kk in range(C_CONV):
        xc = xc + cw_ref[kk:kk + 1, :] * ext[kk:kk + TT]
    xc2 = xc.reshape(TT * nb, C_WIDTH)
    z = _dot(xc2.astype(BF16), w_ref[...]) + b_ref[...]
    r = _sigmoid(z[:, 0:C_WIDTH])
    gi = _sigmoid(z[:, C_WIDTH:2 * C_WIDTH])
    log_a = -C_POW * r * _softplus(-lam_ref[...])
    a = jnp.exp(log_a)
    bc = jnp.sqrt(1.0 - jnp.exp(2.0 * log_a)) * (gi * xc2)
    a_ref[...] = a.reshape(TT, nb, C_WIDTH)
    bc_ref[...] = bc.reshape(TT, nb, C_WIDTH)

    def body(s, h):
        t = TT - 1 - s if rev else s
        h = a_ref[t] * h + bc_ref[t]
        o_ref[t] = h
        return h

    h_ref[...] = lax.fori_loop(0, TT, body, h_ref[...], unroll=8)


def _lru(xr, conv_w, conv_b, w, b, lam, n_ctx, rev):
    t, nb, c = xr.shape
    ntt, nct = t // TT, n_ctx // TT
    idx = lambda j: _lru_tile_index(j, ntt, nct, rev)
    main = pl.BlockSpec((TT, nb, c), lambda j: (idx(j), 0, 0))
    prev = pl.BlockSpec((2, nb, c), lambda j: (jnp.maximum(idx(j) * (TT // 2) - 1, 0), 0, 0))
    nxt = pl.BlockSpec((1, nb, c), lambda j: (jnp.minimum((idx(j) + 1) * TT, t - 1), 0, 0))
    return pl.pallas_call(
        functools.partial(_lru_kernel, rev=rev, ntt=ntt, nct=nct),
        grid=(ntt,),
        in_specs=[main, prev, nxt, _full(conv_w.shape), _full(conv_b.shape), _full(w.shape), _full(b.shape),
                  _full(lam.shape)],
        out_specs=main,
        out_shape=jax.ShapeDtypeStruct(xr.shape, F32),
        scratch_shapes=[pltpu.VMEM((nb, c), F32), pltpu.VMEM((TT, nb, c), F32), pltpu.VMEM((TT, nb, c), F32)],
        compiler_params=_cparams(("arbitrary",)),
        name="lru_rev" if rev else "lru_fwd",
    )(xr, xr, xr, conv_w, conv_b, w, b, lam)


def _win_attn_kernel(q_ref, k_ref, v_ref, sink_ref, o_ref, *, n_ctx, n_lat):
    i = pl.program_id(1)
    nctx_tiles = n_ctx // TQ
    nlb = n_lat // TQ
    grp = D_HEADS // D_KV
    width = grp * TQ
    lane = lax.broadcasted_iota(jnp.int32, (TQ, 128), 1)
    low = lane < D_HD
    srow = lax.broadcasted_iota(jnp.int32, (SINK_ROWS, width), 0)
    vrow = lax.broadcasted_iota(jnp.int32, (SINK_ROWS, 128), 0)
    vlane = lax.broadcasted_iota(jnp.int32, (SINK_ROWS, 128), 1)
    v_sink = jnp.where(jnp.logical_and(vrow == 0, vlane == D_HD), 1.0, 0.0).astype(BF16)

    def stacked_q(g):
        parts = []
        for sl in range(grp // 2):
            slab = q_ref[:, (g * (grp // 2) + sl) * 128:(g * (grp // 2) + sl + 1) * 128]
            zero = jnp.zeros_like(slab)
            parts += [jnp.where(low, slab, zero), jnp.where(low, zero, slab)]
        return jnp.concatenate(parts, axis=0)

    def sink_row(g):
        return jnp.concatenate(
            [jnp.broadcast_to(sink_ref[g * grp + hh:g * grp + hh + 1, 0:1], (1, TQ)) for hh in range(grp)], axis=1)

    def finish(g, o):
        out = o[:, 0:D_HD] / o[:, D_HD:D_HD + 1]
        out = jnp.concatenate([out, jnp.zeros_like(out)], axis=1)
        for sl in range(grp // 2):
            a = out[(2 * sl) * TQ:(2 * sl + 1) * TQ]
            b = pltpu.roll(out[(2 * sl + 1) * TQ:(2 * sl + 2) * TQ], D_HD, 1)
            col = (g * (grp // 2) + sl) * 128
            o_ref[:, col:col + 128] = jnp.where(low, a, b).astype(o_ref.dtype)

    def attend(g, local):
        qz, sk = stacked_q(g), sink_row(g)
        kc = k_ref[0:n_ctx, g * 128:(g + 1) * 128]
        vc = v_ref[0:n_ctx, g * 128:(g + 1) * 128]
        sc = _dot_nt(kc, qz)
        m = jnp.maximum(jnp.max(sc, axis=0, keepdims=True), sk)
        if local is not None:
            start, mask = local
            kl = k_ref[pl.ds(start, 3 * TQ), g * 128:(g + 1) * 128]
            vl = v_ref[pl.ds(start, 3 * TQ), g * 128:(g + 1) * 128]
            sl_ = jnp.where(mask, _dot_nt(kl, qz), NEG_INF)
            m = jnp.maximum(m, jnp.max(sl_, axis=0, keepdims=True))
        e_sink = jnp.where(srow == 0, jnp.exp2(sk - m), 0.0).astype(BF16)
        o = _dot_tn(jnp.exp2(sc - m).astype(BF16), vc) + _dot_tn(e_sink, v_sink)
        if local is not None:
            o = o + _dot_tn(jnp.exp2(sl_ - m).astype(BF16), vl)
        finish(g, o)

    @pl.when(i < nctx_tiles)
    def _():
        for g in range(D_KV):
            attend(g, None)

    @pl.when(i >= nctx_tiles)
    def _():
        il = i - nctx_tiles
        kb = jnp.clip(il - 1, 0, nlb - 3)
        start = pl.multiple_of(n_ctx + kb * TQ, TQ)
        kpos = kb * TQ + lax.broadcasted_iota(jnp.int32, (3 * TQ, width), 0)
        qpos = il * TQ + (lax.broadcasted_iota(jnp.int32, (3 * TQ, width), 1) % TQ)
        mask = jnp.abs(kpos - qpos) <= WINDOW
        for g in range(D_KV):
            attend(g, (start, mask))


def _win_attn(q, k, v, sink, n_ctx):
    nb, t, _ = q.shape
    kv = pl.BlockSpec((None, t, 256), lambda b, i: (b, 0, 0))
    qo = pl.BlockSpec((None, TQ, 512), lambda b, i: (b, i, 0))
    return pl.pallas_call(
        functools.partial(_win_attn_kernel, n_ctx=n_ctx, n_lat=t - n_ctx),
        grid=(nb, t // TQ),
        in_specs=[qo, kv, kv, _full(sink.shape)],
        out_specs=qo,
        out_shape=jax.ShapeDtypeStruct((nb, t, D_HEADS * D_HD), BF16),
        compiler_params=_cparams(("parallel", "arbitrary")),
        name="win_attn",
    )(q, k, v, sink)


def _odd_out_kernel(x_ref, mod_ref, hf_ref, hr_ref, gate_ref, od_ref, w_ref, o_ref):
    d = D_MODEL
    yc = (hf_ref[...] + hr_ref[...]) * _gelu_tanh(gate_ref[...])
    y = _dot(yc.astype(BF16), w_ref[0:512, :]) + _dot(od_ref[...], w_ref[512:1024, :])
    o_ref[...] = x_ref[...] + mod_ref[:, 2 * d:3 * d] * y


def _odd_out(x, mod, hf, hr, gate, od, w):
    nb, t, d = x.shape
    tile = lambda ww: pl.BlockSpec((None, TM, ww), lambda b, j: (b, j, 0))
    tmaj = pl.BlockSpec((TM, C_WIDTH), lambda b, j: (j, b))
    return pl.pallas_call(
        _odd_out_kernel,
        grid=(nb, t // TM),
        in_specs=[tile(d), _mod_spec(nb), tmaj, tmaj, tmaj, tile(512), _full(w.shape)],
        out_specs=tile(d),
        out_shape=jax.ShapeDtypeStruct(x.shape, F32),
        compiler_params=_cparams(("parallel", "parallel")),
        name="odd_out",
    )(x, mod, hf, hr, gate, od, w)


def _final_kernel(x_ref, g_ref, o_ref):
    o_ref[...] = _rms(x_ref[...], g_ref[...])


def _final(x, g, n_ctx):
    nb, t, d = x.shape
    off = n_ctx // TM
    return pl.pallas_call(
        _final_kernel,
        grid=(nb, (t - n_ctx) // TM),
        in_specs=[pl.BlockSpec((None, TM, d), lambda b, j: (b, j + off, 0)), _full((1, d))],
        out_specs=pl.BlockSpec((None, TM, d), lambda b, j: (b, j, 0)),
        out_shape=jax.ShapeDtypeStruct((nb, t - n_ctx, d), F32),
        compiler_params=_cparams(("parallel", "parallel")),
        name="final_norm",
    )(x, g)


def _rope_tables(n_ctx, n_lat):
    pos = np.arange(n_lat)
    inv = ROPE_THETA ** (-np.arange(0, ROT_AXIS, 2, dtype=np.float64) / ROT_AXIS)
    ang_r = (pos // GRID_W)[:, None] * inv
    ang_c = (pos % GRID_W)[:, None] * inv
    lane = np.arange(128) % 64
    seg, f = lane // 16, lane % 16
    ang = np.where(seg[None, :] < 2, ang_r[:, f], ang_c[:, f])
    c = np.cos(ang)
    s = np.sin(ang)
    sa = np.where((seg % 2 == 0)[None, :], -s, 0.0)
    sb = np.where((seg % 2 == 1)[None, :], s, 0.0)
    pad = lambda a, fill: np.concatenate([np.full((n_ctx, 128), fill), a], axis=0).astype(np.float32)
    return jnp.asarray(pad(c, 1.0)), jnp.asarray(pad(sa, 0.0)), jnp.asarray(pad(sb, 0.0))


def _block_diag(w):
    eye = jnp.eye(C_BLOCKS, dtype=w.dtype)
    return jnp.einsum('hij,hg->higj', w, eye).reshape(C_WIDTH, C_WIDTH)


def kernel(x, c, ctx, c_ctx, w_ada, b_ada, norm_mix, norm_ffn, ffn_w_up, ffn_conv, ffn_w_down, final_norm,
           ev_w_in, ev_w_out, diff_lambda, diff_subln, gdn_conv, gdn_a_log, gdn_dt_bias, gdn_norm,
           od_w_in, od_w_out, lru_conv, lru_conv_b, lru_wa, lru_ba, lru_wx, lru_bx, lru_lambda, swa_sink):
    nb, n_lat, d = x.shape
    n_ctx = ctx.shape[1]
    depth = w_ada.shape[0]
    assert d == D_MODEL and n_ctx == TM and n_lat % TM == 0 and n_lat // TQ >= 3 and nb < MOD_ROWS
    t = n_ctx + n_lat

    xa = jnp.concatenate([ctx, x], axis=1)
    c_all = jnp.zeros((MOD_ROWS, d), F32).at[0:nb].set(c).at[nb].set(c_ctx)
    mod_all = _modulation(c_all, w_ada, b_ada).reshape(depth, MOD_ROWS, 1, N_MOD * d)
    tabs = _rope_tables(n_ctx, n_lat)
    row = lambda v: v.reshape(1, -1).astype(F32)

    for layer in range(depth):
        jx = layer // 2
        mod = mod_all[layer]
        mod2 = jnp.stack([jnp.broadcast_to(mod[nb], (nb, N_MOD * d)), mod[0:nb, 0]], axis=0)
        if layer % 2 == 0:
            lam_init = 0.8 - 0.6 * math.exp(-0.3 * layer)
            w_in = ev_w_in[jx]
            wa = w_in[:, 0:1536].astype(BF16)
            wg = w_in[:, 1536:3072].astype(BF16)
            wgate = w_in[:, 3072:3584].astype(BF16)
            wba = jnp.pad(w_in[:, 3584:3600], ((0, 0), (0, 112))).astype(BF16)
            pad16 = lambda v: jnp.pad(v.reshape(1, 8).astype(F32), ((0, 0), (8, 112)))
            q, k, v, qkv, gate, bg = _even_in(xa, mod, row(norm_mix[layer]), tabs, wa, wg, wgate, wba,
                                              pad16(gdn_a_log[jx]), pad16(gdn_dt_bias[jx]))
            ya = _diff_attn(q, k, v, diff_lambda[jx].astype(F32), diff_subln[jx].astype(F32).reshape(-1, 1),
                            lam_init, n_ctx)
            of = _gdn(qkv, bg, gdn_conv[jx].astype(F32), rev=False)
            orv = _gdn(qkv, bg, gdn_conv[jx].astype(F32), rev=True)
            xa = _even_out(xa, mod, ya, of, orv, gate, row(gdn_norm[jx]), ev_w_out[jx].astype(BF16))
        else:
            w_in = od_w_in[jx]
            wr = w_in[:, 0:1024].astype(BF16)
            wq = w_in[:, 1024:1536].astype(BF16)
            dup = lambda w: jnp.concatenate([w[:, 0:64], w[:, 0:64], w[:, 64:128], w[:, 64:128]], axis=1)
            zpad = lambda w: jnp.concatenate([w[:, 0:64], jnp.zeros_like(w[:, 0:64]), w[:, 64:128], jnp.zeros_like(w[:, 0:64])], axis=1)
            wkv = jnp.concatenate([dup(w_in[:, 1536:1664]), zpad(w_in[:, 1664:1792])], axis=1).astype(BF16)
            xr, gate, q, k, v = _odd_in(xa, mod, row(norm_mix[layer]), tabs, wr, wq, wkv)
            xr3 = xr.reshape(t, nb, C_WIDTH)
            hs = []
            for dd in range(2):
                wbig = jnp.concatenate([_block_diag(lru_wa[jx, dd]), _block_diag(lru_wx[jx, dd])], axis=1)
                bbig = jnp.concatenate([lru_ba[jx, dd], lru_bx[jx, dd]]).reshape(1, -1).astype(F32)
                hs.append(_lru(xr3, lru_conv[jx].astype(F32), row(lru_conv_b[jx]), wbig.astype(BF16), bbig,
                               row(lru_lambda[jx, dd]), n_ctx, rev=(dd == 1)).reshape(t, nb * C_WIDTH))
            sink = jnp.broadcast_to(swa_sink[jx].astype(F32)[:, None] * LOG2E, (D_HEADS, 128))
            od = _win_attn(q, k, v, sink, n_ctx)
            xa = _odd_out(xa, mod, hs[0], hs[1], gate, od, od_w_out[jx].astype(BF16))
        xa = _ffn(xa, mod2, row(norm_ffn[layer]), ffn_w_up[layer].astype(BF16), ffn_conv[layer].astype(F32),
                  ffn_w_down[layer].astype(BF16), n_ctx)
    return _final(xa, row(final_norm), n_ctx)
```

```python
import functools
import math

import jax
import jax.numpy as jnp
import numpy as np
from jax import lax
from jax.experimental import pallas as pl
from jax.experimental.pallas import tpu as pltpu

F32 = jnp.float32
BF16 = jnp.bfloat16
HIGHEST = lax.Precision.HIGHEST

D_MODEL = 1024
GRID_W = 64
EPS = 1e-6
NEG_INF = -1e30
N_MOD = 6
ROPE_THETA = 10000.0
ROT_AXIS = 32
A_HEADS = 4
A_HD = 64
A_VD = 128
B_HEADS = 4
B_DK = 128
B_W = 512
B_CONV = 4
B_CHUNK = 64
C_WIDTH = 512
C_BLOCKS = 8
C_BD = 64
C_CONV = 4
C_POW = 8.0
D_HEADS = 8
D_KV = 2
D_HD = 64
WINDOW = 128
FFN = 2816
FFN_CONV = 3

TM = 256
TQ = 128
TQA = 256
KB = 256
ONES_ROWS = 16
SINK_ROWS = 16
LOG2E = math.log2(math.e)
TT = 64
HALO = 8
FC = 256
TTF = 64
MOD_ROWS = 16
VMEM_LIMIT = 56 * 1024 * 1024


def _cparams(sem):
    return pltpu.CompilerParams(dimension_semantics=sem, vmem_limit_bytes=VMEM_LIMIT)


def _sigmoid(x):
    return 1.0 / (1.0 + jnp.exp(-x))


def _silu(x):
    return x * _sigmoid(x)


def _softplus(x):
    return jnp.maximum(x, 0.0) + jnp.log(1.0 + jnp.exp(-jnp.abs(x)))


def _gelu_tanh(x):
    return 0.5 * x * (1.0 + jnp.tanh(math.sqrt(2.0 / math.pi) * (x + 0.044715 * (x * x * x))))


def _dot(a, b):
    return jnp.dot(a, b, preferred_element_type=F32)


def _dot_hi(a, b):
    return jnp.dot(a, b, preferred_element_type=F32, precision=HIGHEST)


def _dot3(a, b):
    ah = a.astype(BF16)
    al = (a - ah.astype(F32)).astype(BF16)
    bh = b.astype(BF16)
    bl = (b - bh.astype(F32)).astype(BF16)
    return _dot(ah, bh) + (_dot(ah, bl) + _dot(al, bh))


def _dot_nt(a, b):
    return lax.dot_general(a, b, (((1,), (1,)), ((), ())), preferred_element_type=F32)


def _dot_tn(a, b):
    return lax.dot_general(a, b, (((0,), (0,)), ((), ())), preferred_element_type=F32)


def _rms(x, g):
    return x * lax.rsqrt(jnp.mean(x * x, axis=-1, keepdims=True) + EPS) * g


def _modulate(x, g, shift, scale):
    return _rms(x, g) * (1.0 + scale) + shift


def _rope128(z, c, sa, sb):
    return z * c + pltpu.roll(z, 112, 1) * sa + pltpu.roll(z, 16, 1) * sb


def _rope(z, c, sa, sb):
    n = z.shape[1] // 128
    return jnp.concatenate([_rope128(z[:, i * 128:(i + 1) * 128], c, sa, sb) for i in range(n)], axis=1)


def _mod_kernel(s_ref, w_ref, b_ref, o_ref):
    s = _silu(s_ref[...])
    o_ref[...] = _dot_hi(s, w_ref[...]) + b_ref[...]


def _modulation(c_all, w_ada, b_ada):
    depth, d, n = w_ada.shape
    tn = 1536
    return pl.pallas_call(
        _mod_kernel,
        grid=(depth, n // tn),
        in_specs=[pl.BlockSpec((MOD_ROWS, d), lambda l, j: (0, 0)),
                  pl.BlockSpec((None, d, tn), lambda l, j: (l, 0, j)),
                  pl.BlockSpec((None, 1, tn), lambda l, j: (l, 0, j))],
        out_specs=pl.BlockSpec((None, MOD_ROWS, tn), lambda l, j: (l, 0, j)),
        out_shape=jax.ShapeDtypeStruct((depth, MOD_ROWS, n), F32),
        compiler_params=_cparams(("arbitrary", "arbitrary")),
        name="modulation",
    )(c_all, w_ada, b_ada.reshape(depth, 1, n))


def _mod_spec(nb):
    return pl.BlockSpec((None, 1, N_MOD * D_MODEL), lambda b, j: (jnp.where(j == 0, nb, b), 0, 0))


def _full(shape):
    nd = len(shape)
    return pl.BlockSpec(shape, lambda *_: (0,) * nd)


def _even_in_kernel(x_ref, prev_ref, next_ref, mod_ref, g_ref, c_ref, sa_ref, sb_ref, wa_ref, wg_ref, wgate_ref,
                    wba_ref, cw_ref, alog_ref, dtb_ref, q_ref, k_ref, v_ref, qkv_ref, gate_ref, bg_ref, ext_ref,
                    *, nt):
    d = D_MODEL
    j = pl.program_id(1)
    mod = mod_ref[...]
    g, shift, scale = g_ref[...], mod[:, 0:d], mod[:, d:2 * d]
    uf = _modulate(x_ref[...], g, shift, scale)
    u = uf.astype(BF16)
    c, sa, sb = c_ref[...], sa_ref[...], sb_ref[...]
    q = _dot(u, wa_ref[:, 0:512])
    q_ref[...] = (_rope(q, c, sa, sb) * (A_HD ** -0.5 * LOG2E)).astype(BF16)
    k = _dot(u, wa_ref[:, 512:1024])
    k_ref[...] = _rope(k, c, sa, sb).astype(BF16)
    v_ref[...] = _dot(u, wa_ref[:, 1024:1536]).astype(BF16)
    gate_ref[...] = _dot(u, wgate_ref[...]).astype(gate_ref.dtype)
    z = _dot(u, wba_ref[...])
    lane = lax.broadcasted_iota(jnp.int32, z.shape, 1)
    beta = _sigmoid(z)
    gdec = -jnp.exp(alog_ref[...]) * _softplus(z + dtb_ref[...])
    bg_ref[...] = jnp.where(lane < 2 * B_HEADS, beta, jnp.where(lane < 4 * B_HEADS, gdec, 0.0))

    lflag = (j >= 2).astype(F32)
    rflag = jnp.logical_and(j >= 1, j <= nt - 2).astype(F32)
    halo = jnp.concatenate([_modulate(prev_ref[...], g, shift, scale) * lflag,
                            _modulate(next_ref[...], g, shift, scale) * rflag], axis=0).astype(BF16)
    zh = _dot(halo, wg_ref[...])
    ext_ref[0:HALO, :] = zh[0:HALO]
    ext_ref[HALO:HALO + TM, :] = _dot(u, wg_ref[...])
    ext_ref[HALO + TM:2 * HALO + TM, :] = zh[HALO:2 * HALO]
    acc = None
    for kk in range(B_CONV):
        term = cw_ref[kk:kk + 1, :] * ext_ref[pl.ds(HALO - 2 + kk, TM), :]
        acc = term if acc is None else acc + term
    act = _silu(acc)

    def l2n(zz):
        return zz * lax.rsqrt(jnp.sum(zz * zz, axis=-1, keepdims=True) + EPS)

    for h in range(B_HEADS):
        qkv_ref[:, h * 128:(h + 1) * 128] = l2n(act[:, h * 128:(h + 1) * 128]) * (B_DK ** -0.5)
        qkv_ref[:, B_W + h * 128:B_W + (h + 1) * 128] = l2n(act[:, B_W + h * 128:B_W + (h + 1) * 128])
    qkv_ref[:, 2 * B_W:3 * B_W] = act[:, 2 * B_W:3 * B_W]


def _even_in(x, mod, g, tabs, wa, wg, wgate, wba, cw, alog, dtb):
    nb, t, d = x.shape
    nt = t // TM
    hb = TM // HALO
    tile = lambda w: pl.BlockSpec((None, TM, w), lambda b, j: (b, j, 0))
    prev = pl.BlockSpec((None, HALO, d), lambda b, j: (b, jnp.maximum(j * hb - 1, 0), 0))
    nxt = pl.BlockSpec((None, HALO, d), lambda b, j: (b, jnp.minimum((j + 1) * hb, t // HALO - 1), 0))
    tab = pl.BlockSpec((TM, 128), lambda b, j: (j, 0))
    outs = [jax.ShapeDtypeStruct((nb, t, 512), BF16)] * 3 + [
        jax.ShapeDtypeStruct((nb, t, 1536), F32), jax.ShapeDtypeStruct((nb, t, 512), BF16),
        jax.ShapeDtypeStruct((nb, t, 128), F32)]
    return pl.pallas_call(
        functools.partial(_even_in_kernel, nt=nt),
        grid=(nb, nt),
        in_specs=[tile(d), prev, nxt, _mod_spec(nb), _full((1, d)), tab, tab, tab, _full(wa.shape), _full(wg.shape),
                  _full(wgate.shape), _full(wba.shape), _full(cw.shape), _full((1, 128)), _full((1, 128))],
        out_specs=[tile(512), tile(512), tile(512), tile(1536), tile(512), tile(128)],
        out_shape=outs,
        scratch_shapes=[pltpu.VMEM((TM + 2 * HALO, 1536), F32)],
        compiler_params=_cparams(("parallel", "parallel")),
        name="even_in",
    )(x, x, x, mod, g, *tabs, wa, wg, wgate, wba, cw, alog, dtb)


def _diff_attn_kernel(q_ref, k_ref, v_ref, lv_ref, g_ref, o_ref, vt_ref, sa_ref, sb_ref, ma_ref, mb_ref,
                      *, lam_init, n_ctx):
    i = pl.program_id(2)
    t = k_ref.shape[0]
    nblk = t // KB

    @pl.when(i == 0)
    def _():
        vt_ref[0:A_VD, :] = v_ref[...].astype(F32).T.astype(BF16)
        orow = lax.broadcasted_iota(jnp.int32, (ONES_ROWS, t), 0)
        vt_ref[A_VD:A_VD + ONES_ROWS, :] = jnp.where(orow == 0, 1.0, 0.0).astype(BF16)
        sb_ref[...] = jnp.zeros_like(sb_ref)
        mb_ref[...] = jnp.zeros_like(mb_ref)

    def step(s_new, m_new, s_old, m_old):
        lv = lv_ref[...]
        lam = (jnp.exp(jnp.sum(lv[0:1] * lv[1:2], axis=-1, keepdims=True))
               - jnp.exp(jnp.sum(lv[2:3] * lv[3:4], axis=-1, keepdims=True)) + lam_init)
        m = jnp.max(m_old[...], axis=0, keepdims=True)
        is_ctx = jnp.minimum(i, pl.num_programs(2) - 2) < n_ctx // TQA
        q = q_ref[...]
        lane = lax.broadcasted_iota(jnp.int32, q.shape, 1)
        zero = jnp.zeros_like(q)
        qz = jnp.concatenate([jnp.where(lane < A_HD, q, zero), jnp.where(lane >= A_HD, q, zero)], axis=0)
        mrun = None
        oe = None
        for kb in range(nblk):
            rows = slice(kb * KB, (kb + 1) * KB)
            e = jnp.exp2(s_old[rows, :] - m).astype(BF16)
            part = _dot(vt_ref[:, rows], e)
            oe = part if oe is None else oe + part
            sblk = _dot_nt(k_ref[rows, :], qz)
            if kb >= n_ctx // KB:
                sblk = jnp.where(is_ctx, NEG_INF, sblk)
            s_new[rows, :] = sblk
            part = jnp.max(sblk.reshape(KB // 8, 8, 2 * TQA), axis=0)
            mrun = part if mrun is None else jnp.maximum(mrun, part)
        m_new[...] = mrun
        on = oe[0:A_VD] / oe[A_VD:A_VD + 1]
        od = on[:, 0:TQA] - lam * on[:, TQA:2 * TQA]
        y = od * lax.rsqrt(jnp.mean(od * od, axis=0, keepdims=True) + EPS) * (g_ref[...] * (1.0 - lam_init))
        o_ref[...] = y.T.astype(o_ref.dtype)

    @pl.when(i % 2 == 0)
    def _():
        step(sa_ref, ma_ref, sb_ref, mb_ref)

    @pl.when(i % 2 == 1)
    def _():
        step(sb_ref, mb_ref, sa_ref, ma_ref)


def _diff_attn(q, k, v, lam_vec, subln, lam_init, n_ctx):
    nb, t, _ = q.shape
    nq = t // TQA
    kv = pl.BlockSpec((None, t, 128), lambda b, h, i: (b, 0, h))
    qin = pl.BlockSpec((None, TQA, 128), lambda b, h, i: (b, jnp.minimum(i, nq - 1), h))
    out = pl.BlockSpec((None, TQA, 128), lambda b, h, i: (b, jnp.maximum(i - 1, 0), h))
    return pl.pallas_call(
        functools.partial(_diff_attn_kernel, lam_init=lam_init, n_ctx=n_ctx),
        grid=(nb, A_HEADS, nq + 1),
        in_specs=[qin, kv, kv, _full((4, A_HD)), _full((A_VD, 1))],
        out_specs=out,
        out_shape=jax.ShapeDtypeStruct((nb, t, A_HEADS * A_VD), BF16),
        scratch_shapes=[pltpu.VMEM((A_VD + ONES_ROWS, t), BF16), pltpu.VMEM((t, 2 * TQA), F32),
                        pltpu.VMEM((t, 2 * TQA), F32), pltpu.VMEM((8, 2 * TQA), F32), pltpu.VMEM((8, 2 * TQA), F32)],
        compiler_params=_cparams(("parallel", "parallel", "arbitrary")),
        name="diff_attn",
    )(q, k, v, lam_vec, subln)


def _gdn_tile_index(j, nt, rev):
    return jnp.where(j == 0, 0, nt - j) if rev else j


def _gdn_kernel(qkv_ref, bg_ref, o_ref, s_ref, *, rev, nt):
    j = pl.program_id(1)
    dirn = 1 if rev else 0
    nch = TM // B_CHUNK

    @pl.when(j == 0)
    def _():
        s_ref[...] = jnp.zeros_like(s_ref)

    bg = bg_ref[...]
    ri = lax.broadcasted_iota(jnp.int32, (TM, TM), 0)
    ci = lax.broadcasted_iota(jnp.int32, (TM, TM), 1)
    same = (ri // B_CHUNK) == (ci // B_CHUNK)
    incl = jnp.logical_and(same, (ri <= ci) if rev else (ri >= ci))
    strict = jnp.logical_and(same, (ri < ci) if rev else (ri > ci))
    eye = (ri == ci).astype(F32)
    gcum = _dot_hi(incl.astype(F32), bg)
    gcum_t = gcum.T
    rchunk = lax.broadcasted_iota(jnp.int32, (TM, B_DK), 0) // B_CHUNK

    def by_chunk(z):
        return jnp.concatenate([jnp.where(rchunk == c, z, 0.0) for c in range(nch)], axis=1).astype(BF16)

    heads = range(B_HEADS)
    lasts = [c * B_CHUNK if rev else (c + 1) * B_CHUNK - 1 for c in range(nch)]
    q, k, v, beta, gcol, eg, qkm, p, pw = ([None] * B_HEADS for _ in range(9))
    for h in heads:
        q[h] = qkv_ref[:, h * 128:(h + 1) * 128]
        k[h] = qkv_ref[:, B_W + h * 128:B_W + (h + 1) * 128]
        v[h] = qkv_ref[:, 2 * B_W + h * 128:2 * B_W + (h + 1) * 128]
        cb = dirn * B_HEADS + h
        cg = 2 * B_HEADS + cb
        beta[h] = bg[:, cb:cb + 1]
        gcol[h] = gcum[:, cg:cg + 1]
        grow = gcum_t[cg:cg + 1, :]
        eg[h] = jnp.exp(gcol[h])
        decay = jnp.where(incl, jnp.exp(jnp.where(incl, gcol[h] - grow, 0.0)), 0.0)
        kb = k[h].astype(BF16)
        qkm[h] = _dot_nt(q[h].astype(BF16), kb) * decay
        pw[h] = jnp.where(strict, beta[h] * _dot_nt(kb, kb) * decay, 0.0)

    xr = ri ^ ci
    for lvl in range(6):
        joins = (xr >> lvl) == 1
        for h in heads:
            l_s = jnp.where(joins, pw[h], 0.0)
            if lvl == 0:
                p[h] = eye - l_s
            else:
                pb = p[h].astype(BF16)
                p[h] = p[h] - _dot(pb, _dot(l_s.astype(BF16), pb).astype(BF16))

    qku, qeff, mn = ([None] * B_HEADS for _ in range(3))
    for h in heads:
        uw = _dot3(p[h], jnp.concatenate([beta[h] * v[h], (beta[h] * eg[h]) * k[h]], axis=1))
        qkuw = _dot(qkm[h].astype(BF16), uw.astype(BF16))
        qku[h] = qkuw[:, 0:128]
        qeff[h] = (q[h] * eg[h] - qkuw[:, 128:256]).astype(BF16)
        glast = jnp.concatenate(
            [jnp.broadcast_to(gcol[h][r:r + 1, :], (B_CHUNK, 1)) for r in lasts], axis=0)
        kdec = (k[h] * jnp.exp(glast - gcol[h])).astype(BF16)
        mn[h] = _dot_tn(kdec, jnp.concatenate([by_chunk(uw[:, 128:256]), by_chunk(uw[:, 0:128])], axis=1))

    for step in range(nch):
        c = nch - 1 - step if rev else step
        r0, r1 = c * B_CHUNK, (c + 1) * B_CHUNK
        for h in heads:
            sh = s_ref[h]
            shb = sh.astype(BF16)
            o_ref[r0:r1, h * 128:(h + 1) * 128] = (_dot(qeff[h][r0:r1], shb) + qku[h][r0:r1]).astype(o_ref.dtype)
            mc = mn[h][:, c * 128:(c + 1) * 128].astype(BF16)
            nc = mn[h][:, (nch + c) * 128:(nch + c + 1) * 128]
            gl = jnp.exp(gcol[h][lasts[c]:lasts[c] + 1, :])
            s_ref[h] = sh * gl - _dot(mc, shb) + nc


def _gdn(qkv, bg, rev):
    nb, t, w = qkv.shape
    nt = t // TM
    main = lambda ww: pl.BlockSpec((None, TM, ww), lambda b, j: (b, _gdn_tile_index(j, nt, rev), 0))
    return pl.pallas_call(
        functools.partial(_gdn_kernel, rev=rev, nt=nt),
        grid=(nb, nt),
        in_specs=[main(w), main(128)],
        out_specs=main(B_W),
        out_shape=jax.ShapeDtypeStruct((nb, t, B_W), BF16),
        scratch_shapes=[pltpu.VMEM((B_HEADS, B_DK, B_DK), F32)],
        compiler_params=_cparams(("parallel", "arbitrary")),
        name="gdn_rev" if rev else "gdn_fwd",
    )(qkv, bg)


def _even_out_kernel(x_ref, mod_ref, ya_ref, of_ref, or_ref, gate_ref, og_ref, w_ref, o_ref):
    d = D_MODEL
    ob = of_ref[...].astype(F32) + or_ref[...].astype(F32)
    gate = gate_ref[...].astype(F32)
    yb = jnp.concatenate(
        [_rms(ob[:, h * 128:(h + 1) * 128], og_ref[...]) * _silu(gate[:, h * 128:(h + 1) * 128])
         for h in range(B_HEADS)], axis=1)
    y = _dot(ya_ref[...], w_ref[0:512, :]) + _dot(yb.astype(BF16), w_ref[512:1024, :])
    o_ref[...] = x_ref[...] + mod_ref[:, 2 * d:3 * d] * y


def _even_out(x, mod, ya, of, orv, gate, og, w):
    nb, t, d = x.shape
    tile = lambda ww: pl.BlockSpec((None, TM, ww), lambda b, j: (b, j, 0))
    return pl.pallas_call(
        _even_out_kernel,
        grid=(nb, t // TM),
        in_specs=[tile(d), _mod_spec(nb), tile(512), tile(512), tile(512), tile(512), _full((1, 128)),
                  _full(w.shape)],
        out_specs=tile(d),
        out_shape=jax.ShapeDtypeStruct(x.shape, F32),
        compiler_params=_cparams(("parallel", "parallel")),
        name="even_out",
    )(x, mod, ya, of, orv, gate, og, w)


def _ffn_kernel(x_ref, prev_ref, next_ref, mod_ref, g_ref, wup_ref, cw_ref, wdn_ref, o_ref, acc_ref,
                *, ntt, nct):
    d = D_MODEL
    nb = x_ref.shape[0]
    j = pl.program_id(0)
    mod = mod_ref[...]
    shift, scale, gate = mod[:, 3 * d:4 * d], mod[:, 4 * d:5 * d], mod[:, 5 * d:6 * d]
    g = g_ref[...]
    lflag = jnp.logical_and(j != 0, j != nct).astype(F32)
    rflag = jnp.logical_and(j != nct - 1, j != ntt - 1).astype(F32)
    x3 = jnp.stack([x_ref[:, t * d:(t + 1) * d] for t in range(TTF)], axis=0)
    u3 = jnp.concatenate([(_modulate(prev_ref[...], g, shift, scale) * lflag)[None],
                          _modulate(x3, g, shift, scale),
                          (_modulate(next_ref[...], g, shift, scale) * rflag)[None]], axis=0)
    u = u3.reshape((TTF + 2) * nb, d).astype(BF16)
    rows = TTF * nb
    for c in range(FFN // FC):
        c0 = c * FC
        hg = _dot(u, wup_ref[:, c0:c0 + FC])
        hv = _dot(u, wup_ref[:, FFN + c0:FFN + c0 + FC])
        cg = None
        cv = None
        for kk in range(FFN_CONV):
            tg = cw_ref[kk:kk + 1, c0:c0 + FC] * hg[kk * nb:kk * nb + rows]
            tv = cw_ref[kk:kk + 1, FFN + c0:FFN + c0 + FC] * hv[kk * nb:kk * nb + rows]
            cg = tg if cg is None else cg + tg
            cv = tv if cv is None else cv + tv
        part = _dot((_silu(cg) * cv).astype(BF16), wdn_ref[c0:c0 + FC, :])
        if c == 0:
            acc_ref[...] = part
        else:
            acc_ref[...] += part
    out3 = x3 + gate * acc_ref[...].reshape(TTF, nb, d)
    for t in range(TTF):
        o_ref[:, t * d:(t + 1) * d] = out3[t]


def _ffn(x, mod2, g, wup, cw, wdn, n_ctx):
    nb, t, d = x.shape
    ntt, nct = t // TTF, n_ctx // TTF
    x2 = x.reshape(nb, t * d)
    tile = pl.BlockSpec((nb, TTF * d), lambda j: (0, j))
    prev = pl.BlockSpec((nb, d), lambda j: (0, jnp.maximum(j * TTF - 1, 0)))
    nxt = pl.BlockSpec((nb, d), lambda j: (0, jnp.minimum((j + 1) * TTF, t - 1)))
    modspec = pl.BlockSpec((None, nb, N_MOD * d), lambda j: (jnp.where(j < nct, 0, 1), 0, 0))
    resident = lambda shape: pl.BlockSpec(shape, lambda j: (0, 0), pipeline_mode=pl.Buffered(1))
    out = pl.pallas_call(
        functools.partial(_ffn_kernel, ntt=ntt, nct=nct),
        grid=(ntt,),
        in_specs=[tile, prev, nxt, modspec, _full((1, d)), resident(wup.shape), _full(cw.shape),
                  resident(wdn.shape)],
        out_specs=tile,
        out_shape=jax.ShapeDtypeStruct(x2.shape, F32),
        scratch_shapes=[pltpu.VMEM((TTF * nb, d), F32)],
        compiler_params=_cparams(("parallel",)),
        name="ffn",
    )(x2, x2, x2, mod2, g, wup, cw, wdn)
    return out.reshape(nb, t, d)


def _odd_in_kernel(x_ref, mod_ref, g_ref, c_ref, sa_ref, sb_ref, wr_ref, wq_ref, wkv_ref,
                   xr_ref, gate_ref, q_ref, k_ref, v_ref):
    d = D_MODEL
    mod = mod_ref[...]
    u = _modulate(x_ref[...], g_ref[...], mod[:, 0:d], mod[:, d:2 * d]).astype(BF16)
    c, sa, sb = c_ref[...], sa_ref[...], sb_ref[...]
    xr_ref[...] = _dot(u, wr_ref[:, 0:C_WIDTH])
    gate_ref[...] = _dot(u, wr_ref[:, C_WIDTH:2 * C_WIDTH])
    q_ref[...] = (_rope(_dot(u, wq_ref[...]), c, sa, sb) * (D_HD ** -0.5 * LOG2E)).astype(BF16)
    k_ref[...] = _rope(_dot(u, wkv_ref[:, 0:256]), c, sa, sb).astype(BF16)
    v = _dot(u, wkv_ref[:, 256:512])
    vlane = lax.broadcasted_iota(jnp.int32, v.shape, 1)
    v_ref[...] = jnp.where(vlane % 128 == D_HD, 1.0, v).astype(BF16)


def _odd_in(x, mod, g, tabs, wr, wq, wkv):
    nb, t, d = x.shape
    tile = lambda w: pl.BlockSpec((None, TM, w), lambda b, j: (b, j, 0))
    tmaj = pl.BlockSpec((TM, C_WIDTH), lambda b, j: (j, b))
    tab = pl.BlockSpec((TM, 128), lambda b, j: (j, 0))
    outs = [jax.ShapeDtypeStruct((t, nb * C_WIDTH), F32)] * 2 + [
        jax.ShapeDtypeStruct((nb, t, 512), BF16), jax.ShapeDtypeStruct((nb, t, 256), BF16),
        jax.ShapeDtypeStruct((nb, t, 256), BF16)]
    return pl.pallas_call(
        _odd_in_kernel,
        grid=(nb, t // TM),
        in_specs=[tile(d), _mod_spec(nb), _full((1, d)), tab, tab, tab, _full(wr.shape), _full(wq.shape),
                  _full(wkv.shape)],
        out_specs=[tmaj, tmaj, tile(512), tile(256), tile(256)],
        out_shape=outs,
        compiler_params=_cparams(("parallel", "parallel")),
        name="odd_in",
    )(x, mod, g, *tabs, wr, wq, wkv)


def _lru_tile_index(j, ntt, nct, rev):
    return jnp.where(j < nct, nct - 1 - j, ntt + nct - 1 - j) if rev else j


def _lru_kernel(x_ref, prev_ref, next_ref, cw_ref, cb_ref, w_ref, b_ref, lam_ref, o_ref,
                h_ref, a_ref, bc_ref, *, rev, ntt, nct):
    j = pl.program_id(0)
    jj = _lru_tile_index(j, ntt, nct, rev)
    nb = x_ref.shape[1]

    @pl.when(j == 0)
    def _():
        h_ref[...] = jnp.zeros_like(h_ref)

    lflag = jnp.logical_and(jj != 0, jj != nct).astype(F32)
    rflag = jnp.logical_and(jj != nct - 1, jj != ntt - 1).astype(F32)
    ext = jnp.concatenate([prev_ref[...] * lflag, x_ref[...], next_ref[...] * rflag], axis=0)
    xc = cb_ref[...]
    for kk in range(C_CONV):
        xc = xc + cw_ref[kk:kk + 1, :] * ext[kk:kk + TT]
    xc2 = xc.reshape(TT * nb, C_WIDTH)
    z = _dot(xc2.astype(BF16), w_ref[...]) + b_ref[...]
    r = _sigmoid(z[:, 0:C_WIDTH])
    gi = _sigmoid(z[:, C_WIDTH:2 * C_WIDTH])
    log_a = -C_POW * r * _softplus(-lam_ref[...])
    a = jnp.exp(log_a)
    bc = jnp.sqrt(1.0 - jnp.exp(2.0 * log_a)) * (gi * xc2)
    a_ref[...] = a.reshape(TT, nb, C_WIDTH)
    bc_ref[...] = bc.reshape(TT, nb, C_WIDTH)

    def body(s, h):
        t = TT - 1 - s if rev else s
        h = a_ref[t] * h + bc_ref[t]
        o_ref[t] = h
        return h

    h_ref[...] = lax.fori_loop(0, TT, body, h_ref[...], unroll=8)


def _lru(xr, conv_w, conv_b, w, b, lam, n_ctx, rev):
    t, nb, c = xr.shape
    ntt, nct = t // TT, n_ctx // TT
    idx = lambda j: _lru_tile_index(j, ntt, nct, rev)
    main = pl.BlockSpec((TT, nb, c), lambda j: (idx(j), 0, 0))
    prev = pl.BlockSpec((2, nb, c), lambda j: (jnp.maximum(idx(j) * (TT // 2) - 1, 0), 0, 0))
    nxt = pl.BlockSpec((1, nb, c), lambda j: (jnp.minimum((idx(j) + 1) * TT, t - 1), 0, 0))
    return pl.pallas_call(
        functools.partial(_lru_kernel, rev=rev, ntt=ntt, nct=nct),
        grid=(ntt,),
        in_specs=[main, prev, nxt, _full(conv_w.shape), _full(conv_b.shape), _full(w.shape), _full(b.shape),
                  _full(lam.shape)],
        out_specs=main,
        out_shape=jax.ShapeDtypeStruct(xr.shape, F32),
        scratch_shapes=[pltpu.VMEM((nb, c), F32), pltpu.VMEM((TT, nb, c), F32), pltpu.VMEM((TT, nb, c), F32)],
        compiler_params=_cparams(("arbitrary",)),
        name="lru_rev" if rev else "lru_fwd",
    )(xr, xr, xr, conv_w, conv_b, w, b, lam)


def _win_attn_kernel(q_ref, k_ref, v_ref, sink_ref, o_ref, *, n_ctx, n_lat):
    i = pl.program_id(1)
    nctx_tiles = n_ctx // TQ
    nlb = n_lat // TQ
    grp = D_HEADS // D_KV
    width = grp * TQ
    lane = lax.broadcasted_iota(jnp.int32, (TQ, 128), 1)
    low = lane < D_HD
    srow = lax.broadcasted_iota(jnp.int32, (SINK_ROWS, width), 0)
    vrow = lax.broadcasted_iota(jnp.int32, (SINK_ROWS, 128), 0)
    vlane = lax.broadcasted_iota(jnp.int32, (SINK_ROWS, 128), 1)
    v_sink = jnp.where(jnp.logical_and(vrow == 0, vlane == D_HD), 1.0, 0.0).astype(BF16)

    def stacked_q(g):
        parts = []
        for sl in range(grp // 2):
            slab = q_ref[:, (g * (grp // 2) + sl) * 128:(g * (grp // 2) + sl + 1) * 128]
            zero = jnp.zeros_like(slab)
            parts += [jnp.where(low, slab, zero), jnp.where(low, zero, slab)]
        return jnp.concatenate(parts, axis=0)

    def sink_row(g):
        return jnp.concatenate(
            [jnp.broadcast_to(sink_ref[g * grp + hh:g * grp + hh + 1, 0:1], (1, TQ)) for hh in range(grp)], axis=1)

    def finish(g, o):
        out = o[:, 0:D_HD] / o[:, D_HD:D_HD + 1]
        out = jnp.concatenate([out, jnp.zeros_like(out)], axis=1)
        for sl in range(grp // 2):
            a = out[(2 * sl) * TQ:(2 * sl + 1) * TQ]
            b = pltpu.roll(out[(2 * sl + 1) * TQ:(2 * sl + 2) * TQ], D_HD, 1)
            col = (g * (grp // 2) + sl) * 128
            o_ref[:, col:col + 128] = jnp.where(low, a, b).astype(o_ref.dtype)

    def attend(g, local):
        qz, sk = stacked_q(g), sink_row(g)
        kc = k_ref[0:n_ctx, g * 128:(g + 1) * 128]
        vc = v_ref[0:n_ctx, g * 128:(g + 1) * 128]
        sc = _dot_nt(kc, qz)
        m = jnp.maximum(jnp.max(sc, axis=0, keepdims=True), sk)
        if local is not None:
            start, mask = local
            kl = k_ref[pl.ds(start, 3 * TQ), g * 128:(g + 1) * 128]
            vl = v_ref[pl.ds(start, 3 * TQ), g * 128:(g + 1) * 128]
            sl_ = jnp.where(mask, _dot_nt(kl, qz), NEG_INF)
            m = jnp.maximum(m, jnp.max(sl_, axis=0, keepdims=True))
        e_sink = jnp.where(srow == 0, jnp.exp2(sk - m), 0.0).astype(BF16)
        o = _dot_tn(jnp.exp2(sc - m).astype(BF16), vc) + _dot_tn(e_sink, v_sink)
        if local is not None:
            o = o + _dot_tn(jnp.exp2(sl_ - m).astype(BF16), vl)
        finish(g, o)

    @pl.when(i < nctx_tiles)
    def _():
        for g in range(D_KV):
            attend(g, None)

    @pl.when(i >= nctx_tiles)
    def _():
        il = i - nctx_tiles
        kb = jnp.clip(il - 1, 0, nlb - 3)
        start = pl.multiple_of(n_ctx + kb * TQ, TQ)
        kpos = kb * TQ + lax.broadcasted_iota(jnp.int32, (3 * TQ, width), 0)
        qpos = il * TQ + (lax.broadcasted_iota(jnp.int32, (3 * TQ, width), 1) % TQ)
        mask = jnp.abs(kpos - qpos) <= WINDOW
        for g in range(D_KV):
            attend(g, (start, mask))


def _win_attn(q, k, v, sink, n_ctx):
    nb, t, _ = q.shape
    kv = pl.BlockSpec((None, t, 256), lambda b, i: (b, 0, 0))
    qo = pl.BlockSpec((None, TQ, 512), lambda b, i: (b, i, 0))
    return pl.pallas_call(
        functools.partial(_win_attn_kernel, n_ctx=n_ctx, n_lat=t - n_ctx),
        grid=(nb, t // TQ),
        in_specs=[qo, kv, kv, _full(sink.shape)],
        out_specs=qo,
        out_shape=jax.ShapeDtypeStruct((nb, t, D_HEADS * D_HD), BF16),
        compiler_params=_cparams(("parallel", "arbitrary")),
        name="win_attn",
    )(q, k, v, sink)


def _odd_out_kernel(x_ref, mod_ref, hf_ref, hr_ref, gate_ref, od_ref, w_ref, o_ref):
    d = D_MODEL
    yc = (hf_ref[...] + hr_ref[...]) * _gelu_tanh(gate_ref[...])
    y = _dot(yc.astype(BF16), w_ref[0:512, :]) + _dot(od_ref[...], w_ref[512:1024, :])
    o_ref[...] = x_ref[...] + mod_ref[:, 2 * d:3 * d] * y


def _odd_out(x, mod, hf, hr, gate, od, w):
    nb, t, d = x.shape
    tile = lambda ww: pl.BlockSpec((None, TM, ww), lambda b, j: (b, j, 0))
    tmaj = pl.BlockSpec((TM, C_WIDTH), lambda b, j: (j, b))
    return pl.pallas_call(
        _odd_out_kernel,
        grid=(nb, t // TM),
        in_specs=[tile(d), _mod_spec(nb), tmaj, tmaj, tmaj, tile(512), _full(w.shape)],
        out_specs=tile(d),
        out_shape=jax.ShapeDtypeStruct(x.shape, F32),
        compiler_params=_cparams(("parallel", "parallel")),
        name="odd_out",
    )(x, mod, hf, hr, gate, od, w)


def _final_kernel(x_ref, g_ref, o_ref):
    o_ref[...] = _rms(x_ref[...], g_ref[...])


def _final(x, g, n_ctx):
    nb, t, d = x.shape
    off = n_ctx // TM
    return pl.pallas_call(
        _final_kernel,
        grid=(nb, (t - n_ctx) // TM),
        in_specs=[pl.BlockSpec((None, TM, d), lambda b, j: (b, j + off, 0)), _full((1, d))],
        out_specs=pl.BlockSpec((None, TM, d), lambda b, j: (b, j, 0)),
        out_shape=jax.ShapeDtypeStruct((nb, t - n_ctx, d), F32),
        compiler_params=_cparams(("parallel", "parallel")),
        name="final_norm",
    )(x, g)


def _rope_tables(n_ctx, n_lat):
    pos = np.arange(n_lat)
    inv = ROPE_THETA ** (-np.arange(0, ROT_AXIS, 2, dtype=np.float64) / ROT_AXIS)
    ang_r = (pos // GRID_W)[:, None] * inv
    ang_c = (pos % GRID_W)[:, None] * inv
    lane = np.arange(128) % 64
    seg, f = lane // 16, lane % 16
    ang = np.where(seg[None, :] < 2, ang_r[:, f], ang_c[:, f])
    c = np.cos(ang)
    s = np.sin(ang)
    sa = np.where((seg % 2 == 0)[None, :], -s, 0.0)
    sb = np.where((seg % 2 == 1)[None, :], s, 0.0)
    pad = lambda a, fill: np.concatenate([np.full((n_ctx, 128), fill), a], axis=0).astype(np.float32)
    return jnp.asarray(pad(c, 1.0)), jnp.asarray(pad(sa, 0.0)), jnp.asarray(pad(sb, 0.0))


def _block_diag(w):
    eye = jnp.eye(C_BLOCKS, dtype=w.dtype)
    return jnp.einsum('hij,hg->higj', w, eye).reshape(C_WIDTH, C_WIDTH)


def kernel(x, c, ctx, c_ctx, w_ada, b_ada, norm_mix, norm_ffn, ffn_w_up, ffn_conv, ffn_w_down, final_norm,
           ev_w_in, ev_w_out, diff_lambda, diff_subln, gdn_conv, gdn_a_log, gdn_dt_bias, gdn_norm,
           od_w_in, od_w_out, lru_conv, lru_conv_b, lru_wa, lru_ba, lru_wx, lru_bx, lru_lambda, swa_sink):
    nb, n_lat, d = x.shape
    n_ctx = ctx.shape[1]
    depth = w_ada.shape[0]
    assert d == D_MODEL and n_ctx == TM and n_lat % TM == 0 and n_lat // TQ >= 3 and nb < MOD_ROWS
    t = n_ctx + n_lat

    xa = jnp.concatenate([ctx, x], axis=1)
    c_all = jnp.zeros((MOD_ROWS, d), F32).at[0:nb].set(c).at[nb].set(c_ctx)
    mod_all = _modulation(c_all, w_ada, b_ada).reshape(depth, MOD_ROWS, 1, N_MOD * d)
    tabs = _rope_tables(n_ctx, n_lat)
    row = lambda v: v.reshape(1, -1).astype(F32)

    for layer in range(depth):
        jx = layer // 2
        mod = mod_all[layer]
        mod2 = jnp.stack([jnp.broadcast_to(mod[nb], (nb, N_MOD * d)), mod[0:nb, 0]], axis=0)
        if layer % 2 == 0:
            lam_init = 0.8 - 0.6 * math.exp(-0.3 * layer)
            w_in = ev_w_in[jx]
            wa = w_in[:, 0:1536].astype(BF16)
            wg = w_in[:, 1536:3072].astype(BF16)
            wgate = w_in[:, 3072:3584].astype(BF16)
            wba = jnp.pad(w_in[:, 3584:3600], ((0, 0), (0, 112))).astype(BF16)
            pad16 = lambda v: jnp.pad(v.reshape(1, 8).astype(F32), ((0, 0), (8, 112)))
            q, k, v, qkv, gate, bg = _even_in(xa, mod, row(norm_mix[layer]), tabs, wa, wg, wgate, wba,
                                              gdn_conv[jx].astype(F32), pad16(gdn_a_log[jx]), pad16(gdn_dt_bias[jx]))
            ya = _diff_attn(q, k, v, diff_lambda[jx].astype(F32), diff_subln[jx].astype(F32).reshape(-1, 1),
                            lam_init, n_ctx)
            of = _gdn(qkv, bg, rev=False)
            orv = _gdn(qkv, bg, rev=True)
            xa = _even_out(xa, mod, ya, of, orv, gate, row(gdn_norm[jx]), ev_w_out[jx].astype(BF16))
        else:
            w_in = od_w_in[jx]
            wr = w_in[:, 0:1024].astype(BF16)
            wq = w_in[:, 1024:1536].astype(BF16)
            dup = lambda w: jnp.concatenate([w[:, 0:64], w[:, 0:64], w[:, 64:128], w[:, 64:128]], axis=1)
            zpad = lambda w: jnp.concatenate([w[:, 0:64], jnp.zeros_like(w[:, 0:64]), w[:, 64:128], jnp.zeros_like(w[:, 0:64])], axis=1)
            wkv = jnp.concatenate([dup(w_in[:, 1536:1664]), zpad(w_in[:, 1664:1792])], axis=1).astype(BF16)
            xr, gate, q, k, v = _odd_in(xa, mod, row(norm_mix[layer]), tabs, wr, wq, wkv)
            xr3 = xr.reshape(t, nb, C_WIDTH)
            hs = []
            for dd in range(2):
                wbig = jnp.concatenate([_block_diag(lru_wa[jx, dd]), _block_diag(lru_wx[jx, dd])], axis=1)
                bbig = jnp.concatenate([lru_ba[jx, dd], lru_bx[jx, dd]]).reshape(1, -1).astype(F32)
                hs.append(_lru(xr3, lru_conv[jx].astype(F32), row(lru_conv_b[jx]), wbig.astype(BF16), bbig,
                               row(lru_lambda[jx, dd]), n_ctx, rev=(dd == 1)).reshape(t, nb * C_WIDTH))
            sink = jnp.broadcast_to(swa_sink[jx].astype(F32)[:, None] * LOG2E, (D_HEADS, 128))
            od = _win_attn(q, k, v, sink, n_ctx)
            xa = _odd_out(xa, mod, hs[0], hs[1], gate, od, od_w_out[jx].astype(BF16))
        xa = _ffn(xa, mod2, row(norm_ffn[layer]), ffn_w_up[layer].astype(BF16), ffn_conv[layer].astype(F32),
                  ffn_w_down[layer].astype(BF16), n_ctx)
    return _final(xa, row(final_norm), n_ctx)
```

```python
import functools
import math

import jax
import jax.numpy as jnp
import numpy as np
from jax import lax
from jax.experimental import pallas as pl
from jax.experimental.pallas import tpu as pltpu

F32 = jnp.float32
BF16 = jnp.bfloat16
HIGHEST = lax.Precision.HIGHEST

D_MODEL = 1024
GRID_W = 64
EPS = 1e-6
NEG_INF = -1e30
N_MOD = 6
ROPE_THETA = 10000.0
ROT_AXIS = 32
A_HEADS = 4
A_HD = 64
A_VD = 128
B_HEADS = 4
B_DK = 128
B_W = 512
B_CONV = 4
B_CHUNK = 64
C_WIDTH = 512
C_BLOCKS = 8
C_BD = 64
C_CONV = 4
C_POW = 8.0
D_HEADS = 8
D_KV = 2
D_HD = 64
WINDOW = 128
FFN = 2816
FFN_CONV = 3

TM = 256
TQ = 128
TQA = 256
KB = 256
ONES_ROWS = 16
SINK_ROWS = 16
LOG2E = math.log2(math.e)
TT = 64
HALO = 8
FC = 256
TTF = 64
MOD_ROWS = 16
VMEM_LIMIT = 56 * 1024 * 1024


def _cparams(sem):
    return pltpu.CompilerParams(dimension_semantics=sem, vmem_limit_bytes=VMEM_LIMIT)


def _sigmoid(x):
    return 1.0 / (1.0 + jnp.exp(-x))


def _sigmoid_t(x):
    return 0.5 * (1.0 + jnp.tanh(0.5 * x))


def _silu(x):
    return x * _sigmoid(x)


def _softplus(x):
    return jnp.maximum(x, 0.0) + jnp.log(1.0 + jnp.exp(-jnp.abs(x)))


def _gelu_tanh(x):
    return 0.5 * x * (1.0 + jnp.tanh(math.sqrt(2.0 / math.pi) * (x + 0.044715 * (x * x * x))))


def _dot(a, b):
    return jnp.dot(a, b, preferred_element_type=F32)


def _dot_hi(a, b):
    return jnp.dot(a, b, preferred_element_type=F32, precision=HIGHEST)


def _dot3(a, b):
    ah = a.astype(BF16)
    al = (a - ah.astype(F32)).astype(BF16)
    bh = b.astype(BF16)
    bl = (b - bh.astype(F32)).astype(BF16)
    return _dot(ah, bh) + (_dot(ah, bl) + _dot(al, bh))


def _dot_nt(a, b):
    return lax.dot_general(a, b, (((1,), (1,)), ((), ())), preferred_element_type=F32)


def _dot_tn(a, b):
    return lax.dot_general(a, b, (((0,), (0,)), ((), ())), preferred_element_type=F32)


def _rms(x, g):
    return x * lax.rsqrt(jnp.mean(x * x, axis=-1, keepdims=True) + EPS) * g


def _modulate(x, g, shift, scale):
    return _rms(x, g) * (1.0 + scale) + shift


def _rope128(z, c, sa, sb):
    return z * c + pltpu.roll(z, 112, 1) * sa + pltpu.roll(z, 16, 1) * sb


def _rope(z, c, sa, sb):
    n = z.shape[1] // 128
    return jnp.concatenate([_rope128(z[:, i * 128:(i + 1) * 128], c, sa, sb) for i in range(n)], axis=1)


def _mod_kernel(s_ref, w_ref, b_ref, o_ref):
    s = _silu(s_ref[...])
    o_ref[...] = _dot_hi(s, w_ref[...]) + b_ref[...]


def _modulation(c_all, w_ada, b_ada):
    depth, d, n = w_ada.shape
    tn = 1536
    return pl.pallas_call(
        _mod_kernel,
        grid=(depth, n // tn),
        in_specs=[pl.BlockSpec((MOD_ROWS, d), lambda l, j: (0, 0)),
                  pl.BlockSpec((None, d, tn), lambda l, j: (l, 0, j)),
                  pl.BlockSpec((None, 1, tn), lambda l, j: (l, 0, j))],
        out_specs=pl.BlockSpec((None, MOD_ROWS, tn), lambda l, j: (l, 0, j)),
        out_shape=jax.ShapeDtypeStruct((depth, MOD_ROWS, n), F32),
        compiler_params=_cparams(("arbitrary", "arbitrary")),
        name="modulation",
    )(c_all, w_ada, b_ada.reshape(depth, 1, n))


def _mod_spec(nb):
    return pl.BlockSpec((None, 1, N_MOD * D_MODEL), lambda b, j: (jnp.where(j == 0, nb, b), 0, 0))


def _full(shape):
    nd = len(shape)
    return pl.BlockSpec(shape, lambda *_: (0,) * nd)


def _even_in_kernel(x_ref, prev_ref, next_ref, mod_ref, g_ref, c_ref, sa_ref, sb_ref, wa_ref, wg_ref, wgate_ref,
                    wba_ref, cw_ref, alog_ref, dtb_ref, q_ref, k_ref, v_ref, qkv_ref, gate_ref, bg_ref, ext_ref,
                    *, nt):
    d = D_MODEL
    j = pl.program_id(1)
    mod = mod_ref[...]
    g, shift, scale = g_ref[...], mod[:, 0:d], mod[:, d:2 * d]
    uf = _modulate(x_ref[...], g, shift, scale)
    u = uf.astype(BF16)
    c, sa, sb = c_ref[...], sa_ref[...], sb_ref[...]
    q = _dot(u, wa_ref[:, 0:512])
    q_ref[...] = (_rope(q, c, sa, sb) * (A_HD ** -0.5 * LOG2E)).astype(BF16)
    k = _dot(u, wa_ref[:, 512:1024])
    k_ref[...] = _rope(k, c, sa, sb).astype(BF16)
    v_ref[...] = _dot(u, wa_ref[:, 1024:1536]).astype(BF16)
    gate_ref[...] = _dot(u, wgate_ref[...]).astype(gate_ref.dtype)
    z = _dot(u, wba_ref[...])
    lane = lax.broadcasted_iota(jnp.int32, z.shape, 1)
    beta = _sigmoid(z)
    gdec = -jnp.exp(alog_ref[...]) * _softplus(z + dtb_ref[...])
    bg_ref[...] = jnp.where(lane < 2 * B_HEADS, beta, jnp.where(lane < 4 * B_HEADS, gdec, 0.0))

    lflag = (j >= 2).astype(F32)
    rflag = jnp.logical_and(j >= 1, j <= nt - 2).astype(F32)
    halo = jnp.concatenate([_modulate(prev_ref[...], g, shift, scale) * lflag,
                            _modulate(next_ref[...], g, shift, scale) * rflag], axis=0).astype(BF16)
    zh = _dot(halo, wg_ref[...])
    ext_ref[0:HALO, :] = zh[0:HALO]
    ext_ref[HALO:HALO + TM, :] = _dot(u, wg_ref[...])
    ext_ref[HALO + TM:2 * HALO + TM, :] = zh[HALO:2 * HALO]
    acc = None
    for kk in range(B_CONV):
        term = cw_ref[kk:kk + 1, :] * ext_ref[pl.ds(HALO - 2 + kk, TM), :]
        acc = term if acc is None else acc + term
    act = _silu(acc)

    def l2n(zz):
        return zz * lax.rsqrt(jnp.sum(zz * zz, axis=-1, keepdims=True) + EPS)

    for h in range(B_HEADS):
        qkv_ref[:, h * 128:(h + 1) * 128] = l2n(act[:, h * 128:(h + 1) * 128]) * (B_DK ** -0.5)
        qkv_ref[:, B_W + h * 128:B_W + (h + 1) * 128] = l2n(act[:, B_W + h * 128:B_W + (h + 1) * 128])
    qkv_ref[:, 2 * B_W:3 * B_W] = act[:, 2 * B_W:3 * B_W]


def _even_in(x, mod, g, tabs, wa, wg, wgate, wba, cw, alog, dtb):
    nb, t, d = x.shape
    nt = t // TM
    hb = TM // HALO
    tile = lambda w: pl.BlockSpec((None, TM, w), lambda b, j: (b, j, 0))
    prev = pl.BlockSpec((None, HALO, d), lambda b, j: (b, jnp.maximum(j * hb - 1, 0), 0))
    nxt = pl.BlockSpec((None, HALO, d), lambda b, j: (b, jnp.minimum((j + 1) * hb, t // HALO - 1), 0))
    tab = pl.BlockSpec((TM, 128), lambda b, j: (j, 0))
    outs = [jax.ShapeDtypeStruct((nb, t, 512), BF16)] * 3 + [
        jax.ShapeDtypeStruct((nb, t, 1536), F32), jax.ShapeDtypeStruct((nb, t, 512), BF16),
        jax.ShapeDtypeStruct((nb, t, 128), F32)]
    return pl.pallas_call(
        functools.partial(_even_in_kernel, nt=nt),
        grid=(nb, nt),
        in_specs=[tile(d), prev, nxt, _mod_spec(nb), _full((1, d)), tab, tab, tab, _full(wa.shape), _full(wg.shape),
                  _full(wgate.shape), _full(wba.shape), _full(cw.shape), _full((1, 128)), _full((1, 128))],
        out_specs=[tile(512), tile(512), tile(512), tile(1536), tile(512), tile(128)],
        out_shape=outs,
        scratch_shapes=[pltpu.VMEM((TM + 2 * HALO, 1536), F32)],
        compiler_params=_cparams(("parallel", "parallel")),
        name="even_in",
    )(x, x, x, mod, g, *tabs, wa, wg, wgate, wba, cw, alog, dtb)


def _diff_attn_kernel(q_ref, k_ref, v_ref, lv_ref, g_ref, o_ref, vt_ref, sa_ref, sb_ref, ma_ref, mb_ref,
                      *, lam_init, n_ctx):
    i = pl.program_id(2)
    t = k_ref.shape[0]
    nblk = t // KB

    @pl.when(i == 0)
    def _():
        vt_ref[0:A_VD, :] = v_ref[...].astype(F32).T.astype(BF16)
        orow = lax.broadcasted_iota(jnp.int32, (ONES_ROWS, t), 0)
        vt_ref[A_VD:A_VD + ONES_ROWS, :] = jnp.where(orow == 0, 1.0, 0.0).astype(BF16)
        sb_ref[...] = jnp.zeros_like(sb_ref)
        mb_ref[...] = jnp.zeros_like(mb_ref)

    def step(s_new, m_new, s_old, m_old):
        lv = lv_ref[...]
        lam = (jnp.exp(jnp.sum(lv[0:1] * lv[1:2], axis=-1, keepdims=True))
               - jnp.exp(jnp.sum(lv[2:3] * lv[3:4], axis=-1, keepdims=True)) + lam_init)
        m = jnp.max(m_old[...], axis=0, keepdims=True)
        is_ctx = jnp.minimum(i, pl.num_programs(2) - 2) < n_ctx // TQA
        q = q_ref[...]
        lane = lax.broadcasted_iota(jnp.int32, q.shape, 1)
        zero = jnp.zeros_like(q)
        qz = jnp.concatenate([jnp.where(lane < A_HD, q, zero), jnp.where(lane >= A_HD, q, zero)], axis=0)
        mrun = None
        oe = None
        for kb in range(nblk):
            rows = slice(kb * KB, (kb + 1) * KB)
            e = jnp.exp2(s_old[rows, :] - m).astype(BF16)
            part = _dot(vt_ref[:, rows], e)
            oe = part if oe is None else oe + part
            sblk = _dot_nt(k_ref[rows, :], qz)
            if kb >= n_ctx // KB:
                sblk = jnp.where(is_ctx, NEG_INF, sblk)
            s_new[rows, :] = sblk
            part = jnp.max(sblk.reshape(KB // 8, 8, 2 * TQA), axis=0)
            mrun = part if mrun is None else jnp.maximum(mrun, part)
        m_new[...] = mrun
        on = oe[0:A_VD] / oe[A_VD:A_VD + 1]
        od = on[:, 0:TQA] - lam * on[:, TQA:2 * TQA]
        y = od * lax.rsqrt(jnp.mean(od * od, axis=0, keepdims=True) + EPS) * (g_ref[...] * (1.0 - lam_init))
        o_ref[...] = y.T.astype(o_ref.dtype)

    @pl.when(i % 2 == 0)
    def _():
        step(sa_ref, ma_ref, sb_ref, mb_ref)

    @pl.when(i % 2 == 1)
    def _():
        step(sb_ref, mb_ref, sa_ref, ma_ref)


def _diff_attn(q, k, v, lam_vec, subln, lam_init, n_ctx):
    nb, t, _ = q.shape
    nq = t // TQA
    kv = pl.BlockSpec((None, t, 128), lambda b, h, i: (b, 0, h))
    qin = pl.BlockSpec((None, TQA, 128), lambda b, h, i: (b, jnp.minimum(i, nq - 1), h))
    out = pl.BlockSpec((None, TQA, 128), lambda b, h, i: (b, jnp.maximum(i - 1, 0), h))
    return pl.pallas_call(
        functools.partial(_diff_attn_kernel, lam_init=lam_init, n_ctx=n_ctx),
        grid=(nb, A_HEADS, nq + 1),
        in_specs=[qin, kv, kv, _full((4, A_HD)), _full((A_VD, 1))],
        out_specs=out,
        out_shape=jax.ShapeDtypeStruct((nb, t, A_HEADS * A_VD), BF16),
        scratch_shapes=[pltpu.VMEM((A_VD + ONES_ROWS, t), BF16), pltpu.VMEM((t, 2 * TQA), F32),
                        pltpu.VMEM((t, 2 * TQA), F32), pltpu.VMEM((8, 2 * TQA), F32), pltpu.VMEM((8, 2 * TQA), F32)],
        compiler_params=_cparams(("parallel", "parallel", "arbitrary")),
        name="diff_attn",
    )(q, k, v, lam_vec, subln)


def _gdn_tile_index(j, nt, rev):
    return jnp.where(j == 0, 0, nt - j) if rev else j


def _gdn_kernel(qkv_ref, bg_ref, o_ref, s_ref, *, rev, nt):
    j = pl.program_id(1)
    dirn = 1 if rev else 0
    nch = TM // B_CHUNK

    @pl.when(j == 0)
    def _():
        s_ref[...] = jnp.zeros_like(s_ref)

    bg = bg_ref[...]
    ri = lax.broadcasted_iota(jnp.int32, (TM, TM), 0)
    ci = lax.broadcasted_iota(jnp.int32, (TM, TM), 1)
    same = (ri // B_CHUNK) == (ci // B_CHUNK)
    incl = jnp.logical_and(same, (ri <= ci) if rev else (ri >= ci))
    strict = jnp.logical_and(same, (ri < ci) if rev else (ri > ci))
    eye = (ri == ci).astype(F32)
    gcum = _dot_hi(incl.astype(F32), bg)
    gcum_t = gcum.T
    rchunk = lax.broadcasted_iota(jnp.int32, (TM, B_DK), 0) // B_CHUNK

    def by_chunk(z):
        return jnp.concatenate([jnp.where(rchunk == c, z, 0.0) for c in range(nch)], axis=1).astype(BF16)

    heads = range(B_HEADS)
    lasts = [c * B_CHUNK if rev else (c + 1) * B_CHUNK - 1 for c in range(nch)]
    q, k, v, beta, gcol, eg, qkm, p, pw = ([None] * B_HEADS for _ in range(9))
    for h in heads:
        q[h] = qkv_ref[:, h * 128:(h + 1) * 128]
        k[h] = qkv_ref[:, B_W + h * 128:B_W + (h + 1) * 128]
        v[h] = qkv_ref[:, 2 * B_W + h * 128:2 * B_W + (h + 1) * 128]
        cb = dirn * B_HEADS + h
        cg = 2 * B_HEADS + cb
        beta[h] = bg[:, cb:cb + 1]
        gcol[h] = gcum[:, cg:cg + 1]
        grow = gcum_t[cg:cg + 1, :]
        eg[h] = jnp.exp(gcol[h])
        decay = jnp.where(incl, jnp.exp(jnp.where(incl, gcol[h] - grow, 0.0)), 0.0)
        kb = k[h].astype(BF16)
        qkm[h] = _dot_nt(q[h].astype(BF16), kb) * decay
        pw[h] = jnp.where(strict, beta[h] * _dot_nt(kb, kb) * decay, 0.0)

    xr = ri ^ ci
    for lvl in range(6):
        joins = (xr >> lvl) == 1
        for h in heads:
            l_s = jnp.where(joins, pw[h], 0.0)
            if lvl == 0:
                p[h] = eye - l_s
            else:
                pb = p[h].astype(BF16)
                p[h] = p[h] - _dot(pb, _dot(l_s.astype(BF16), pb).astype(BF16))

    qku, qeff, mn = ([None] * B_HEADS for _ in range(3))
    for h in heads:
        rhs = jnp.concatenate([beta[h] * v[h], (beta[h] * eg[h]) * k[h]], axis=1)
        uw = _dot(p[h].astype(BF16), rhs.astype(BF16))
        qkuw = _dot(qkm[h].astype(BF16), uw.astype(BF16))
        qku[h] = qkuw[:, 0:128]
        qeff[h] = (q[h] * eg[h] - qkuw[:, 128:256]).astype(BF16)
        glast = jnp.concatenate(
            [jnp.broadcast_to(gcol[h][r:r + 1, :], (B_CHUNK, 1)) for r in lasts], axis=0)
        kdec = (k[h] * jnp.exp(glast - gcol[h])).astype(BF16)
        mn[h] = _dot_tn(kdec, jnp.concatenate([by_chunk(uw[:, 128:256]), by_chunk(uw[:, 0:128])], axis=1))

    for step in range(nch):
        c = nch - 1 - step if rev else step
        r0, r1 = c * B_CHUNK, (c + 1) * B_CHUNK
        for h in heads:
            sh = s_ref[h]
            shb = sh.astype(BF16)
            o_ref[r0:r1, h * 128:(h + 1) * 128] = (_dot(qeff[h][r0:r1], shb) + qku[h][r0:r1]).astype(o_ref.dtype)
            mc = mn[h][:, c * 128:(c + 1) * 128].astype(BF16)
            nc = mn[h][:, (nch + c) * 128:(nch + c + 1) * 128]
            gl = jnp.exp(gcol[h][lasts[c]:lasts[c] + 1, :])
            s_ref[h] = sh * gl - _dot(mc, shb) + nc


def _gdn(qkv, bg, rev):
    nb, t, w = qkv.shape
    nt = t // TM
    main = lambda ww: pl.BlockSpec((None, TM, ww), lambda b, j: (b, _gdn_tile_index(j, nt, rev), 0))
    return pl.pallas_call(
        functools.partial(_gdn_kernel, rev=rev, nt=nt),
        grid=(nb, nt),
        in_specs=[main(w), main(128)],
        out_specs=main(B_W),
        out_shape=jax.ShapeDtypeStruct((nb, t, B_W), BF16),
        scratch_shapes=[pltpu.VMEM((B_HEADS, B_DK, B_DK), F32)],
        compiler_params=_cparams(("parallel", "arbitrary")),
        name="gdn_rev" if rev else "gdn_fwd",
    )(qkv, bg)


def _even_out_kernel(x_ref, mod_ref, ya_ref, of_ref, or_ref, gate_ref, og_ref, w_ref, o_ref):
    d = D_MODEL
    ob = of_ref[...].astype(F32) + or_ref[...].astype(F32)
    gate = gate_ref[...].astype(F32)
    yb = jnp.concatenate(
        [_rms(ob[:, h * 128:(h + 1) * 128], og_ref[...]) * _silu(gate[:, h * 128:(h + 1) * 128])
         for h in range(B_HEADS)], axis=1)
    y = _dot(ya_ref[...], w_ref[0:512, :]) + _dot(yb.astype(BF16), w_ref[512:1024, :])
    o_ref[...] = x_ref[...] + mod_ref[:, 2 * d:3 * d] * y


def _even_out(x, mod, ya, of, orv, gate, og, w):
    nb, t, d = x.shape
    tile = lambda ww: pl.BlockSpec((None, TM, ww), lambda b, j: (b, j, 0))
    return pl.pallas_call(
        _even_out_kernel,
        grid=(nb, t // TM),
        in_specs=[tile(d), _mod_spec(nb), tile(512), tile(512), tile(512), tile(512), _full((1, 128)),
                  _full(w.shape)],
        out_specs=tile(d),
        out_shape=jax.ShapeDtypeStruct(x.shape, F32),
        compiler_params=_cparams(("parallel", "parallel")),
        name="even_out",
    )(x, mod, ya, of, orv, gate, og, w)


def _ffn_kernel(x_ref, prev_ref, next_ref, mod_ref, g_ref, wup_ref, cw_ref, wdn_ref, fg_ref, o_ref, act_ref,
                *, ntt, nct, final):
    d = D_MODEL
    nb = x_ref.shape[0]
    j = pl.program_id(0)
    mod = mod_ref[...]
    shift, scale, gate = mod[:, 3 * d:4 * d], mod[:, 4 * d:5 * d], mod[:, 5 * d:6 * d]
    g = g_ref[...]
    lflag = jnp.logical_and(j != 0, j != nct).astype(F32)
    rflag = jnp.logical_and(j != nct - 1, j != ntt - 1).astype(F32)
    x3 = jnp.stack([x_ref[:, t * d:(t + 1) * d] for t in range(TTF)], axis=0)
    u3 = jnp.concatenate([(_modulate(prev_ref[...], g, shift, scale) * lflag)[None],
                          _modulate(x3, g, shift, scale),
                          (_modulate(next_ref[...], g, shift, scale) * rflag)[None]], axis=0)
    u = u3.reshape((TTF + 2) * nb, d).astype(BF16)
    rows = TTF * nb
    for c in range(FFN // FC):
        c0 = c * FC
        hg = _dot(u, wup_ref[:, c0:c0 + FC])
        hv = _dot(u, wup_ref[:, FFN + c0:FFN + c0 + FC])
        cg = None
        cv = None
        for kk in range(FFN_CONV):
            tg = cw_ref[kk:kk + 1, c0:c0 + FC] * hg[kk * nb:kk * nb + rows]
            tv = cw_ref[kk:kk + 1, FFN + c0:FFN + c0 + FC] * hv[kk * nb:kk * nb + rows]
            cg = tg if cg is None else cg + tg
            cv = tv if cv is None else cv + tv
        act_ref[:, c0:c0 + FC] = (_silu(cg) * cv).astype(BF16)
    out3 = x3 + gate * _dot(act_ref[...], wdn_ref[...]).reshape(TTF, nb, d)
    if final:
        out3 = _rms(out3, fg_ref[...])
    for t in range(TTF):
        o_ref[:, t * d:(t + 1) * d] = out3[t]


def _ffn(x, mod2, g, wup, cw, wdn, n_ctx, final_g=None):
    nb, t, d = x.shape
    ntt, nct = t // TTF, n_ctx // TTF
    x2 = x.reshape(nb, t * d)
    tile = pl.BlockSpec((nb, TTF * d), lambda j: (0, j))
    prev = pl.BlockSpec((nb, d), lambda j: (0, jnp.maximum(j * TTF - 1, 0)))
    nxt = pl.BlockSpec((nb, d), lambda j: (0, jnp.minimum((j + 1) * TTF, t - 1)))
    modspec = pl.BlockSpec((None, nb, N_MOD * d), lambda j: (jnp.where(j < nct, 0, 1), 0, 0))
    resident = lambda shape: pl.BlockSpec(shape, lambda j: (0, 0), pipeline_mode=pl.Buffered(1))
    out = pl.pallas_call(
        functools.partial(_ffn_kernel, ntt=ntt, nct=nct, final=final_g is not None),
        grid=(ntt,),
        in_specs=[tile, prev, nxt, modspec, _full((1, d)), resident(wup.shape), _full(cw.shape),
                  resident(wdn.shape), _full((1, d))],
        out_specs=tile,
        out_shape=jax.ShapeDtypeStruct(x2.shape, F32),
        scratch_shapes=[pltpu.VMEM((TTF * nb, FFN), BF16)],
        compiler_params=_cparams(("parallel",)),
        name="ffn",
    )(x2, x2, x2, mod2, g, wup, cw, wdn, g if final_g is None else final_g)
    return out.reshape(nb, t, d)


def _odd_in_kernel(x_ref, mod_ref, g_ref, c_ref, sa_ref, sb_ref, wr_ref, wq_ref, wkv_ref,
                   xr_ref, gate_ref, q_ref, k_ref, v_ref):
    d = D_MODEL
    mod = mod_ref[...]
    u = _modulate(x_ref[...], g_ref[...], mod[:, 0:d], mod[:, d:2 * d]).astype(BF16)
    c, sa, sb = c_ref[...], sa_ref[...], sb_ref[...]
    xr_ref[...] = _dot(u, wr_ref[:, 0:C_WIDTH])
    gate_ref[...] = _dot(u, wr_ref[:, C_WIDTH:2 * C_WIDTH])
    q_ref[...] = (_rope(_dot(u, wq_ref[...]), c, sa, sb) * (D_HD ** -0.5 * LOG2E)).astype(BF16)
    k_ref[...] = _rope(_dot(u, wkv_ref[:, 0:256]), c, sa, sb).astype(BF16)
    v = _dot(u, wkv_ref[:, 256:512])
    vlane = lax.broadcasted_iota(jnp.int32, v.shape, 1)
    v_ref[...] = jnp.where(vlane % 128 == D_HD, 1.0, v).astype(BF16)


def _odd_in(x, mod, g, tabs, wr, wq, wkv):
    nb, t, d = x.shape
    tile = lambda w: pl.BlockSpec((None, TM, w), lambda b, j: (b, j, 0))
    tmaj = pl.BlockSpec((TM, C_WIDTH), lambda b, j: (j, b))
    tab = pl.BlockSpec((TM, 128), lambda b, j: (j, 0))
    outs = [jax.ShapeDtypeStruct((t, nb * C_WIDTH), F32)] * 2 + [
        jax.ShapeDtypeStruct((nb, t, 512), BF16), jax.ShapeDtypeStruct((nb, t, 256), BF16),
        jax.ShapeDtypeStruct((nb, t, 256), BF16)]
    return pl.pallas_call(
        _odd_in_kernel,
        grid=(nb, t // TM),
        in_specs=[tile(d), _mod_spec(nb), _full((1, d)), tab, tab, tab, _full(wr.shape), _full(wq.shape),
                  _full(wkv.shape)],
        out_specs=[tmaj, tmaj, tile(512), tile(256), tile(256)],
        out_shape=outs,
        compiler_params=_cparams(("parallel", "parallel")),
        name="odd_in",
    )(x, mod, g, *tabs, wr, wq, wkv)


def _lru_tile_index(j, ntt, nct, rev):
    return jnp.where(j < nct, nct - 1 - j, ntt + nct - 1 - j) if rev else j


def _lru_kernel(x_ref, prev_ref, next_ref, cw_ref, cb_ref, w_ref, b_ref, lam_ref, o_ref,
                h_ref, a_ref, bc_ref, *, rev, ntt, nct):
    j = pl.program_id(0)
    jj = _lru_tile_index(j, ntt, nct, rev)
    nb = x_ref.shape[1]

    @pl.when(j == 0)
    def _():
        h_ref[...] = jnp.zeros_like(h_ref)

    lflag = jnp.logical_and(jj != 0, jj != nct).astype(F32)
    rflag = jnp.logical_and(jj != nct - 1, jj != ntt - 1).astype(F32)
    ext = jnp.concatenate([prev_ref[...] * lflag, x_ref[...], next_ref[...] * rflag], axis=0)
    xc = cb_ref[...]
    for kk in range(C_CONV):
        xc = xc + cw_ref[kk:kk + 1, :] * ext[kk:kk + TT]
    xc2 = xc.reshape(TT * nb, C_WIDTH)
    z = _dot(xc2.astype(BF16), w_ref[...]) + b_ref[...]
    r = _sigmoid_t(z[:, 0:C_WIDTH])
    gi = _sigmoid_t(z[:, C_WIDTH:2 * C_WIDTH])
    log_a = -C_POW * r * _softplus(-lam_ref[...])
    a = jnp.exp(log_a)
    bc = jnp.sqrt(1.0 - a * a) * (gi * xc2)
    a_ref[...] = a.reshape(TT, nb, C_WIDTH)
    bc_ref[...] = bc.reshape(TT, nb, C_WIDTH)

    def body(s, h):
        t = TT - 1 - s if rev else s
        h = a_ref[t] * h + bc_ref[t]
        o_ref[t] = h
        return h

    h_ref[...] = lax.fori_loop(0, TT, body, h_ref[...], unroll=8)


def _lru(xr, conv_w, conv_b, w, b, lam, n_ctx, rev):
    t, nb, c = xr.shape
    ntt, nct = t // TT, n_ctx // TT
    idx = lambda j: _lru_tile_index(j, ntt, nct, rev)
    main = pl.BlockSpec((TT, nb, c), lambda j: (idx(j), 0, 0))
    prev = pl.BlockSpec((2, nb, c), lambda j: (jnp.maximum(idx(j) * (TT // 2) - 1, 0), 0, 0))
    nxt = pl.BlockSpec((1, nb, c), lambda j: (jnp.minimum((idx(j) + 1) * TT, t - 1), 0, 0))
    return pl.pallas_call(
        functools.partial(_lru_kernel, rev=rev, ntt=ntt, nct=nct),
        grid=(ntt,),
        in_specs=[main, prev, nxt, _full(conv_w.shape), _full(conv_b.shape), _full(w.shape), _full(b.shape),
                  _full(lam.shape)],
        out_specs=main,
        out_shape=jax.ShapeDtypeStruct(xr.shape, F32),
        scratch_shapes=[pltpu.VMEM((nb, c), F32), pltpu.VMEM((TT, nb, c), F32), pltpu.VMEM((TT, nb, c), F32)],
        compiler_params=_cparams(("arbitrary",)),
        name="lru_rev" if rev else "lru_fwd",
    )(xr, xr, xr, conv_w, conv_b, w, b, lam)


def _win_attn_kernel(q_ref, k_ref, v_ref, sink_ref, o_ref, *, n_ctx, n_lat):
    i = pl.program_id(1)
    nctx_tiles = n_ctx // TQ
    nlb = n_lat // TQ
    grp = D_HEADS // D_KV
    width = grp * TQ
    lane = lax.broadcasted_iota(jnp.int32, (TQ, 128), 1)
    low = lane < D_HD
    srow = lax.broadcasted_iota(jnp.int32, (SINK_ROWS, width), 0)
    vrow = lax.broadcasted_iota(jnp.int32, (SINK_ROWS, 128), 0)
    vlane = lax.broadcasted_iota(jnp.int32, (SINK_ROWS, 128), 1)
    v_sink = jnp.where(jnp.logical_and(vrow == 0, vlane == D_HD), 1.0, 0.0).astype(BF16)

    def stacked_q(g):
        parts = []
        for sl in range(grp // 2):
            slab = q_ref[:, (g * (grp // 2) + sl) * 128:(g * (grp // 2) + sl + 1) * 128]
            zero = jnp.zeros_like(slab)
            parts += [jnp.where(low, slab, zero), jnp.where(low, zero, slab)]
        return jnp.concatenate(parts, axis=0)

    def sink_row(g):
        return jnp.concatenate(
            [jnp.broadcast_to(sink_ref[g * grp + hh:g * grp + hh + 1, 0:1], (1, TQ)) for hh in range(grp)], axis=1)

    def finish(g, o):
        out = o[:, 0:D_HD] / o[:, D_HD:D_HD + 1]
        out = jnp.concatenate([out, jnp.zeros_like(out)], axis=1)
        for sl in range(grp // 2):
            a = out[(2 * sl) * TQ:(2 * sl + 1) * TQ]
            b = pltpu.roll(out[(2 * sl + 1) * TQ:(2 * sl + 2) * TQ], D_HD, 1)
            col = (g * (grp // 2) + sl) * 128
            o_ref[:, col:col + 128] = jnp.where(low, a, b).astype(o_ref.dtype)

    def attend(g, local):
        qz, sk = stacked_q(g), sink_row(g)
        kc = k_ref[0:n_ctx, g * 128:(g + 1) * 128]
        vc = v_ref[0:n_ctx, g * 128:(g + 1) * 128]
        sc = _dot_nt(kc, qz)
        m = jnp.maximum(jnp.max(sc, axis=0, keepdims=True), sk)
        if local is not None:
            start, mask = local
            kl = k_ref[pl.ds(start, 3 * TQ), g * 128:(g + 1) * 128]
            vl = v_ref[pl.ds(start, 3 * TQ), g * 128:(g + 1) * 128]
            sl_ = jnp.where(mask, _dot_nt(kl, qz), NEG_INF)
            m = jnp.maximum(m, jnp.max(sl_, axis=0, keepdims=True))
        e_sink = jnp.where(srow == 0, jnp.exp2(sk - m), 0.0).astype(BF16)
        o = _dot_tn(jnp.exp2(sc - m).astype(BF16), vc) + _dot_tn(e_sink, v_sink)
        if local is not None:
            o = o + _dot_tn(jnp.exp2(sl_ - m).astype(BF16), vl)
        finish(g, o)

    @pl.when(i < nctx_tiles)
    def _():
        for g in range(D_KV):
            attend(g, None)

    @pl.when(i >= nctx_tiles)
    def _():
        il = i - nctx_tiles
        kb = jnp.clip(il - 1, 0, nlb - 3)
        start = pl.multiple_of(n_ctx + kb * TQ, TQ)
        kpos = kb * TQ + lax.broadcasted_iota(jnp.int32, (3 * TQ, width), 0)
        qpos = il * TQ + (lax.broadcasted_iota(jnp.int32, (3 * TQ, width), 1) % TQ)
        mask = jnp.abs(kpos - qpos) <= WINDOW
        for g in range(D_KV):
            attend(g, (start, mask))


def _win_attn(q, k, v, sink, n_ctx):
    nb, t, _ = q.shape
    kv = pl.BlockSpec((None, t, 256), lambda b, i: (b, 0, 0))
    qo = pl.BlockSpec((None, TQ, 512), lambda b, i: (b, i, 0))
    return pl.pallas_call(
        functools.partial(_win_attn_kernel, n_ctx=n_ctx, n_lat=t - n_ctx),
        grid=(nb, t // TQ),
        in_specs=[qo, kv, kv, _full(sink.shape)],
        out_specs=qo,
        out_shape=jax.ShapeDtypeStruct((nb, t, D_HEADS * D_HD), BF16),
        compiler_params=_cparams(("parallel", "arbitrary")),
        name="win_attn",
    )(q, k, v, sink)


def _odd_out_kernel(x_ref, mod_ref, hf_ref, hr_ref, gate_ref, od_ref, w_ref, o_ref):
    d = D_MODEL
    yc = (hf_ref[...] + hr_ref[...]) * _gelu_tanh(gate_ref[...])
    y = _dot(yc.astype(BF16), w_ref[0:512, :]) + _dot(od_ref[...], w_ref[512:1024, :])
    o_ref[...] = x_ref[...] + mod_ref[:, 2 * d:3 * d] * y


def _odd_out(x, mod, hf, hr, gate, od, w):
    nb, t, d = x.shape
    tile = lambda ww: pl.BlockSpec((None, TM, ww), lambda b, j: (b, j, 0))
    tmaj = pl.BlockSpec((TM, C_WIDTH), lambda b, j: (j, b))
    return pl.pallas_call(
        _odd_out_kernel,
        grid=(nb, t // TM),
        in_specs=[tile(d), _mod_spec(nb), tmaj, tmaj, tmaj, tile(512), _full(w.shape)],
        out_specs=tile(d),
        out_shape=jax.ShapeDtypeStruct(x.shape, F32),
        compiler_params=_cparams(("parallel", "parallel")),
        name="odd_out",
    )(x, mod, hf, hr, gate, od, w)


def _rope_tables(n_ctx, n_lat):
    pos = np.arange(n_lat)
    inv = ROPE_THETA ** (-np.arange(0, ROT_AXIS, 2, dtype=np.float64) / ROT_AXIS)
    ang_r = (pos // GRID_W)[:, None] * inv
    ang_c = (pos % GRID_W)[:, None] * inv
    lane = np.arange(128) % 64
    seg, f = lane // 16, lane % 16
    ang = np.where(seg[None, :] < 2, ang_r[:, f], ang_c[:, f])
    c = np.cos(ang)
    s = np.sin(ang)
    sa = np.where((seg % 2 == 0)[None, :], -s, 0.0)
    sb = np.where((seg % 2 == 1)[None, :], s, 0.0)
    pad = lambda a, fill: np.concatenate([np.full((n_ctx, 128), fill), a], axis=0).astype(np.float32)
    return jnp.asarray(pad(c, 1.0)), jnp.asarray(pad(sa, 0.0)), jnp.asarray(pad(sb, 0.0))


def _block_diag(w):
    eye = jnp.eye(C_BLOCKS, dtype=w.dtype)
    return jnp.einsum('hij,hg->higj', w, eye).reshape(C_WIDTH, C_WIDTH)


def kernel(x, c, ctx, c_ctx, w_ada, b_ada, norm_mix, norm_ffn, ffn_w_up, ffn_conv, ffn_w_down, final_norm,
           ev_w_in, ev_w_out, diff_lambda, diff_subln, gdn_conv, gdn_a_log, gdn_dt_bias, gdn_norm,
           od_w_in, od_w_out, lru_conv, lru_conv_b, lru_wa, lru_ba, lru_wx, lru_bx, lru_lambda, swa_sink):
    nb, n_lat, d = x.shape
    n_ctx = ctx.shape[1]
    depth = w_ada.shape[0]
    assert d == D_MODEL and n_ctx == TM and n_lat % TM == 0 and n_lat // TQ >= 3 and nb < MOD_ROWS
    t = n_ctx + n_lat

    xa = jnp.concatenate([ctx, x], axis=1)
    c_all = jnp.zeros((MOD_ROWS, d), F32).at[0:nb].set(c).at[nb].set(c_ctx)
    mod_all = _modulation(c_all, w_ada, b_ada).reshape(depth, MOD_ROWS, 1, N_MOD * d)
    tabs = _rope_tables(n_ctx, n_lat)
    row = lambda v: v.reshape(1, -1).astype(F32)

    for layer in range(depth):
        jx = layer // 2
        mod = mod_all[layer]
        mod2 = jnp.stack([jnp.broadcast_to(mod[nb], (nb, N_MOD * d)), mod[0:nb, 0]], axis=0)
        if layer % 2 == 0:
            lam_init = 0.8 - 0.6 * math.exp(-0.3 * layer)
            w_in = ev_w_in[jx]
            wa = w_in[:, 0:1536].astype(BF16)
            wg = w_in[:, 1536:3072].astype(BF16)
            wgate = w_in[:, 3072:3584].astype(BF16)
            wba = jnp.pad(w_in[:, 3584:3600], ((0, 0), (0, 112))).astype(BF16)
            pad16 = lambda v: jnp.pad(v.reshape(1, 8).astype(F32), ((0, 0), (8, 112)))
            q, k, v, qkv, gate, bg = _even_in(xa, mod, row(norm_mix[layer]), tabs, wa, wg, wgate, wba,
                                              gdn_conv[jx].astype(F32), pad16(gdn_a_log[jx]), pad16(gdn_dt_bias[jx]))
            ya = _diff_attn(q, k, v, diff_lambda[jx].astype(F32), diff_subln[jx].astype(F32).reshape(-1, 1),
                            lam_init, n_ctx)
            of = _gdn(qkv, bg, rev=False)
            orv = _gdn(qkv, bg, rev=True)
            xa = _even_out(xa, mod, ya, of, orv, gate, row(gdn_norm[jx]), ev_w_out[jx].astype(BF16))
        else:
            w_in = od_w_in[jx]
            wr = w_in[:, 0:1024].astype(BF16)
            wq = w_in[:, 1024:1536].astype(BF16)
            dup = lambda w: jnp.concatenate([w[:, 0:64], w[:, 0:64], w[:, 64:128], w[:, 64:128]], axis=1)
            zpad = lambda w: jnp.concatenate([w[:, 0:64], jnp.zeros_like(w[:, 0:64]), w[:, 64:128], jnp.zeros_like(w[:, 0:64])], axis=1)
            wkv = jnp.concatenate([dup(w_in[:, 1536:1664]), zpad(w_in[:, 1664:1792])], axis=1).astype(BF16)
            xr, gate, q, k, v = _odd_in(xa, mod, row(norm_mix[layer]), tabs, wr, wq, wkv)
            xr3 = xr.reshape(t, nb, C_WIDTH)
            hs = []
            for dd in range(2):
                wbig = jnp.concatenate([_block_diag(lru_wa[jx, dd]), _block_diag(lru_wx[jx, dd])], axis=1)
                bbig = jnp.concatenate([lru_ba[jx, dd], lru_bx[jx, dd]]).reshape(1, -1).astype(F32)
                hs.append(_lru(xr3, lru_conv[jx].astype(F32), row(lru_conv_b[jx]), wbig.astype(BF16), bbig,
                               row(lru_lambda[jx, dd]), n_ctx, rev=(dd == 1)).reshape(t, nb * C_WIDTH))
            sink = jnp.broadcast_to(swa_sink[jx].astype(F32)[:, None] * LOG2E, (D_HEADS, 128))
            od = _win_attn(q, k, v, sink, n_ctx)
            xa = _odd_out(xa, mod, hs[0], hs[1], gate, od, od_w_out[jx].astype(BF16))
        xa = _ffn(xa, mod2, row(norm_ffn[layer]), ffn_w_up[layer].astype(BF16), ffn_conv[layer].astype(F32),
                  ffn_w_down[layer].astype(BF16), n_ctx, row(final_norm) if layer == depth - 1 else None)
    return xa[:, n_ctx:]
```

```python
import functools
import math

import jax
import jax.numpy as jnp
import numpy as np
from jax import lax
from jax.experimental import pallas as pl
from jax.experimental.pallas import tpu as pltpu

F32 = jnp.float32
BF16 = jnp.bfloat16
HIGHEST = lax.Precision.HIGHEST

D_MODEL = 1024
GRID_W = 64
EPS = 1e-6
NEG_INF = -1e30
N_MOD = 6
ROPE_THETA = 10000.0
ROT_AXIS = 32
A_HEADS = 4
A_HD = 64
A_VD = 128
B_HEADS = 4
B_DK = 128
B_W = 512
B_CONV = 4
B_CHUNK = 64
C_WIDTH = 512
C_BLOCKS = 8
C_BD = 64
C_CONV = 4
C_POW = 8.0
D_HEADS = 8
D_KV = 2
D_HD = 64
WINDOW = 128
FFN = 2816
FFN_CONV = 3

TM = 256
TQ = 128
TQA = 256
KB = 256
ONES_ROWS = 16
SINK_ROWS = 16
LOG2E = math.log2(math.e)
TT = 64
HALO = 8
FC = 256
TTF = 64
MOD_ROWS = 16
VMEM_LIMIT = 56 * 1024 * 1024


def _cparams(sem):
    return pltpu.CompilerParams(dimension_semantics=sem, vmem_limit_bytes=VMEM_LIMIT)


def _sigmoid(x):
    return 1.0 / (1.0 + jnp.exp(-x))


def _sigmoid_t(x):
    return 0.5 * (1.0 + jnp.tanh(0.5 * x))


def _silu(x):
    return x * _sigmoid(x)


def _softplus(x):
    return jnp.maximum(x, 0.0) + jnp.log(1.0 + jnp.exp(-jnp.abs(x)))


def _gelu_tanh(x):
    return 0.5 * x * (1.0 + jnp.tanh(math.sqrt(2.0 / math.pi) * (x + 0.044715 * (x * x * x))))


def _dot(a, b):
    return jnp.dot(a, b, preferred_element_type=F32)


def _dot_hi(a, b):
    return jnp.dot(a, b, preferred_element_type=F32, precision=HIGHEST)


def _dot3(a, b):
    ah = a.astype(BF16)
    al = (a - ah.astype(F32)).astype(BF16)
    bh = b.astype(BF16)
    bl = (b - bh.astype(F32)).astype(BF16)
    return _dot(ah, bh) + (_dot(ah, bl) + _dot(al, bh))


def _dot_nt(a, b):
    return lax.dot_general(a, b, (((1,), (1,)), ((), ())), preferred_element_type=F32)


def _dot_tn(a, b):
    return lax.dot_general(a, b, (((0,), (0,)), ((), ())), preferred_element_type=F32)


def _rms(x, g):
    return x * lax.rsqrt(jnp.mean(x * x, axis=-1, keepdims=True) + EPS) * g


def _modulate(x, g, shift, scale):
    return _rms(x, g) * (1.0 + scale) + shift


def _rope128(z, c, sa, sb):
    return z * c + pltpu.roll(z, 112, 1) * sa + pltpu.roll(z, 16, 1) * sb


def _rope(z, c, sa, sb):
    n = z.shape[1] // 128
    return jnp.concatenate([_rope128(z[:, i * 128:(i + 1) * 128], c, sa, sb) for i in range(n)], axis=1)


def _mod_kernel(s_ref, w_ref, b_ref, o_ref):
    s = _silu(s_ref[...])
    o_ref[...] = _dot_hi(s, w_ref[...]) + b_ref[...]


def _modulation(c_all, w_ada, b_ada):
    depth, d, n = w_ada.shape
    tn = 1536
    return pl.pallas_call(
        _mod_kernel,
        grid=(depth, n // tn),
        in_specs=[pl.BlockSpec((MOD_ROWS, d), lambda l, j: (0, 0)),
                  pl.BlockSpec((None, d, tn), lambda l, j: (l, 0, j)),
                  pl.BlockSpec((None, 1, tn), lambda l, j: (l, 0, j))],
        out_specs=pl.BlockSpec((None, MOD_ROWS, tn), lambda l, j: (l, 0, j)),
        out_shape=jax.ShapeDtypeStruct((depth, MOD_ROWS, n), F32),
        compiler_params=_cparams(("arbitrary", "arbitrary")),
        name="modulation",
    )(c_all, w_ada, b_ada.reshape(depth, 1, n))


def _mod_spec(nb):
    return pl.BlockSpec((None, 1, N_MOD * D_MODEL), lambda b, j: (jnp.where(j == 0, nb, b), 0, 0))


def _full(shape):
    nd = len(shape)
    return pl.BlockSpec(shape, lambda *_: (0,) * nd)


def _even_in_kernel(x_ref, prev_ref, next_ref, mod_ref, g_ref, c_ref, sa_ref, sb_ref, wa_ref, wg_ref, wgate_ref,
                    wba_ref, cw_ref, alog_ref, dtb_ref, q_ref, k_ref, v_ref, qkv_ref, gate_ref, bg_ref, ext_ref,
                    *, nt):
    d = D_MODEL
    j = pl.program_id(1)
    mod = mod_ref[...]
    g, shift, scale = g_ref[...], mod[:, 0:d], mod[:, d:2 * d]
    uf = _modulate(x_ref[...], g, shift, scale)
    u = uf.astype(BF16)
    c, sa, sb = c_ref[...], sa_ref[...], sb_ref[...]
    q = _dot(u, wa_ref[:, 0:512])
    q_ref[...] = (_rope(q, c, sa, sb) * (A_HD ** -0.5 * LOG2E)).astype(BF16)
    k = _dot(u, wa_ref[:, 512:1024])
    k_ref[...] = _rope(k, c, sa, sb).astype(BF16)
    v_ref[...] = _dot(u, wa_ref[:, 1024:1536]).astype(BF16)
    gate_ref[...] = _dot(u, wgate_ref[...]).astype(gate_ref.dtype)
    z = _dot(u, wba_ref[...])
    lane = lax.broadcasted_iota(jnp.int32, z.shape, 1)
    beta = _sigmoid(z)
    gdec = -jnp.exp(alog_ref[...]) * _softplus(z + dtb_ref[...])
    bg_ref[...] = jnp.where(lane < 2 * B_HEADS, beta, jnp.where(lane < 4 * B_HEADS, gdec, 0.0))

    lflag = (j >= 2).astype(F32)
    rflag = jnp.logical_and(j >= 1, j <= nt - 2).astype(F32)
    halo = jnp.concatenate([_modulate(prev_ref[...], g, shift, scale) * lflag,
                            _modulate(next_ref[...], g, shift, scale) * rflag], axis=0).astype(BF16)
    zh = _dot(halo, wg_ref[...])
    ext_ref[0:HALO, :] = zh[0:HALO]
    ext_ref[HALO:HALO + TM, :] = _dot(u, wg_ref[...])
    ext_ref[HALO + TM:2 * HALO + TM, :] = zh[HALO:2 * HALO]
    acc = None
    for kk in range(B_CONV):
        term = cw_ref[kk:kk + 1, :] * ext_ref[pl.ds(HALO - 2 + kk, TM), :]
        acc = term if acc is None else acc + term
    act = _silu(acc)

    def l2n(zz):
        return zz * lax.rsqrt(jnp.sum(zz * zz, axis=-1, keepdims=True) + EPS)

    for h in range(B_HEADS):
        qkv_ref[:, h * 128:(h + 1) * 128] = l2n(act[:, h * 128:(h + 1) * 128]) * (B_DK ** -0.5)
        qkv_ref[:, B_W + h * 128:B_W + (h + 1) * 128] = l2n(act[:, B_W + h * 128:B_W + (h + 1) * 128])
    qkv_ref[:, 2 * B_W:3 * B_W] = act[:, 2 * B_W:3 * B_W]


def _even_in(x, mod, g, tabs, wa, wg, wgate, wba, cw, alog, dtb):
    nb, t, d = x.shape
    nt = t // TM
    hb = TM // HALO
    tile = lambda w: pl.BlockSpec((None, TM, w), lambda b, j: (b, j, 0))
    prev = pl.BlockSpec((None, HALO, d), lambda b, j: (b, jnp.maximum(j * hb - 1, 0), 0))
    nxt = pl.BlockSpec((None, HALO, d), lambda b, j: (b, jnp.minimum((j + 1) * hb, t // HALO - 1), 0))
    tab = pl.BlockSpec((TM, 128), lambda b, j: (j, 0))
    outs = [jax.ShapeDtypeStruct((nb, t, 512), BF16)] * 3 + [
        jax.ShapeDtypeStruct((nb, t, 1536), F32), jax.ShapeDtypeStruct((nb, t, 512), BF16),
        jax.ShapeDtypeStruct((nb, t, 128), F32)]
    return pl.pallas_call(
        functools.partial(_even_in_kernel, nt=nt),
        grid=(nb, nt),
        in_specs=[tile(d), prev, nxt, _mod_spec(nb), _full((1, d)), tab, tab, tab, _full(wa.shape), _full(wg.shape),
                  _full(wgate.shape), _full(wba.shape), _full(cw.shape), _full((1, 128)), _full((1, 128))],
        out_specs=[tile(512), tile(512), tile(512), tile(1536), tile(512), tile(128)],
        out_shape=outs,
        scratch_shapes=[pltpu.VMEM((TM + 2 * HALO, 1536), F32)],
        compiler_params=_cparams(("parallel", "parallel")),
        name="even_in",
    )(x, x, x, mod, g, *tabs, wa, wg, wgate, wba, cw, alog, dtb)


def _diff_attn_kernel(q_ref, k_ref, v_ref, lv_ref, g_ref, o_ref, vt_ref, sa_ref, sb_ref, ma_ref, mb_ref,
                      *, lam_init, n_ctx):
    i = pl.program_id(2)
    t = k_ref.shape[0]
    nblk = t // KB

    @pl.when(i == 0)
    def _():
        vt_ref[0:A_VD, :] = v_ref[...].astype(F32).T.astype(BF16)
        orow = lax.broadcasted_iota(jnp.int32, (ONES_ROWS, t), 0)
        vt_ref[A_VD:A_VD + ONES_ROWS, :] = jnp.where(orow == 0, 1.0, 0.0).astype(BF16)
        sb_ref[...] = jnp.zeros_like(sb_ref)
        mb_ref[...] = jnp.zeros_like(mb_ref)

    def step(s_new, m_new, s_old, m_old):
        lv = lv_ref[...]
        lam = (jnp.exp(jnp.sum(lv[0:1] * lv[1:2], axis=-1, keepdims=True))
               - jnp.exp(jnp.sum(lv[2:3] * lv[3:4], axis=-1, keepdims=True)) + lam_init)
        m = jnp.max(m_old[...], axis=0, keepdims=True)
        is_ctx = jnp.minimum(i, pl.num_programs(2) - 2) < n_ctx // TQA
        q = q_ref[...]
        lane = lax.broadcasted_iota(jnp.int32, q.shape, 1)
        zero = jnp.zeros_like(q)
        qz = jnp.concatenate([jnp.where(lane < A_HD, q, zero), jnp.where(lane >= A_HD, q, zero)], axis=0)
        mrun = None
        oe = None
        for kb in range(nblk):
            rows = slice(kb * KB, (kb + 1) * KB)
            e = jnp.exp2(s_old[rows, :] - m).astype(BF16)
            part = _dot(vt_ref[:, rows], e)
            oe = part if oe is None else oe + part
            sblk = _dot_nt(k_ref[rows, :], qz)
            if kb >= n_ctx // KB:
                sblk = jnp.where(is_ctx, NEG_INF, sblk)
            s_new[rows, :] = sblk
            part = jnp.max(sblk.reshape(KB // 8, 8, 2 * TQA), axis=0)
            mrun = part if mrun is None else jnp.maximum(mrun, part)
        m_new[...] = mrun
        on = oe[0:A_VD] / oe[A_VD:A_VD + 1]
        od = on[:, 0:TQA] - lam * on[:, TQA:2 * TQA]
        y = od * lax.rsqrt(jnp.mean(od * od, axis=0, keepdims=True) + EPS) * (g_ref[...] * (1.0 - lam_init))
        o_ref[...] = y.T.astype(o_ref.dtype)

    @pl.when(i % 2 == 0)
    def _():
        step(sa_ref, ma_ref, sb_ref, mb_ref)

    @pl.when(i % 2 == 1)
    def _():
        step(sb_ref, mb_ref, sa_ref, ma_ref)


def _diff_attn(q, k, v, lam_vec, subln, lam_init, n_ctx):
    nb, t, _ = q.shape
    nq = t // TQA
    kv = pl.BlockSpec((None, t, 128), lambda b, h, i: (b, 0, h))
    qin = pl.BlockSpec((None, TQA, 128), lambda b, h, i: (b, jnp.minimum(i, nq - 1), h))
    out = pl.BlockSpec((None, TQA, 128), lambda b, h, i: (b, jnp.maximum(i - 1, 0), h))
    return pl.pallas_call(
        functools.partial(_diff_attn_kernel, lam_init=lam_init, n_ctx=n_ctx),
        grid=(nb, A_HEADS, nq + 1),
        in_specs=[qin, kv, kv, _full((4, A_HD)), _full((A_VD, 1))],
        out_specs=out,
        out_shape=jax.ShapeDtypeStruct((nb, t, A_HEADS * A_VD), BF16),
        scratch_shapes=[pltpu.VMEM((A_VD + ONES_ROWS, t), BF16), pltpu.VMEM((t, 2 * TQA), F32),
                        pltpu.VMEM((t, 2 * TQA), F32), pltpu.VMEM((8, 2 * TQA), F32), pltpu.VMEM((8, 2 * TQA), F32)],
        compiler_params=_cparams(("parallel", "parallel", "arbitrary")),
        name="diff_attn",
    )(q, k, v, lam_vec, subln)


def _gdn_tile_index(j, nt, rev):
    return jnp.where(j == 0, 0, nt - j) if rev else j


def _gdn_kernel(qkv_ref, bg_ref, o_ref, s_ref, *, rev, nt):
    j = pl.program_id(1)
    dirn = 1 if rev else 0
    nch = TM // B_CHUNK

    @pl.when(j == 0)
    def _():
        s_ref[...] = jnp.zeros_like(s_ref)

    bg = bg_ref[...]
    ri = lax.broadcasted_iota(jnp.int32, (TM, TM), 0)
    ci = lax.broadcasted_iota(jnp.int32, (TM, TM), 1)
    same = (ri // B_CHUNK) == (ci // B_CHUNK)
    incl = jnp.logical_and(same, (ri <= ci) if rev else (ri >= ci))
    strict = jnp.logical_and(same, (ri < ci) if rev else (ri > ci))
    eye = (ri == ci).astype(F32)
    inclb = incl.astype(BF16)
    b1 = bg.astype(BF16)
    r1 = bg - b1.astype(F32)
    b2 = r1.astype(BF16)
    b3 = (r1 - b2.astype(F32)).astype(BF16)
    gcum = _dot(inclb, b1) + (_dot(inclb, b2) + _dot(inclb, b3))
    gcum_t = gcum.T
    rchunk = lax.broadcasted_iota(jnp.int32, (TM, B_DK), 0) // B_CHUNK

    def by_chunk(z):
        return jnp.concatenate([jnp.where(rchunk == c, z, 0.0) for c in range(nch)], axis=1).astype(BF16)

    heads = range(B_HEADS)
    lasts = [c * B_CHUNK if rev else (c + 1) * B_CHUNK - 1 for c in range(nch)]
    q, k, v, beta, gcol, eg, qkm, p, pw = ([None] * B_HEADS for _ in range(9))
    for h in heads:
        q[h] = qkv_ref[:, h * 128:(h + 1) * 128]
        k[h] = qkv_ref[:, B_W + h * 128:B_W + (h + 1) * 128]
        v[h] = qkv_ref[:, 2 * B_W + h * 128:2 * B_W + (h + 1) * 128]
        cb = dirn * B_HEADS + h
        cg = 2 * B_HEADS + cb
        beta[h] = bg[:, cb:cb + 1]
        gcol[h] = gcum[:, cg:cg + 1]
        grow = gcum_t[cg:cg + 1, :]
        eg[h] = jnp.exp(gcol[h])
        decay = jnp.where(incl, jnp.exp(jnp.where(incl, gcol[h] - grow, 0.0)), 0.0)
        kb = k[h].astype(BF16)
        qkm[h] = _dot_nt(q[h].astype(BF16), kb) * decay
        pw[h] = jnp.where(strict, beta[h] * _dot_nt(kb, kb) * decay, 0.0)

    xr = ri ^ ci
    for lvl in range(6):
        joins = (xr >> lvl) == 1
        for h in heads:
            l_s = jnp.where(joins, pw[h], 0.0)
            if lvl == 0:
                p[h] = eye - l_s
            else:
                pb = p[h].astype(BF16)
                p[h] = p[h] - _dot(pb, _dot(l_s.astype(BF16), pb).astype(BF16))

    qku, qeff, mn = ([None] * B_HEADS for _ in range(3))
    for h in heads:
        rhs = jnp.concatenate([beta[h] * v[h], (beta[h] * eg[h]) * k[h]], axis=1)
        uw = _dot(p[h].astype(BF16), rhs.astype(BF16))
        qkuw = _dot(qkm[h].astype(BF16), uw.astype(BF16))
        qku[h] = qkuw[:, 0:128]
        qeff[h] = (q[h] * eg[h] - qkuw[:, 128:256]).astype(BF16)
        glast = jnp.concatenate(
            [jnp.broadcast_to(gcol[h][r:r + 1, :], (B_CHUNK, 1)) for r in lasts], axis=0)
        kdec = (k[h] * jnp.exp(glast - gcol[h])).astype(BF16)
        mn[h] = _dot_tn(kdec, jnp.concatenate([by_chunk(uw[:, 128:256]), by_chunk(uw[:, 0:128])], axis=1))

    for step in range(nch):
        c = nch - 1 - step if rev else step
        r0, r1 = c * B_CHUNK, (c + 1) * B_CHUNK
        for h in heads:
            sh = s_ref[h]
            shb = sh.astype(BF16)
            o_ref[r0:r1, h * 128:(h + 1) * 128] = (_dot(qeff[h][r0:r1], shb) + qku[h][r0:r1]).astype(o_ref.dtype)
            mc = mn[h][:, c * 128:(c + 1) * 128].astype(BF16)
            nc = mn[h][:, (nch + c) * 128:(nch + c + 1) * 128]
            gl = jnp.exp(gcol[h][lasts[c]:lasts[c] + 1, :])
            s_ref[h] = sh * gl - _dot(mc, shb) + nc


def _gdn(qkv, bg, rev):
    nb, t, w = qkv.shape
    nt = t // TM
    main = lambda ww: pl.BlockSpec((None, TM, ww), lambda b, j: (b, _gdn_tile_index(j, nt, rev), 0))
    return pl.pallas_call(
        functools.partial(_gdn_kernel, rev=rev, nt=nt),
        grid=(nb, nt),
        in_specs=[main(w), main(128)],
        out_specs=main(B_W),
        out_shape=jax.ShapeDtypeStruct((nb, t, B_W), BF16),
        scratch_shapes=[pltpu.VMEM((B_HEADS, B_DK, B_DK), F32)],
        compiler_params=_cparams(("parallel", "arbitrary")),
        name="gdn_rev" if rev else "gdn_fwd",
    )(qkv, bg)


def _even_out_kernel(x_ref, mod_ref, ya_ref, of_ref, or_ref, gate_ref, og_ref, w_ref, o_ref):
    d = D_MODEL
    ob = of_ref[...].astype(F32) + or_ref[...].astype(F32)
    gate = gate_ref[...].astype(F32)
    yb = jnp.concatenate(
        [_rms(ob[:, h * 128:(h + 1) * 128], og_ref[...]) * _silu(gate[:, h * 128:(h + 1) * 128])
         for h in range(B_HEADS)], axis=1)
    y = _dot(ya_ref[...], w_ref[0:512, :]) + _dot(yb.astype(BF16), w_ref[512:1024, :])
    o_ref[...] = x_ref[...] + mod_ref[:, 2 * d:3 * d] * y


def _even_out(x, mod, ya, of, orv, gate, og, w):
    nb, t, d = x.shape
    tile = lambda ww: pl.BlockSpec((None, TM, ww), lambda b, j: (b, j, 0))
    return pl.pallas_call(
        _even_out_kernel,
        grid=(nb, t // TM),
        in_specs=[tile(d), _mod_spec(nb), tile(512), tile(512), tile(512), tile(512), _full((1, 128)),
                  _full(w.shape)],
        out_specs=tile(d),
        out_shape=jax.ShapeDtypeStruct(x.shape, F32),
        compiler_params=_cparams(("parallel", "parallel")),
        name="even_out",
    )(x, mod, ya, of, orv, gate, og, w)


def _ffn_kernel(x_ref, prev_ref, next_ref, mod_ref, g_ref, wup_ref, cw_ref, wdn_ref, fg_ref, o_ref, act_ref,
                *, ntt, nct, final):
    d = D_MODEL
    nb = x_ref.shape[0]
    j = pl.program_id(0)
    mod = mod_ref[...]
    shift, scale, gate = mod[:, 3 * d:4 * d], mod[:, 4 * d:5 * d], mod[:, 5 * d:6 * d]
    g = g_ref[...]
    lflag = jnp.logical_and(j != 0, j != nct).astype(F32)
    rflag = jnp.logical_and(j != nct - 1, j != ntt - 1).astype(F32)
    x3 = jnp.stack([x_ref[:, t * d:(t + 1) * d] for t in range(TTF)], axis=0)
    u3 = jnp.concatenate([(_modulate(prev_ref[...], g, shift, scale) * lflag)[None],
                          _modulate(x3, g, shift, scale),
                          (_modulate(next_ref[...], g, shift, scale) * rflag)[None]], axis=0)
    u = u3.reshape((TTF + 2) * nb, d).astype(BF16)
    rows = TTF * nb
    for c in range(FFN // FC):
        c0 = c * FC
        hg = _dot(u, wup_ref[:, c0:c0 + FC])
        hv = _dot(u, wup_ref[:, FFN + c0:FFN + c0 + FC])
        cg = None
        cv = None
        for kk in range(FFN_CONV):
            tg = cw_ref[kk:kk + 1, c0:c0 + FC] * hg[kk * nb:kk * nb + rows]
            tv = cw_ref[kk:kk + 1, FFN + c0:FFN + c0 + FC] * hv[kk * nb:kk * nb + rows]
            cg = tg if cg is None else cg + tg
            cv = tv if cv is None else cv + tv
        act_ref[:, c0:c0 + FC] = (_silu(cg) * cv).astype(BF16)
    out3 = x3 + gate * _dot(act_ref[...], wdn_ref[...]).reshape(TTF, nb, d)
    if final:
        out3 = _rms(out3, fg_ref[...])
    for t in range(TTF):
        o_ref[:, t * d:(t + 1) * d] = out3[t]


def _ffn(x, mod2, g, wup, cw, wdn, n_ctx, final_g=None):
    nb, t, d = x.shape
    ntt, nct = t // TTF, n_ctx // TTF
    x2 = x.reshape(nb, t * d)
    tile = pl.BlockSpec((nb, TTF * d), lambda j: (0, j))
    prev = pl.BlockSpec((nb, d), lambda j: (0, jnp.maximum(j * TTF - 1, 0)))
    nxt = pl.BlockSpec((nb, d), lambda j: (0, jnp.minimum((j + 1) * TTF, t - 1)))
    modspec = pl.BlockSpec((None, nb, N_MOD * d), lambda j: (jnp.where(j < nct, 0, 1), 0, 0))
    resident = lambda shape: pl.BlockSpec(shape, lambda j: (0, 0), pipeline_mode=pl.Buffered(1))
    final = final_g is not None
    t_out = t - n_ctx if final else t
    out_tile = pl.BlockSpec((nb, TTF * d), lambda j: (0, jnp.maximum(j - nct, 0))) if final else tile
    out = pl.pallas_call(
        functools.partial(_ffn_kernel, ntt=ntt, nct=nct, final=final),
        grid=(ntt,),
        in_specs=[tile, prev, nxt, modspec, _full((1, d)), resident(wup.shape), _full(cw.shape),
                  resident(wdn.shape), _full((1, d))],
        out_specs=out_tile,
        out_shape=jax.ShapeDtypeStruct((nb, t_out * d), F32),
        scratch_shapes=[pltpu.VMEM((TTF * nb, FFN), BF16)],
        compiler_params=_cparams(("arbitrary" if final else "parallel",)),
        name="ffn",
    )(x2, x2, x2, mod2, g, wup, cw, wdn, final_g if final else g)
    return out.reshape(nb, t_out, d)


def _odd_in_kernel(x_ref, mod_ref, g_ref, c_ref, sa_ref, sb_ref, wr_ref, wq_ref, wkv_ref,
                   xr_ref, gate_ref, q_ref, k_ref, v_ref):
    d = D_MODEL
    mod = mod_ref[...]
    u = _modulate(x_ref[...], g_ref[...], mod[:, 0:d], mod[:, d:2 * d]).astype(BF16)
    c, sa, sb = c_ref[...], sa_ref[...], sb_ref[...]
    xr_ref[...] = _dot(u, wr_ref[:, 0:C_WIDTH])
    gate_ref[...] = _dot(u, wr_ref[:, C_WIDTH:2 * C_WIDTH])
    q_ref[...] = (_rope(_dot(u, wq_ref[...]), c, sa, sb) * (D_HD ** -0.5 * LOG2E)).astype(BF16)
    k_ref[...] = _rope(_dot(u, wkv_ref[:, 0:256]), c, sa, sb).astype(BF16)
    v = _dot(u, wkv_ref[:, 256:512])
    vlane = lax.broadcasted_iota(jnp.int32, v.shape, 1)
    v_ref[...] = jnp.where(vlane % 128 == D_HD, 1.0, v).astype(BF16)


def _odd_in(x, mod, g, tabs, wr, wq, wkv):
    nb, t, d = x.shape
    tile = lambda w: pl.BlockSpec((None, TM, w), lambda b, j: (b, j, 0))
    tmaj = pl.BlockSpec((TM, C_WIDTH), lambda b, j: (j, b))
    tab = pl.BlockSpec((TM, 128), lambda b, j: (j, 0))
    outs = [jax.ShapeDtypeStruct((t, nb * C_WIDTH), F32)] * 2 + [
        jax.ShapeDtypeStruct((nb, t, 512), BF16), jax.ShapeDtypeStruct((nb, t, 256), BF16),
        jax.ShapeDtypeStruct((nb, t, 256), BF16)]
    return pl.pallas_call(
        _odd_in_kernel,
        grid=(nb, t // TM),
        in_specs=[tile(d), _mod_spec(nb), _full((1, d)), tab, tab, tab, _full(wr.shape), _full(wq.shape),
                  _full(wkv.shape)],
        out_specs=[tmaj, tmaj, tile(512), tile(256), tile(256)],
        out_shape=outs,
        compiler_params=_cparams(("parallel", "parallel")),
        name="odd_in",
    )(x, mod, g, *tabs, wr, wq, wkv)


def _lru_tile_index(j, ntt, nct, rev):
    return jnp.where(j < nct, nct - 1 - j, ntt + nct - 1 - j) if rev else j


def _lru_kernel(x_ref, prev_ref, next_ref, cw_ref, cb_ref, w_ref, b_ref, lam_ref, o_ref,
                h_ref, a_ref, bc_ref, *, rev, ntt, nct):
    j = pl.program_id(0)
    jj = _lru_tile_index(j, ntt, nct, rev)
    nb = x_ref.shape[1]

    @pl.when(j == 0)
    def _():
        h_ref[...] = jnp.zeros_like(h_ref)

    lflag = jnp.logical_and(jj != 0, jj != nct).astype(F32)
    rflag = jnp.logical_and(jj != nct - 1, jj != ntt - 1).astype(F32)
    ext = jnp.concatenate([prev_ref[...] * lflag, x_ref[...], next_ref[...] * rflag], axis=0)
    xc = cb_ref[...]
    for kk in range(C_CONV):
        xc = xc + cw_ref[kk:kk + 1, :] * ext[kk:kk + TT]
    xc2 = xc.reshape(TT * nb, C_WIDTH)
    z = _dot(xc2.astype(BF16), w_ref[...]) + b_ref[...]
    r = _sigmoid_t(z[:, 0:C_WIDTH])
    gi = _sigmoid_t(z[:, C_WIDTH:2 * C_WIDTH])
    log_a = -C_POW * r * _softplus(-lam_ref[...])
    a = jnp.exp(log_a)
    bc = jnp.sqrt(1.0 - a * a) * (gi * xc2)
    a_ref[...] = a.reshape(TT, nb, C_WIDTH)
    bc_ref[...] = bc.reshape(TT, nb, C_WIDTH)

    def body(s, h):
        t = TT - 1 - s if rev else s
        h = a_ref[t] * h + bc_ref[t]
        o_ref[t] = h
        return h

    h_ref[...] = lax.fori_loop(0, TT, body, h_ref[...], unroll=8)


def _lru(xr, conv_w, conv_b, w, b, lam, n_ctx, rev):
    t, nb, c = xr.shape
    ntt, nct = t // TT, n_ctx // TT
    idx = lambda j: _lru_tile_index(j, ntt, nct, rev)
    main = pl.BlockSpec((TT, nb, c), lambda j: (idx(j), 0, 0))
    prev = pl.BlockSpec((2, nb, c), lambda j: (jnp.maximum(idx(j) * (TT // 2) - 1, 0), 0, 0))
    nxt = pl.BlockSpec((1, nb, c), lambda j: (jnp.minimum((idx(j) + 1) * TT, t - 1), 0, 0))
    return pl.pallas_call(
        functools.partial(_lru_kernel, rev=rev, ntt=ntt, nct=nct),
        grid=(ntt,),
        in_specs=[main, prev, nxt, _full(conv_w.shape), _full(conv_b.shape), _full(w.shape), _full(b.shape),
                  _full(lam.shape)],
        out_specs=main,
        out_shape=jax.ShapeDtypeStruct(xr.shape, F32),
        scratch_shapes=[pltpu.VMEM((nb, c), F32), pltpu.VMEM((TT, nb, c), F32), pltpu.VMEM((TT, nb, c), F32)],
        compiler_params=_cparams(("arbitrary",)),
        name="lru_rev" if rev else "lru_fwd",
    )(xr, xr, xr, conv_w, conv_b, w, b, lam)


def _win_attn_kernel(q_ref, k_ref, v_ref, sink_ref, o_ref, *, n_ctx, n_lat):
    i = pl.program_id(1)
    nctx_tiles = n_ctx // TQ
    nlb = n_lat // TQ
    grp = D_HEADS // D_KV
    width = grp * TQ
    lane = lax.broadcasted_iota(jnp.int32, (TQ, 128), 1)
    low = lane < D_HD
    srow = lax.broadcasted_iota(jnp.int32, (SINK_ROWS, width), 0)
    vrow = lax.broadcasted_iota(jnp.int32, (SINK_ROWS, 128), 0)
    vlane = lax.broadcasted_iota(jnp.int32, (SINK_ROWS, 128), 1)
    v_sink = jnp.where(jnp.logical_and(vrow == 0, vlane == D_HD), 1.0, 0.0).astype(BF16)

    def stacked_q(g):
        parts = []
        for sl in range(grp // 2):
            slab = q_ref[:, (g * (grp // 2) + sl) * 128:(g * (grp // 2) + sl + 1) * 128]
            zero = jnp.zeros_like(slab)
            parts += [jnp.where(low, slab, zero), jnp.where(low, zero, slab)]
        return jnp.concatenate(parts, axis=0)

    def sink_row(g):
        return jnp.concatenate(
            [jnp.broadcast_to(sink_ref[g * grp + hh:g * grp + hh + 1, 0:1], (1, TQ)) for hh in range(grp)], axis=1)

    def finish(g, o):
        out = o[:, 0:D_HD] / o[:, D_HD:D_HD + 1]
        out = jnp.concatenate([out, jnp.zeros_like(out)], axis=1)
        for sl in range(grp // 2):
            a = out[(2 * sl) * TQ:(2 * sl + 1) * TQ]
            b = pltpu.roll(out[(2 * sl + 1) * TQ:(2 * sl + 2) * TQ], D_HD, 1)
            col = (g * (grp // 2) + sl) * 128
            o_ref[:, col:col + 128] = jnp.where(low, a, b).astype(o_ref.dtype)

    def attend(local):
        groups = range(D_KV)
        qz = [stacked_q(g) for g in groups]
        sk = [sink_row(g) for g in groups]
        sc, sl_, m = [None] * D_KV, [None] * D_KV, [None] * D_KV
        for g in groups:
            sc[g] = _dot_nt(k_ref[0:n_ctx, g * 128:(g + 1) * 128], qz[g])
            m[g] = jnp.maximum(jnp.max(sc[g], axis=0, keepdims=True), sk[g])
            if local is not None:
                start, mask = local
                kl = k_ref[pl.ds(start, 3 * TQ), g * 128:(g + 1) * 128]
                sl_[g] = jnp.where(mask, _dot_nt(kl, qz[g]), NEG_INF)
                m[g] = jnp.maximum(m[g], jnp.max(sl_[g], axis=0, keepdims=True))
        for g in groups:
            e_sink = jnp.where(srow == 0, jnp.exp2(sk[g] - m[g]), 0.0).astype(BF16)
            o = (_dot_tn(jnp.exp2(sc[g] - m[g]).astype(BF16), v_ref[0:n_ctx, g * 128:(g + 1) * 128])
                 + _dot_tn(e_sink, v_sink))
            if local is not None:
                vl = v_ref[pl.ds(local[0], 3 * TQ), g * 128:(g + 1) * 128]
                o = o + _dot_tn(jnp.exp2(sl_[g] - m[g]).astype(BF16), vl)
            finish(g, o)

    @pl.when(i < nctx_tiles)
    def _():
        attend(None)

    @pl.when(i >= nctx_tiles)
    def _():
        il = i - nctx_tiles
        kb = jnp.clip(il - 1, 0, nlb - 3)
        start = pl.multiple_of(n_ctx + kb * TQ, TQ)
        kpos = kb * TQ + lax.broadcasted_iota(jnp.int32, (3 * TQ, width), 0)
        qpos = il * TQ + (lax.broadcasted_iota(jnp.int32, (3 * TQ, width), 1) % TQ)
        mask = jnp.abs(kpos - qpos) <= WINDOW
        attend((start, mask))


def _win_attn(q, k, v, sink, n_ctx):
    nb, t, _ = q.shape
    kv = pl.BlockSpec((None, t, 256), lambda b, i: (b, 0, 0))
    qo = pl.BlockSpec((None, TQ, 512), lambda b, i: (b, i, 0))
    return pl.pallas_call(
        functools.partial(_win_attn_kernel, n_ctx=n_ctx, n_lat=t - n_ctx),
        grid=(nb, t // TQ),
        in_specs=[qo, kv, kv, _full(sink.shape)],
        out_specs=qo,
        out_shape=jax.ShapeDtypeStruct((nb, t, D_HEADS * D_HD), BF16),
        compiler_params=_cparams(("parallel", "arbitrary")),
        name="win_attn",
    )(q, k, v, sink)


def _odd_out_kernel(x_ref, mod_ref, hf_ref, hr_ref, gate_ref, od_ref, w_ref, o_ref):
    d = D_MODEL
    yc = (hf_ref[...] + hr_ref[...]) * _gelu_tanh(gate_ref[...])
    y = _dot(yc.astype(BF16), w_ref[0:512, :]) + _dot(od_ref[...], w_ref[512:1024, :])
    o_ref[...] = x_ref[...] + mod_ref[:, 2 * d:3 * d] * y


def _odd_out(x, mod, hf, hr, gate, od, w):
    nb, t, d = x.shape
    tile = lambda ww: pl.BlockSpec((None, TM, ww), lambda b, j: (b, j, 0))
    tmaj = pl.BlockSpec((TM, C_WIDTH), lambda b, j: (j, b))
    return pl.pallas_call(
        _odd_out_kernel,
        grid=(nb, t // TM),
        in_specs=[tile(d), _mod_spec(nb), tmaj, tmaj, tmaj, tile(512), _full(w.shape)],
        out_specs=tile(d),
        out_shape=jax.ShapeDtypeStruct(x.shape, F32),
        compiler_params=_cparams(("parallel", "parallel")),
        name="odd_out",
    )(x, mod, hf, hr, gate, od, w)


def _rope_tables(n_ctx, n_lat):
    pos = np.arange(n_lat)
    inv = ROPE_THETA ** (-np.arange(0, ROT_AXIS, 2, dtype=np.float64) / ROT_AXIS)
    ang_r = (pos // GRID_W)[:, None] * inv
    ang_c = (pos % GRID_W)[:, None] * inv
    lane = np.arange(128) % 64
    seg, f = lane // 16, lane % 16
    ang = np.where(seg[None, :] < 2, ang_r[:, f], ang_c[:, f])
    c = np.cos(ang)
    s = np.sin(ang)
    sa = np.where((seg % 2 == 0)[None, :], -s, 0.0)
    sb = np.where((seg % 2 == 1)[None, :], s, 0.0)
    pad = lambda a, fill: np.concatenate([np.full((n_ctx, 128), fill), a], axis=0).astype(np.float32)
    return jnp.asarray(pad(c, 1.0)), jnp.asarray(pad(sa, 0.0)), jnp.asarray(pad(sb, 0.0))


def _block_diag(w):
    eye = jnp.eye(C_BLOCKS, dtype=w.dtype)
    return jnp.einsum('hij,hg->higj', w, eye).reshape(C_WIDTH, C_WIDTH)


def kernel(x, c, ctx, c_ctx, w_ada, b_ada, norm_mix, norm_ffn, ffn_w_up, ffn_conv, ffn_w_down, final_norm,
           ev_w_in, ev_w_out, diff_lambda, diff_subln, gdn_conv, gdn_a_log, gdn_dt_bias, gdn_norm,
           od_w_in, od_w_out, lru_conv, lru_conv_b, lru_wa, lru_ba, lru_wx, lru_bx, lru_lambda, swa_sink):
    nb, n_lat, d = x.shape
    n_ctx = ctx.shape[1]
    depth = w_ada.shape[0]
    assert d == D_MODEL and n_ctx == TM and n_lat % TM == 0 and n_lat // TQ >= 3 and nb < MOD_ROWS
    t = n_ctx + n_lat

    xa = jnp.concatenate([ctx, x], axis=1)
    c_all = jnp.zeros((MOD_ROWS, d), F32).at[0:nb].set(c).at[nb].set(c_ctx)
    mod_all = _modulation(c_all, w_ada, b_ada).reshape(depth, MOD_ROWS, 1, N_MOD * d)
    tabs = _rope_tables(n_ctx, n_lat)
    row = lambda v: v.reshape(1, -1).astype(F32)

    for layer in range(depth):
        jx = layer // 2
        mod = mod_all[layer]
        mod2 = jnp.stack([jnp.broadcast_to(mod[nb], (nb, N_MOD * d)), mod[0:nb, 0]], axis=0)
        if layer % 2 == 0:
            lam_init = 0.8 - 0.6 * math.exp(-0.3 * layer)
            w_in = ev_w_in[jx]
            wa = w_in[:, 0:1536].astype(BF16)
            wg = w_in[:, 1536:3072].astype(BF16)
            wgate = w_in[:, 3072:3584].astype(BF16)
            wba = jnp.pad(w_in[:, 3584:3600], ((0, 0), (0, 112))).astype(BF16)
            pad16 = lambda v: jnp.pad(v.reshape(1, 8).astype(F32), ((0, 0), (8, 112)))
            q, k, v, qkv, gate, bg = _even_in(xa, mod, row(norm_mix[layer]), tabs, wa, wg, wgate, wba,
                                              gdn_conv[jx].astype(F32), pad16(gdn_a_log[jx]), pad16(gdn_dt_bias[jx]))
            ya = _diff_attn(q, k, v, diff_lambda[jx].astype(F32), diff_subln[jx].astype(F32).reshape(-1, 1),
                            lam_init, n_ctx)
            of = _gdn(qkv, bg, rev=False)
            orv = _gdn(qkv, bg, rev=True)
            xa = _even_out(xa, mod, ya, of, orv, gate, row(gdn_norm[jx]), ev_w_out[jx].astype(BF16))
        else:
            w_in = od_w_in[jx]
            wr = w_in[:, 0:1024].astype(BF16)
            wq = w_in[:, 1024:1536].astype(BF16)
            dup = lambda w: jnp.concatenate([w[:, 0:64], w[:, 0:64], w[:, 64:128], w[:, 64:128]], axis=1)
            zpad = lambda w: jnp.concatenate([w[:, 0:64], jnp.zeros_like(w[:, 0:64]), w[:, 64:128], jnp.zeros_like(w[:, 0:64])], axis=1)
            wkv = jnp.concatenate([dup(w_in[:, 1536:1664]), zpad(w_in[:, 1664:1792])], axis=1).astype(BF16)
            xr, gate, q, k, v = _odd_in(xa, mod, row(norm_mix[layer]), tabs, wr, wq, wkv)
            xr3 = xr.reshape(t, nb, C_WIDTH)
            hs = []
            for dd in range(2):
                wbig = jnp.concatenate([_block_diag(lru_wa[jx, dd]), _block_diag(lru_wx[jx, dd])], axis=1)
                bbig = jnp.concatenate([lru_ba[jx, dd], lru_bx[jx, dd]]).reshape(1, -1).astype(F32)
                hs.append(_lru(xr3, lru_conv[jx].astype(F32), row(lru_conv_b[jx]), wbig.astype(BF16), bbig,
                               row(lru_lambda[jx, dd]), n_ctx, rev=(dd == 1)).reshape(t, nb * C_WIDTH))
            sink = jnp.broadcast_to(swa_sink[jx].astype(F32)[:, None] * LOG2E, (D_HEADS, 128))
            od = _win_attn(q, k, v, sink, n_ctx)
            xa = _odd_out(xa, mod, hs[0], hs[1], gate, od, od_w_out[jx].astype(BF16))
        xa = _ffn(xa, mod2, row(norm_ffn[layer]), ffn_w_up[layer].astype(BF16), ffn_conv[layer].astype(F32),
                  ffn_w_down[layer].astype(BF16), n_ctx, row(final_norm) if layer == depth - 1 else None)
    return xa
```

```python
import functools
import math

import jax
import jax.numpy as jnp
import numpy as np
from jax import lax
from jax.experimental import pallas as pl
from jax.experimental.pallas import tpu as pltpu

F32 = jnp.float32
BF16 = jnp.bfloat16
HIGHEST = lax.Precision.HIGHEST

D_MODEL = 1024
GRID_W = 64
EPS = 1e-6
NEG_INF = -1e30
N_MOD = 6
ROPE_THETA = 10000.0
ROT_AXIS = 32
A_HEADS = 4
A_HD = 64
A_VD = 128
B_HEADS = 4
B_DK = 128
B_W = 512
B_CONV = 4
B_CHUNK = 64
C_WIDTH = 512
C_BLOCKS = 8
C_BD = 64
C_CONV = 4
C_POW = 8.0
D_HEADS = 8
D_KV = 2
D_HD = 64
WINDOW = 128
FFN = 2816
FFN_CONV = 3

TM = 256
TQ = 128
TQA = 256
KB = 256
ONES_ROWS = 16
SINK_ROWS = 16
LOG2E = math.log2(math.e)
TT = 64
HALO = 8
FC = 256
TTF = 64
MOD_ROWS = 16
VMEM_LIMIT = 56 * 1024 * 1024


def _cparams(sem):
    return pltpu.CompilerParams(dimension_semantics=sem, vmem_limit_bytes=VMEM_LIMIT)


def _sigmoid(x):
    return 1.0 / (1.0 + jnp.exp(-x))


def _sigmoid_t(x):
    return 0.5 * (1.0 + jnp.tanh(0.5 * x))


def _silu(x):
    return x * _sigmoid(x)


def _softplus(x):
    return jnp.maximum(x, 0.0) + jnp.log(1.0 + jnp.exp(-jnp.abs(x)))


def _gelu_tanh(x):
    return 0.5 * x * (1.0 + jnp.tanh(math.sqrt(2.0 / math.pi) * (x + 0.044715 * (x * x * x))))


def _dot(a, b):
    return jnp.dot(a, b, preferred_element_type=F32)


def _dot_hi(a, b):
    return jnp.dot(a, b, preferred_element_type=F32, precision=HIGHEST)


def _dot3(a, b):
    ah = a.astype(BF16)
    al = (a - ah.astype(F32)).astype(BF16)
    bh = b.astype(BF16)
    bl = (b - bh.astype(F32)).astype(BF16)
    return _dot(ah, bh) + (_dot(ah, bl) + _dot(al, bh))


def _dot_nt(a, b):
    return lax.dot_general(a, b, (((1,), (1,)), ((), ())), preferred_element_type=F32)


def _dot_tn(a, b):
    return lax.dot_general(a, b, (((0,), (0,)), ((), ())), preferred_element_type=F32)


def _rms(x, g):
    return x * lax.rsqrt(jnp.mean(x * x, axis=-1, keepdims=True) + EPS) * g


def _modulate(x, g, shift, scale):
    return _rms(x, g) * (1.0 + scale) + shift


def _rope128(z, c, sa, sb):
    return z * c + pltpu.roll(z, 112, 1) * sa + pltpu.roll(z, 16, 1) * sb


def _rope(z, c, sa, sb):
    n = z.shape[1] // 128
    return jnp.concatenate([_rope128(z[:, i * 128:(i + 1) * 128], c, sa, sb) for i in range(n)], axis=1)


def _mod_kernel(s_ref, w_ref, b_ref, o_ref):
    s = _silu(s_ref[...])
    o_ref[...] = _dot_hi(s, w_ref[...]) + b_ref[...]


def _modulation(c_all, w_ada, b_ada):
    depth, d, n = w_ada.shape
    tn = 1536
    return pl.pallas_call(
        _mod_kernel,
        grid=(depth, n // tn),
        in_specs=[pl.BlockSpec((MOD_ROWS, d), lambda l, j: (0, 0)),
                  pl.BlockSpec((None, d, tn), lambda l, j: (l, 0, j)),
                  pl.BlockSpec((None, 1, tn), lambda l, j: (l, 0, j))],
        out_specs=pl.BlockSpec((None, MOD_ROWS, tn), lambda l, j: (l, 0, j)),
        out_shape=jax.ShapeDtypeStruct((depth, MOD_ROWS, n), F32),
        compiler_params=_cparams(("arbitrary", "arbitrary")),
        name="modulation",
    )(c_all, w_ada, b_ada.reshape(depth, 1, n))


def _mod_spec(nb):
    return pl.BlockSpec((None, 1, N_MOD * D_MODEL), lambda b, j: (jnp.where(j == 0, nb, b), 0, 0))


def _full(shape):
    nd = len(shape)
    return pl.BlockSpec(shape, lambda *_: (0,) * nd)


def _even_in_kernel(x_ref, prev_ref, next_ref, mod_ref, g_ref, c_ref, sa_ref, sb_ref, wa_ref, wg_ref, wgate_ref,
                    wba_ref, cw_ref, alog_ref, dtb_ref, q_ref, k_ref, v_ref, qkv_ref, gate_ref, bg_ref, ext_ref,
                    *, nt):
    d = D_MODEL
    j = pl.program_id(1)
    mod = mod_ref[...]
    g, shift, scale = g_ref[...], mod[:, 0:d], mod[:, d:2 * d]
    uf = _modulate(x_ref[...], g, shift, scale)
    u = uf.astype(BF16)
    c, sa, sb = c_ref[...], sa_ref[...], sb_ref[...]
    q = _dot(u, wa_ref[:, 0:512])
    q_ref[...] = (_rope(q, c, sa, sb) * (A_HD ** -0.5 * LOG2E)).astype(BF16)
    k = _dot(u, wa_ref[:, 512:1024])
    k_ref[...] = _rope(k, c, sa, sb).astype(BF16)
    v_ref[...] = _dot(u, wa_ref[:, 1024:1536]).astype(BF16)
    gate_ref[...] = _dot(u, wgate_ref[...]).astype(gate_ref.dtype)
    z = _dot(u, wba_ref[...])
    lane = lax.broadcasted_iota(jnp.int32, z.shape, 1)
    beta = _sigmoid(z)
    gdec = -jnp.exp(alog_ref[...]) * _softplus(z + dtb_ref[...])
    bg_ref[...] = jnp.where(lane < 2 * B_HEADS, beta, jnp.where(lane < 4 * B_HEADS, gdec, 0.0))

    lflag = (j >= 2).astype(F32)
    rflag = jnp.logical_and(j >= 1, j <= nt - 2).astype(F32)
    halo = jnp.concatenate([_modulate(prev_ref[...], g, shift, scale) * lflag,
                            _modulate(next_ref[...], g, shift, scale) * rflag], axis=0).astype(BF16)
    zh = _dot(halo, wg_ref[...])
    ext_ref[0:HALO, :] = zh[0:HALO]
    ext_ref[HALO:HALO + TM, :] = _dot(u, wg_ref[...])
    ext_ref[HALO + TM:2 * HALO + TM, :] = zh[HALO:2 * HALO]
    acc = None
    for kk in range(B_CONV):
        term = cw_ref[kk:kk + 1, :] * ext_ref[pl.ds(HALO - 2 + kk, TM), :]
        acc = term if acc is None else acc + term
    act = _silu(acc)

    def l2n(zz):
        return zz * lax.rsqrt(jnp.sum(zz * zz, axis=-1, keepdims=True) + EPS)

    for h in range(B_HEADS):
        qkv_ref[:, h * 128:(h + 1) * 128] = l2n(act[:, h * 128:(h + 1) * 128]) * (B_DK ** -0.5)
        qkv_ref[:, B_W + h * 128:B_W + (h + 1) * 128] = l2n(act[:, B_W + h * 128:B_W + (h + 1) * 128])
    qkv_ref[:, 2 * B_W:3 * B_W] = act[:, 2 * B_W:3 * B_W]


def _even_in(x, mod, g, tabs, wa, wg, wgate, wba, cw, alog, dtb):
    nb, t, d = x.shape
    nt = t // TM
    hb = TM // HALO
    tile = lambda w: pl.BlockSpec((None, TM, w), lambda b, j: (b, j, 0))
    prev = pl.BlockSpec((None, HALO, d), lambda b, j: (b, jnp.maximum(j * hb - 1, 0), 0))
    nxt = pl.BlockSpec((None, HALO, d), lambda b, j: (b, jnp.minimum((j + 1) * hb, t // HALO - 1), 0))
    tab = pl.BlockSpec((TM, 128), lambda b, j: (j, 0))
    outs = [jax.ShapeDtypeStruct((nb, t, 512), BF16)] * 3 + [
        jax.ShapeDtypeStruct((nb, t, 1536), F32), jax.ShapeDtypeStruct((nb, t, 512), BF16),
        jax.ShapeDtypeStruct((nb, t, 128), F32)]
    return pl.pallas_call(
        functools.partial(_even_in_kernel, nt=nt),
        grid=(nb, nt),
        in_specs=[tile(d), prev, nxt, _mod_spec(nb), _full((1, d)), tab, tab, tab, _full(wa.shape), _full(wg.shape),
                  _full(wgate.shape), _full(wba.shape), _full(cw.shape), _full((1, 128)), _full((1, 128))],
        out_specs=[tile(512), tile(512), tile(512), tile(1536), tile(512), tile(128)],
        out_shape=outs,
        scratch_shapes=[pltpu.VMEM((TM + 2 * HALO, 1536), F32)],
        compiler_params=_cparams(("parallel", "parallel")),
        name="even_in",
    )(x, x, x, mod, g, *tabs, wa, wg, wgate, wba, cw, alog, dtb)


def _diff_attn_kernel(q_ref, k_ref, v_ref, lv_ref, g_ref, o_ref, vt_ref, sa_ref, sb_ref, ma_ref, mb_ref,
                      *, lam_init, n_ctx):
    i = pl.program_id(2)
    t = k_ref.shape[0]
    nblk = t // KB

    @pl.when(i == 0)
    def _():
        vt_ref[0:A_VD, :] = v_ref[...].astype(F32).T.astype(BF16)
        orow = lax.broadcasted_iota(jnp.int32, (ONES_ROWS, t), 0)
        vt_ref[A_VD:A_VD + ONES_ROWS, :] = jnp.where(orow == 0, 1.0, 0.0).astype(BF16)
        sb_ref[...] = jnp.zeros_like(sb_ref)
        mb_ref[...] = jnp.zeros_like(mb_ref)

    def step(s_new, m_new, s_old, m_old):
        lv = lv_ref[...]
        lam = (jnp.exp(jnp.sum(lv[0:1] * lv[1:2], axis=-1, keepdims=True))
               - jnp.exp(jnp.sum(lv[2:3] * lv[3:4], axis=-1, keepdims=True)) + lam_init)
        m = jnp.max(m_old[...], axis=0, keepdims=True)
        is_ctx = jnp.minimum(i, pl.num_programs(2) - 2) < n_ctx // TQA
        q = q_ref[...]
        lane = lax.broadcasted_iota(jnp.int32, q.shape, 1)
        zero = jnp.zeros_like(q)
        qz = jnp.concatenate([jnp.where(lane < A_HD, q, zero), jnp.where(lane >= A_HD, q, zero)], axis=0)
        mrun = None
        oe = None
        for kb in range(nblk):
            rows = slice(kb * KB, (kb + 1) * KB)
            e = jnp.exp2(s_old[rows, :] - m).astype(BF16)
            part = _dot(vt_ref[:, rows], e)
            oe = part if oe is None else oe + part
            sblk = _dot_nt(k_ref[rows, :], qz)
            if kb >= n_ctx // KB:
                sblk = jnp.where(is_ctx, NEG_INF, sblk)
            s_new[rows, :] = sblk
            part = jnp.max(sblk.reshape(KB // 8, 8, 2 * TQA), axis=0)
            mrun = part if mrun is None else jnp.maximum(mrun, part)
        m_new[...] = mrun
        on = oe[0:A_VD] / oe[A_VD:A_VD + 1]
        od = on[:, 0:TQA] - lam * on[:, TQA:2 * TQA]
        y = od * lax.rsqrt(jnp.mean(od * od, axis=0, keepdims=True) + EPS) * (g_ref[...] * (1.0 - lam_init))
        o_ref[...] = y.T.astype(o_ref.dtype)

    @pl.when(i % 2 == 0)
    def _():
        step(sa_ref, ma_ref, sb_ref, mb_ref)

    @pl.when(i % 2 == 1)
    def _():
        step(sb_ref, mb_ref, sa_ref, ma_ref)


def _diff_attn(q, k, v, lam_vec, subln, lam_init, n_ctx):
    nb, t, _ = q.shape
    nq = t // TQA
    kv = pl.BlockSpec((None, t, 128), lambda b, h, i: (b, 0, h))
    qin = pl.BlockSpec((None, TQA, 128), lambda b, h, i: (b, jnp.minimum(i, nq - 1), h))
    out = pl.BlockSpec((None, TQA, 128), lambda b, h, i: (b, jnp.maximum(i - 1, 0), h))
    return pl.pallas_call(
        functools.partial(_diff_attn_kernel, lam_init=lam_init, n_ctx=n_ctx),
        grid=(nb, A_HEADS, nq + 1),
        in_specs=[qin, kv, kv, _full((4, A_HD)), _full((A_VD, 1))],
        out_specs=out,
        out_shape=jax.ShapeDtypeStruct((nb, t, A_HEADS * A_VD), BF16),
        scratch_shapes=[pltpu.VMEM((A_VD + ONES_ROWS, t), BF16), pltpu.VMEM((t, 2 * TQA), F32),
                        pltpu.VMEM((t, 2 * TQA), F32), pltpu.VMEM((8, 2 * TQA), F32), pltpu.VMEM((8, 2 * TQA), F32)],
        compiler_params=_cparams(("parallel", "parallel", "arbitrary")),
        name="diff_attn",
    )(q, k, v, lam_vec, subln)


def _gdn_tile_index(j, nt, rev):
    return jnp.where(j == 0, 0, nt - j) if rev else j


def _gdn_kernel(qkv_ref, bg_ref, o_ref, s_ref, *, rev, nt):
    j = pl.program_id(1)
    dirn = 1 if rev else 0
    nch = TM // B_CHUNK

    @pl.when(j == 0)
    def _():
        s_ref[...] = jnp.zeros_like(s_ref)

    bg = bg_ref[...]
    ri = lax.broadcasted_iota(jnp.int32, (TM, TM), 0)
    ci = lax.broadcasted_iota(jnp.int32, (TM, TM), 1)
    same = (ri // B_CHUNK) == (ci // B_CHUNK)
    incl = jnp.logical_and(same, (ri <= ci) if rev else (ri >= ci))
    strict = jnp.logical_and(same, (ri < ci) if rev else (ri > ci))
    eye = (ri == ci).astype(F32)
    inclb = incl.astype(BF16)
    b1 = bg.astype(BF16)
    r1 = bg - b1.astype(F32)
    b2 = r1.astype(BF16)
    b3 = (r1 - b2.astype(F32)).astype(BF16)
    gcum = _dot(inclb, b1) + (_dot(inclb, b2) + _dot(inclb, b3))
    gcum_t = gcum.T
    rchunk = lax.broadcasted_iota(jnp.int32, (TM, B_DK), 0) // B_CHUNK

    def by_chunk(z):
        return jnp.concatenate([jnp.where(rchunk == c, z, 0.0) for c in range(nch)], axis=1).astype(BF16)

    heads = range(B_HEADS)
    lasts = [c * B_CHUNK if rev else (c + 1) * B_CHUNK - 1 for c in range(nch)]
    q, k, v, beta, gcol, eg, qkm, p, pw = ([None] * B_HEADS for _ in range(9))
    for h in heads:
        q[h] = qkv_ref[:, h * 128:(h + 1) * 128]
        k[h] = qkv_ref[:, B_W + h * 128:B_W + (h + 1) * 128]
        v[h] = qkv_ref[:, 2 * B_W + h * 128:2 * B_W + (h + 1) * 128]
        cb = dirn * B_HEADS + h
        cg = 2 * B_HEADS + cb
        beta[h] = bg[:, cb:cb + 1]
        gcol[h] = gcum[:, cg:cg + 1]
        grow = gcum_t[cg:cg + 1, :]
        eg[h] = jnp.exp(gcol[h])
        decay = jnp.where(incl, jnp.exp(jnp.where(incl, gcol[h] - grow, 0.0)), 0.0)
        kb = k[h].astype(BF16)
        qkm[h] = _dot_nt(q[h].astype(BF16), kb) * decay
        pw[h] = jnp.where(strict, beta[h] * _dot_nt(kb, kb) * decay, 0.0)

    xr = ri ^ ci
    for lvl in range(6):
        joins = (xr >> lvl) == 1
        for h in heads:
            l_s = jnp.where(joins, pw[h], 0.0)
            if lvl == 0:
                p[h] = eye - l_s
            else:
                pb = p[h].astype(BF16)
                p[h] = p[h] - _dot(pb, _dot(l_s.astype(BF16), pb).astype(BF16))

    qku, qeff, mn = ([None] * B_HEADS for _ in range(3))
    for h in heads:
        rhs = jnp.concatenate([beta[h] * v[h], (beta[h] * eg[h]) * k[h]], axis=1)
        uw = _dot(p[h].astype(BF16), rhs.astype(BF16))
        qkuw = _dot(qkm[h].astype(BF16), uw.astype(BF16))
        qku[h] = qkuw[:, 0:128]
        qeff[h] = (q[h] * eg[h] - qkuw[:, 128:256]).astype(BF16)
        glast = jnp.concatenate(
            [jnp.broadcast_to(gcol[h][r:r + 1, :], (B_CHUNK, 1)) for r in lasts], axis=0)
        kdec = (k[h] * jnp.exp(glast - gcol[h])).astype(BF16)
        mn[h] = _dot_tn(kdec, jnp.concatenate([by_chunk(uw[:, 128:256]), by_chunk(uw[:, 0:128])], axis=1))

    for step in range(nch):
        c = nch - 1 - step if rev else step
        r0, r1 = c * B_CHUNK, (c + 1) * B_CHUNK
        for h in heads:
            sh = s_ref[h]
            shb = sh.astype(BF16)
            o_ref[r0:r1, h * 128:(h + 1) * 128] = (_dot(qeff[h][r0:r1], shb) + qku[h][r0:r1]).astype(o_ref.dtype)
            mc = mn[h][:, c * 128:(c + 1) * 128].astype(BF16)
            nc = mn[h][:, (nch + c) * 128:(nch + c + 1) * 128]
            gl = jnp.exp(gcol[h][lasts[c]:lasts[c] + 1, :])
            s_ref[h] = sh * gl - _dot(mc, shb) + nc


def _gdn(qkv, bg, rev):
    nb, t, w = qkv.shape
    nt = t // TM
    main = lambda ww: pl.BlockSpec((None, TM, ww), lambda b, j: (b, _gdn_tile_index(j, nt, rev), 0))
    return pl.pallas_call(
        functools.partial(_gdn_kernel, rev=rev, nt=nt),
        grid=(nb, nt),
        in_specs=[main(w), main(128)],
        out_specs=main(B_W),
        out_shape=jax.ShapeDtypeStruct((nb, t, B_W), BF16),
        scratch_shapes=[pltpu.VMEM((B_HEADS, B_DK, B_DK), F32)],
        compiler_params=_cparams(("parallel", "arbitrary")),
        name="gdn_rev" if rev else "gdn_fwd",
    )(qkv, bg)


def _even_out_kernel(x_ref, mod_ref, ya_ref, of_ref, or_ref, gate_ref, og_ref, w_ref, o_ref):
    d = D_MODEL
    ob = of_ref[...].astype(F32) + or_ref[...].astype(F32)
    gate = gate_ref[...].astype(F32)
    yb = jnp.concatenate(
        [_rms(ob[:, h * 128:(h + 1) * 128], og_ref[...]) * _silu(gate[:, h * 128:(h + 1) * 128])
         for h in range(B_HEADS)], axis=1)
    y = _dot(ya_ref[...], w_ref[0:512, :]) + _dot(yb.astype(BF16), w_ref[512:1024, :])
    o_ref[...] = x_ref[...] + mod_ref[:, 2 * d:3 * d] * y


def _even_out(x, mod, ya, of, orv, gate, og, w):
    nb, t, d = x.shape
    tile = lambda ww: pl.BlockSpec((None, TM, ww), lambda b, j: (b, j, 0))
    return pl.pallas_call(
        _even_out_kernel,
        grid=(nb, t // TM),
        in_specs=[tile(d), _mod_spec(nb), tile(512), tile(512), tile(512), tile(512), _full((1, 128)),
                  _full(w.shape)],
        out_specs=tile(d),
        out_shape=jax.ShapeDtypeStruct(x.shape, F32),
        compiler_params=_cparams(("parallel", "parallel")),
        name="even_out",
    )(x, mod, ya, of, orv, gate, og, w)


def _ffn_kernel(x_ref, prev_ref, next_ref, mod_ref, g_ref, wup_ref, cw_ref, wdn_ref, fg_ref, o_ref, act_ref,
                *, ntt, nct, final):
    d = D_MODEL
    nb = x_ref.shape[0]
    j = pl.program_id(0)
    mod = mod_ref[...]
    shift, scale, gate = mod[:, 3 * d:4 * d], mod[:, 4 * d:5 * d], mod[:, 5 * d:6 * d]
    g = g_ref[...]
    lflag = jnp.logical_and(j != 0, j != nct).astype(F32)
    rflag = jnp.logical_and(j != nct - 1, j != ntt - 1).astype(F32)
    x3 = pltpu.einshape("btd->tbd", x_ref[...])
    xp = pltpu.einshape("btd->tbd", prev_ref[...])[HALO - 1]
    xn = pltpu.einshape("btd->tbd", next_ref[...])[0]
    u3 = jnp.concatenate([(_modulate(xp, g, shift, scale) * lflag)[None],
                          _modulate(x3, g, shift, scale),
                          (_modulate(xn, g, shift, scale) * rflag)[None]], axis=0)
    u = u3.reshape((TTF + 2) * nb, d).astype(BF16)
    rows = TTF * nb
    for c in range(FFN // FC):
        c0 = c * FC
        hg = _dot(u, wup_ref[:, c0:c0 + FC])
        hv = _dot(u, wup_ref[:, FFN + c0:FFN + c0 + FC])
        cg = None
        cv = None
        for kk in range(FFN_CONV):
            tg = cw_ref[kk:kk + 1, c0:c0 + FC] * hg[kk * nb:kk * nb + rows]
            tv = cw_ref[kk:kk + 1, FFN + c0:FFN + c0 + FC] * hv[kk * nb:kk * nb + rows]
            cg = tg if cg is None else cg + tg
            cv = tv if cv is None else cv + tv
        act_ref[:, c0:c0 + FC] = (_silu(cg) * cv).astype(BF16)
    out3 = x3 + gate * _dot(act_ref[...], wdn_ref[...]).reshape(TTF, nb, d)
    if final:
        out3 = _rms(out3, fg_ref[...])
    o_ref[...] = pltpu.einshape("tbd->btd", out3)


def _ffn(x, mod2, g, wup, cw, wdn, n_ctx, final_g=None):
    nb, t, d = x.shape
    ntt, nct = t // TTF, n_ctx // TTF
    hb = TTF // HALO
    tile = pl.BlockSpec((nb, TTF, d), lambda j: (0, j, 0))
    prev = pl.BlockSpec((nb, HALO, d), lambda j: (0, jnp.maximum(j * hb - 1, 0), 0))
    nxt = pl.BlockSpec((nb, HALO, d), lambda j: (0, jnp.minimum((j + 1) * hb, t // HALO - 1), 0))
    modspec = pl.BlockSpec((None, nb, N_MOD * d), lambda j: (jnp.where(j < nct, 0, 1), 0, 0))
    resident = lambda shape: pl.BlockSpec(shape, lambda j: (0, 0), pipeline_mode=pl.Buffered(1))
    final = final_g is not None
    t_out = t - n_ctx if final else t
    out_tile = pl.BlockSpec((nb, TTF, d), lambda j: (0, jnp.maximum(j - nct, 0), 0)) if final else tile
    out = pl.pallas_call(
        functools.partial(_ffn_kernel, ntt=ntt, nct=nct, final=final),
        grid=(ntt,),
        in_specs=[tile, prev, nxt, modspec, _full((1, d)), resident(wup.shape), _full(cw.shape),
                  resident(wdn.shape), _full((1, d))],
        out_specs=out_tile,
        out_shape=jax.ShapeDtypeStruct((nb, t_out, d), F32),
        scratch_shapes=[pltpu.VMEM((TTF * nb, FFN), BF16)],
        compiler_params=_cparams(("arbitrary" if final else "parallel",)),
        name="ffn",
    )(x, x, x, mod2, g, wup, cw, wdn, final_g if final else g)
    return out


def _odd_in_kernel(x_ref, mod_ref, g_ref, c_ref, sa_ref, sb_ref, wr_ref, wq_ref, wkv_ref,
                   xr_ref, gate_ref, q_ref, k_ref, v_ref):
    d = D_MODEL
    mod = mod_ref[...]
    u = _modulate(x_ref[...], g_ref[...], mod[:, 0:d], mod[:, d:2 * d]).astype(BF16)
    c, sa, sb = c_ref[...], sa_ref[...], sb_ref[...]
    xr_ref[...] = _dot(u, wr_ref[:, 0:C_WIDTH])
    gate_ref[...] = _dot(u, wr_ref[:, C_WIDTH:2 * C_WIDTH])
    q_ref[...] = (_rope(_dot(u, wq_ref[...]), c, sa, sb) * (D_HD ** -0.5 * LOG2E)).astype(BF16)
    k_ref[...] = _rope(_dot(u, wkv_ref[:, 0:256]), c, sa, sb).astype(BF16)
    v = _dot(u, wkv_ref[:, 256:512])
    vlane = lax.broadcasted_iota(jnp.int32, v.shape, 1)
    v_ref[...] = jnp.where(vlane % 128 == D_HD, 1.0, v).astype(BF16)


def _odd_in(x, mod, g, tabs, wr, wq, wkv):
    nb, t, d = x.shape
    tile = lambda w: pl.BlockSpec((None, TM, w), lambda b, j: (b, j, 0))
    tab = pl.BlockSpec((TM, 128), lambda b, j: (j, 0))
    outs = [jax.ShapeDtypeStruct((nb, t, C_WIDTH), F32)] * 2 + [
        jax.ShapeDtypeStruct((nb, t, 512), BF16), jax.ShapeDtypeStruct((nb, t, 256), BF16),
        jax.ShapeDtypeStruct((nb, t, 256), BF16)]
    return pl.pallas_call(
        _odd_in_kernel,
        grid=(nb, t // TM),
        in_specs=[tile(d), _mod_spec(nb), _full((1, d)), tab, tab, tab, _full(wr.shape), _full(wq.shape),
                  _full(wkv.shape)],
        out_specs=[tile(C_WIDTH), tile(C_WIDTH), tile(512), tile(256), tile(256)],
        out_shape=outs,
        compiler_params=_cparams(("parallel", "parallel")),
        name="odd_in",
    )(x, mod, g, *tabs, wr, wq, wkv)


def _lru_tile_index(j, ntt, nct, rev):
    return jnp.where(j < nct, nct - 1 - j, ntt + nct - 1 - j) if rev else j


def _lru_kernel(x_ref, prev_ref, next_ref, cw_ref, cb_ref, w_ref, b_ref, lam_ref, o_ref,
                h_ref, a_ref, bc_ref, *, rev, ntt, nct):
    j = pl.program_id(0)
    jj = _lru_tile_index(j, ntt, nct, rev)
    nb = x_ref.shape[0]
    tmajor = lambda ref: pltpu.einshape("btc->tbc", ref[...])

    @pl.when(j == 0)
    def _():
        h_ref[...] = jnp.zeros_like(h_ref)

    lflag = jnp.logical_and(jj != 0, jj != nct).astype(F32)
    rflag = jnp.logical_and(jj != nct - 1, jj != ntt - 1).astype(F32)
    ext = jnp.concatenate([tmajor(prev_ref)[HALO - 2:HALO] * lflag, tmajor(x_ref),
                           tmajor(next_ref)[0:1] * rflag], axis=0)
    xc = cb_ref[...]
    for kk in range(C_CONV):
        xc = xc + cw_ref[kk:kk + 1, :] * ext[kk:kk + TT]
    xc2 = xc.reshape(TT * nb, C_WIDTH)
    z = _dot(xc2.astype(BF16), w_ref[...]) + b_ref[...]
    r = _sigmoid_t(z[:, 0:C_WIDTH])
    gi = _sigmoid_t(z[:, C_WIDTH:2 * C_WIDTH])
    log_a = -C_POW * r * _softplus(-lam_ref[...])
    a = jnp.exp(log_a)
    bc = jnp.sqrt(1.0 - a * a) * (gi * xc2)
    a_ref[...] = a.reshape(TT, nb, C_WIDTH)
    bc_ref[...] = bc.reshape(TT, nb, C_WIDTH)

    def body(s, h):
        t = TT - 1 - s if rev else s
        h = a_ref[t] * h + bc_ref[t]
        a_ref[t] = h
        return h

    h_ref[...] = lax.fori_loop(0, TT, body, h_ref[...], unroll=8)
    o_ref[...] = pltpu.einshape("tbc->btc", a_ref[...])


def _lru(xr, conv_w, conv_b, w, b, lam, n_ctx, rev):
    nb, t, c = xr.shape
    ntt, nct = t // TT, n_ctx // TT
    hb = TT // HALO
    idx = lambda j: _lru_tile_index(j, ntt, nct, rev)
    main = pl.BlockSpec((nb, TT, c), lambda j: (0, idx(j), 0))
    prev = pl.BlockSpec((nb, HALO, c), lambda j: (0, jnp.maximum(idx(j) * hb - 1, 0), 0))
    nxt = pl.BlockSpec((nb, HALO, c), lambda j: (0, jnp.minimum((idx(j) + 1) * hb, t // HALO - 1), 0))
    return pl.pallas_call(
        functools.partial(_lru_kernel, rev=rev, ntt=ntt, nct=nct),
        grid=(ntt,),
        in_specs=[main, prev, nxt, _full(conv_w.shape), _full(conv_b.shape), _full(w.shape), _full(b.shape),
                  _full(lam.shape)],
        out_specs=main,
        out_shape=jax.ShapeDtypeStruct(xr.shape, F32),
        scratch_shapes=[pltpu.VMEM((nb, c), F32), pltpu.VMEM((TT, nb, c), F32), pltpu.VMEM((TT, nb, c), F32)],
        compiler_params=_cparams(("arbitrary",)),
        name="lru_rev" if rev else "lru_fwd",
    )(xr, xr, xr, conv_w, conv_b, w, b, lam)


def _win_attn_kernel(q_ref, k_ref, v_ref, sink_ref, o_ref, *, n_ctx, n_lat):
    i = pl.program_id(1)
    nctx_tiles = n_ctx // TQ
    nlb = n_lat // TQ
    grp = D_HEADS // D_KV
    width = grp * TQ
    lane = lax.broadcasted_iota(jnp.int32, (TQ, 128), 1)
    low = lane < D_HD
    srow = lax.broadcasted_iota(jnp.int32, (SINK_ROWS, width), 0)
    vrow = lax.broadcasted_iota(jnp.int32, (SINK_ROWS, 128), 0)
    vlane = lax.broadcasted_iota(jnp.int32, (SINK_ROWS, 128), 1)
    v_sink = jnp.where(jnp.logical_and(vrow == 0, vlane == D_HD), 1.0, 0.0).astype(BF16)

    def stacked_q(g):
        parts = []
        for sl in range(grp // 2):
            slab = q_ref[:, (g * (grp // 2) + sl) * 128:(g * (grp // 2) + sl + 1) * 128]
            zero = jnp.zeros_like(slab)
            parts += [jnp.where(low, slab, zero), jnp.where(low, zero, slab)]
        return jnp.concatenate(parts, axis=0)

    def sink_row(g):
        return jnp.concatenate(
            [jnp.broadcast_to(sink_ref[g * grp + hh:g * grp + hh + 1, 0:1], (1, TQ)) for hh in range(grp)], axis=1)

    def finish(g, o):
        out = o[:, 0:D_HD] / o[:, D_HD:D_HD + 1]
        out = jnp.concatenate([out, jnp.zeros_like(out)], axis=1)
        for sl in range(grp // 2):
            a = out[(2 * sl) * TQ:(2 * sl + 1) * TQ]
            b = pltpu.roll(out[(2 * sl + 1) * TQ:(2 * sl + 2) * TQ], D_HD, 1)
            col = (g * (grp // 2) + sl) * 128
            o_ref[:, col:col + 128] = jnp.where(low, a, b).astype(o_ref.dtype)

    def attend(local):
        groups = range(D_KV)
        qz = [stacked_q(g) for g in groups]
        sk = [sink_row(g) for g in groups]
        sc, sl_, m = [None] * D_KV, [None] * D_KV, [None] * D_KV
        for g in groups:
            sc[g] = _dot_nt(k_ref[0:n_ctx, g * 128:(g + 1) * 128], qz[g])
            m[g] = jnp.maximum(jnp.max(sc[g], axis=0, keepdims=True), sk[g])
            if local is not None:
                start, mask = local
                kl = k_ref[pl.ds(start, 3 * TQ), g * 128:(g + 1) * 128]
                sl_[g] = jnp.where(mask, _dot_nt(kl, qz[g]), NEG_INF)
                m[g] = jnp.maximum(m[g], jnp.max(sl_[g], axis=0, keepdims=True))
        for g in groups:
            e_sink = jnp.where(srow == 0, jnp.exp2(sk[g] - m[g]), 0.0).astype(BF16)
            o = (_dot_tn(jnp.exp2(sc[g] - m[g]).astype(BF16), v_ref[0:n_ctx, g * 128:(g + 1) * 128])
                 + _dot_tn(e_sink, v_sink))
            if local is not None:
                vl = v_ref[pl.ds(local[0], 3 * TQ), g * 128:(g + 1) * 128]
                o = o + _dot_tn(jnp.exp2(sl_[g] - m[g]).astype(BF16), vl)
            finish(g, o)

    @pl.when(i < nctx_tiles)
    def _():
        attend(None)

    @pl.when(i >= nctx_tiles)
    def _():
        il = i - nctx_tiles
        kb = jnp.clip(il - 1, 0, nlb - 3)
        start = pl.multiple_of(n_ctx + kb * TQ, TQ)
        kpos = kb * TQ + lax.broadcasted_iota(jnp.int32, (3 * TQ, width), 0)
        qpos = il * TQ + (lax.broadcasted_iota(jnp.int32, (3 * TQ, width), 1) % TQ)
        mask = jnp.abs(kpos - qpos) <= WINDOW
        attend((start, mask))


def _win_attn(q, k, v, sink, n_ctx):
    nb, t, _ = q.shape
    kv = pl.BlockSpec((None, t, 256), lambda b, i: (b, 0, 0))
    qo = pl.BlockSpec((None, TQ, 512), lambda b, i: (b, i, 0))
    return pl.pallas_call(
        functools.partial(_win_attn_kernel, n_ctx=n_ctx, n_lat=t - n_ctx),
        grid=(nb, t // TQ),
        in_specs=[qo, kv, kv, _full(sink.shape)],
        out_specs=qo,
        out_shape=jax.ShapeDtypeStruct((nb, t, D_HEADS * D_HD), BF16),
        compiler_params=_cparams(("parallel", "arbitrary")),
        name="win_attn",
    )(q, k, v, sink)


def _odd_out_kernel(x_ref, mod_ref, hf_ref, hr_ref, gate_ref, od_ref, w_ref, o_ref):
    d = D_MODEL
    yc = (hf_ref[...] + hr_ref[...]) * _gelu_tanh(gate_ref[...])
    y = _dot(yc.astype(BF16), w_ref[0:512, :]) + _dot(od_ref[...], w_ref[512:1024, :])
    o_ref[...] = x_ref[...] + mod_ref[:, 2 * d:3 * d] * y


def _odd_out(x, mod, hf, hr, gate, od, w):
    nb, t, d = x.shape
    tile = lambda ww: pl.BlockSpec((None, TM, ww), lambda b, j: (b, j, 0))
    return pl.pallas_call(
        _odd_out_kernel,
        grid=(nb, t // TM),
        in_specs=[tile(d), _mod_spec(nb), tile(C_WIDTH), tile(C_WIDTH), tile(C_WIDTH), tile(512), _full(w.shape)],
        out_specs=tile(d),
        out_shape=jax.ShapeDtypeStruct(x.shape, F32),
        compiler_params=_cparams(("parallel", "parallel")),
        name="odd_out",
    )(x, mod, hf, hr, gate, od, w)


def _rope_tables(n_ctx, n_lat):
    pos = np.arange(n_lat)
    inv = ROPE_THETA ** (-np.arange(0, ROT_AXIS, 2, dtype=np.float64) / ROT_AXIS)
    ang_r = (pos // GRID_W)[:, None] * inv
    ang_c = (pos % GRID_W)[:, None] * inv
    lane = np.arange(128) % 64
    seg, f = lane // 16, lane % 16
    ang = np.where(seg[None, :] < 2, ang_r[:, f], ang_c[:, f])
    c = np.cos(ang)
    s = np.sin(ang)
    sa = np.where((seg % 2 == 0)[None, :], -s, 0.0)
    sb = np.where((seg % 2 == 1)[None, :], s, 0.0)
    pad = lambda a, fill: np.concatenate([np.full((n_ctx, 128), fill), a], axis=0).astype(np.float32)
    return jnp.asarray(pad(c, 1.0)), jnp.asarray(pad(sa, 0.0)), jnp.asarray(pad(sb, 0.0))


def _block_diag(w):
    eye = jnp.eye(C_BLOCKS, dtype=w.dtype)
    return jnp.einsum('hij,hg->higj', w, eye).reshape(C_WIDTH, C_WIDTH)


def kernel(x, c, ctx, c_ctx, w_ada, b_ada, norm_mix, norm_ffn, ffn_w_up, ffn_conv, ffn_w_down, final_norm,
           ev_w_in, ev_w_out, diff_lambda, diff_subln, gdn_conv, gdn_a_log, gdn_dt_bias, gdn_norm,
           od_w_in, od_w_out, lru_conv, lru_conv_b, lru_wa, lru_ba, lru_wx, lru_bx, lru_lambda, swa_sink):
    nb, n_lat, d = x.shape
    n_ctx = ctx.shape[1]
    depth = w_ada.shape[0]
    assert d == D_MODEL and n_ctx == TM and n_lat % TM == 0 and n_lat // TQ >= 3 and nb < MOD_ROWS
    t = n_ctx + n_lat

    xa = jnp.concatenate([ctx, x], axis=1)
    c_all = jnp.zeros((MOD_ROWS, d), F32).at[0:nb].set(c).at[nb].set(c_ctx)
    mod_all = _modulation(c_all, w_ada, b_ada).reshape(depth, MOD_ROWS, 1, N_MOD * d)
    tabs = _rope_tables(n_ctx, n_lat)
    row = lambda v: v.reshape(1, -1).astype(F32)

    for layer in range(depth):
        jx = layer // 2
        mod = mod_all[layer]
        mod2 = jnp.stack([jnp.broadcast_to(mod[nb], (nb, N_MOD * d)), mod[0:nb, 0]], axis=0)
        if layer % 2 == 0:
            lam_init = 0.8 - 0.6 * math.exp(-0.3 * layer)
            w_in = ev_w_in[jx]
            wa = w_in[:, 0:1536].astype(BF16)
            wg = w_in[:, 1536:3072].astype(BF16)
            wgate = w_in[:, 3072:3584].astype(BF16)
            wba = jnp.pad(w_in[:, 3584:3600], ((0, 0), (0, 112))).astype(BF16)
            pad16 = lambda v: jnp.pad(v.reshape(1, 8).astype(F32), ((0, 0), (8, 112)))
            q, k, v, qkv, gate, bg = _even_in(xa, mod, row(norm_mix[layer]), tabs, wa, wg, wgate, wba,
                                              gdn_conv[jx].astype(F32), pad16(gdn_a_log[jx]), pad16(gdn_dt_bias[jx]))
            ya = _diff_attn(q, k, v, diff_lambda[jx].astype(F32), diff_subln[jx].astype(F32).reshape(-1, 1),
                            lam_init, n_ctx)
            of = _gdn(qkv, bg, rev=False)
            orv = _gdn(qkv, bg, rev=True)
            xa = _even_out(xa, mod, ya, of, orv, gate, row(gdn_norm[jx]), ev_w_out[jx].astype(BF16))
        else:
            w_in = od_w_in[jx]
            wr = w_in[:, 0:1024].astype(BF16)
            wq = w_in[:, 1024:1536].astype(BF16)
            dup = lambda w: jnp.concatenate([w[:, 0:64], w[:, 0:64], w[:, 64:128], w[:, 64:128]], axis=1)
            zpad = lambda w: jnp.concatenate([w[:, 0:64], jnp.zeros_like(w[:, 0:64]), w[:, 64:128], jnp.zeros_like(w[:, 0:64])], axis=1)
            wkv = jnp.concatenate([dup(w_in[:, 1536:1664]), zpad(w_in[:, 1664:1792])], axis=1).astype(BF16)
            xr, gate, q, k, v = _odd_in(xa, mod, row(norm_mix[layer]), tabs, wr, wq, wkv)
            hs = []
            for dd in range(2):
                wbig = jnp.concatenate([_block_diag(lru_wa[jx, dd]), _block_diag(lru_wx[jx, dd])], axis=1)
                bbig = jnp.concatenate([lru_ba[jx, dd], lru_bx[jx, dd]]).reshape(1, -1).astype(F32)
                hs.append(_lru(xr, lru_conv[jx].astype(F32), row(lru_conv_b[jx]), wbig.astype(BF16), bbig,
                               row(lru_lambda[jx, dd]), n_ctx, rev=(dd == 1)))
            sink = jnp.broadcast_to(swa_sink[jx].astype(F32)[:, None] * LOG2E, (D_HEADS, 128))
            od = _win_attn(q, k, v, sink, n_ctx)
            xa = _odd_out(xa, mod, hs[0], hs[1], gate, od, od_w_out[jx].astype(BF16))
        xa = _ffn(xa, mod2, row(norm_ffn[layer]), ffn_w_up[layer].astype(BF16), ffn_conv[layer].astype(F32),
                  ffn_w_down[layer].astype(BF16), n_ctx, row(final_norm) if layer == depth - 1 else None)
    return xa
```

```python
import functools
import math

import jax
import jax.numpy as jnp
import numpy as np
from jax import lax
from jax.experimental import pallas as pl
from jax.experimental.pallas import tpu as pltpu

F32 = jnp.float32
BF16 = jnp.bfloat16
HIGHEST = lax.Precision.HIGHEST

D_MODEL = 1024
GRID_W = 64
EPS = 1e-6
NEG_INF = -1e30
N_MOD = 6
ROPE_THETA = 10000.0
ROT_AXIS = 32
A_HEADS = 4
A_HD = 64
A_VD = 128
B_HEADS = 4
B_DK = 128
B_W = 512
B_CONV = 4
B_CHUNK = 64
C_WIDTH = 512
C_BLOCKS = 8
C_BD = 64
C_CONV = 4
C_POW = 8.0
D_HEADS = 8
D_KV = 2
D_HD = 64
WINDOW = 128
FFN = 2816
FFN_CONV = 3

TM = 256
TQ = 128
TQA = 256
KB = 256
ONES_ROWS = 16
SINK_ROWS = 16
LOG2E = math.log2(math.e)
TT = 64
GDN_NB = 2
HALO = 8
FC = 256
TTF = 64
MOD_ROWS = 16
VMEM_LIMIT = 56 * 1024 * 1024


def _cparams(sem):
    return pltpu.CompilerParams(dimension_semantics=sem, vmem_limit_bytes=VMEM_LIMIT)


def _sigmoid(x):
    return 1.0 / (1.0 + jnp.exp(-x))


def _sigmoid_t(x):
    return 0.5 * (1.0 + jnp.tanh(0.5 * x))


def _silu(x):
    return x * _sigmoid(x)


def _softplus(x):
    return jnp.maximum(x, 0.0) + jnp.log(1.0 + jnp.exp(-jnp.abs(x)))


def _gelu_tanh(x):
    return 0.5 * x * (1.0 + jnp.tanh(math.sqrt(2.0 / math.pi) * (x + 0.044715 * (x * x * x))))


def _dot(a, b):
    return jnp.dot(a, b, preferred_element_type=F32)


def _dot_hi(a, b):
    return jnp.dot(a, b, preferred_element_type=F32, precision=HIGHEST)


def _dot3(a, b):
    ah = a.astype(BF16)
    al = (a - ah.astype(F32)).astype(BF16)
    bh = b.astype(BF16)
    bl = (b - bh.astype(F32)).astype(BF16)
    return _dot(ah, bh) + (_dot(ah, bl) + _dot(al, bh))


def _dot_nt(a, b):
    return lax.dot_general(a, b, (((1,), (1,)), ((), ())), preferred_element_type=F32)


def _dot_tn(a, b):
    return lax.dot_general(a, b, (((0,), (0,)), ((), ())), preferred_element_type=F32)


def _rms(x, g):
    return x * lax.rsqrt(jnp.mean(x * x, axis=-1, keepdims=True) + EPS) * g


def _modulate(x, g, shift, scale):
    return _rms(x, g) * (1.0 + scale) + shift


def _rope128(z, c, sa, sb):
    return z * c + pltpu.roll(z, 112, 1) * sa + pltpu.roll(z, 16, 1) * sb


def _rope(z, c, sa, sb):
    n = z.shape[1] // 128
    return jnp.concatenate([_rope128(z[:, i * 128:(i + 1) * 128], c, sa, sb) for i in range(n)], axis=1)


def _mod_kernel(s_ref, w_ref, b_ref, o_ref):
    s = _silu(s_ref[...])
    o_ref[...] = _dot_hi(s, w_ref[...]) + b_ref[...]


def _modulation(c_all, w_ada, b_ada):
    depth, d, n = w_ada.shape
    tn = 1536
    return pl.pallas_call(
        _mod_kernel,
        grid=(depth, n // tn),
        in_specs=[pl.BlockSpec((MOD_ROWS, d), lambda l, j: (0, 0)),
                  pl.BlockSpec((None, d, tn), lambda l, j: (l, 0, j)),
                  pl.BlockSpec((None, 1, tn), lambda l, j: (l, 0, j))],
        out_specs=pl.BlockSpec((None, MOD_ROWS, tn), lambda l, j: (l, 0, j)),
        out_shape=jax.ShapeDtypeStruct((depth, MOD_ROWS, n), F32),
        compiler_params=_cparams(("arbitrary", "arbitrary")),
        name="modulation",
    )(c_all, w_ada, b_ada.reshape(depth, 1, n))


def _mod_spec(nb):
    return pl.BlockSpec((None, 1, N_MOD * D_MODEL), lambda b, j: (jnp.where(j == 0, nb, b), 0, 0))


def _full(shape):
    nd = len(shape)
    return pl.BlockSpec(shape, lambda *_: (0,) * nd)


def _even_in_kernel(x_ref, prev_ref, next_ref, mod_ref, g_ref, c_ref, sa_ref, sb_ref, wa_ref, wg_ref, wgate_ref,
                    wba_ref, cw_ref, alog_ref, dtb_ref, q_ref, k_ref, v_ref, qkv_ref, gate_ref, bg_ref, ext_ref,
                    *, nt):
    d = D_MODEL
    j = pl.program_id(1)
    mod = mod_ref[...]
    g, shift, scale = g_ref[...], mod[:, 0:d], mod[:, d:2 * d]
    uf = _modulate(x_ref[...], g, shift, scale)
    u = uf.astype(BF16)
    lflag = (j >= 2).astype(F32)
    rflag = jnp.logical_and(j >= 1, j <= nt - 2).astype(F32)
    halo = jnp.concatenate([_modulate(prev_ref[...], g, shift, scale) * lflag,
                            _modulate(next_ref[...], g, shift, scale) * rflag], axis=0).astype(BF16)
    zh = _dot(halo, wg_ref[...])
    ext_ref[0:HALO, :] = zh[0:HALO]
    ext_ref[HALO:HALO + TM, :] = _dot(u, wg_ref[...])
    ext_ref[HALO + TM:2 * HALO + TM, :] = zh[HALO:2 * HALO]
    acc = None
    for kk in range(B_CONV):
        term = cw_ref[kk:kk + 1, :] * ext_ref[pl.ds(HALO - 2 + kk, TM), :]
        acc = term if acc is None else acc + term
    act = _silu(acc)

    def l2n(zz):
        return zz * lax.rsqrt(jnp.sum(zz * zz, axis=-1, keepdims=True) + EPS)

    for h in range(B_HEADS):
        qkv_ref[:, h * 128:(h + 1) * 128] = l2n(act[:, h * 128:(h + 1) * 128]) * (B_DK ** -0.5)
        qkv_ref[:, B_W + h * 128:B_W + (h + 1) * 128] = l2n(act[:, B_W + h * 128:B_W + (h + 1) * 128])
    qkv_ref[:, 2 * B_W:3 * B_W] = act[:, 2 * B_W:3 * B_W]

    c, sa, sb = c_ref[...], sa_ref[...], sb_ref[...]
    q = _dot(u, wa_ref[:, 0:512])
    q_ref[...] = (_rope(q, c, sa, sb) * (A_HD ** -0.5 * LOG2E)).astype(BF16)
    k = _dot(u, wa_ref[:, 512:1024])
    k_ref[...] = _rope(k, c, sa, sb).astype(BF16)
    v_ref[...] = _dot(u, wa_ref[:, 1024:1536]).astype(BF16)
    gate_ref[...] = _dot(u, wgate_ref[...]).astype(gate_ref.dtype)
    z = _dot(u, wba_ref[...])
    lane = lax.broadcasted_iota(jnp.int32, z.shape, 1)
    beta = _sigmoid(z)
    gdec = -jnp.exp(alog_ref[...]) * _softplus(z + dtb_ref[...])
    bg_ref[...] = jnp.where(lane < 2 * B_HEADS, beta, jnp.where(lane < 4 * B_HEADS, gdec, 0.0))


def _even_in(x, mod, g, tabs, wa, wg, wgate, wba, cw, alog, dtb):
    nb, t, d = x.shape
    nt = t // TM
    hb = TM // HALO
    tile = lambda w: pl.BlockSpec((None, TM, w), lambda b, j: (b, j, 0))
    prev = pl.BlockSpec((None, HALO, d), lambda b, j: (b, jnp.maximum(j * hb - 1, 0), 0))
    nxt = pl.BlockSpec((None, HALO, d), lambda b, j: (b, jnp.minimum((j + 1) * hb, t // HALO - 1), 0))
    tab = pl.BlockSpec((TM, 128), lambda b, j: (j, 0))
    outs = [jax.ShapeDtypeStruct((nb, t, 512), BF16)] * 3 + [
        jax.ShapeDtypeStruct((nb, t, 1536), F32), jax.ShapeDtypeStruct((nb, t, 512), BF16),
        jax.ShapeDtypeStruct((nb, t, 128), F32)]
    return pl.pallas_call(
        functools.partial(_even_in_kernel, nt=nt),
        grid=(nb, nt),
        in_specs=[tile(d), prev, nxt, _mod_spec(nb), _full((1, d)), tab, tab, tab, _full(wa.shape), _full(wg.shape),
                  _full(wgate.shape), _full(wba.shape), _full(cw.shape), _full((1, 128)), _full((1, 128))],
        out_specs=[tile(512), tile(512), tile(512), tile(1536), tile(512), tile(128)],
        out_shape=outs,
        scratch_shapes=[pltpu.VMEM((TM + 2 * HALO, 1536), F32)],
        compiler_params=_cparams(("parallel", "parallel")),
        name="even_in",
    )(x, x, x, mod, g, *tabs, wa, wg, wgate, wba, cw, alog, dtb)


def _diff_attn_kernel(q_ref, k_ref, v_ref, lv_ref, g_ref, o_ref, vt_ref, sa_ref, sb_ref, ma_ref, mb_ref,
                      *, lam_init, n_ctx):
    i = pl.program_id(2)
    t = k_ref.shape[0]
    nblk = t // KB

    @pl.when(i == 0)
    def _():
        vt_ref[0:A_VD, :] = v_ref[...].astype(F32).T.astype(BF16)
        orow = lax.broadcasted_iota(jnp.int32, (ONES_ROWS, t), 0)
        vt_ref[A_VD:A_VD + ONES_ROWS, :] = jnp.where(orow == 0, 1.0, 0.0).astype(BF16)
        sb_ref[...] = jnp.zeros_like(sb_ref)
        mb_ref[...] = jnp.zeros_like(mb_ref)

    def step(s_new, m_new, s_old, m_old):
        lv = lv_ref[...]
        lam = (jnp.exp(jnp.sum(lv[0:1] * lv[1:2], axis=-1, keepdims=True))
               - jnp.exp(jnp.sum(lv[2:3] * lv[3:4], axis=-1, keepdims=True)) + lam_init)
        m = jnp.max(m_old[...], axis=0, keepdims=True)
        is_ctx = jnp.minimum(i, pl.num_programs(2) - 2) < n_ctx // TQA
        q = q_ref[...]
        lane = lax.broadcasted_iota(jnp.int32, q.shape, 1)
        zero = jnp.zeros_like(q)
        qz = jnp.concatenate([jnp.where(lane < A_HD, q, zero), jnp.where(lane >= A_HD, q, zero)], axis=0)
        mrun = None
        oe = None
        for kb in range(nblk):
            rows = slice(kb * KB, (kb + 1) * KB)
            e = jnp.exp2(s_old[rows, :] - m).astype(BF16)
            part = _dot(vt_ref[:, rows], e)
            oe = part if oe is None else oe + part
            sblk = _dot_nt(k_ref[rows, :], qz)
            if kb >= n_ctx // KB:
                sblk = jnp.where(is_ctx, NEG_INF, sblk)
            s_new[rows, :] = sblk
            part = jnp.max(sblk.reshape(KB // 8, 8, 2 * TQA), axis=0)
            mrun = part if mrun is None else jnp.maximum(mrun, part)
        m_new[...] = mrun
        on = oe[0:A_VD] / oe[A_VD:A_VD + 1]
        od = on[:, 0:TQA] - lam * on[:, TQA:2 * TQA]
        y = od * lax.rsqrt(jnp.mean(od * od, axis=0, keepdims=True) + EPS) * (g_ref[...] * (1.0 - lam_init))
        o_ref[...] = y.T.astype(o_ref.dtype)

    @pl.when(i % 2 == 0)
    def _():
        step(sa_ref, ma_ref, sb_ref, mb_ref)

    @pl.when(i % 2 == 1)
    def _():
        step(sb_ref, mb_ref, sa_ref, ma_ref)


def _diff_attn(q, k, v, lam_vec, subln, lam_init, n_ctx):
    nb, t, _ = q.shape
    nq = t // TQA
    kv = pl.BlockSpec((None, t, 128), lambda b, h, i: (b, 0, h))
    qin = pl.BlockSpec((None, TQA, 128), lambda b, h, i: (b, jnp.minimum(i, nq - 1), h))
    out = pl.BlockSpec((None, TQA, 128), lambda b, h, i: (b, jnp.maximum(i - 1, 0), h))
    return pl.pallas_call(
        functools.partial(_diff_attn_kernel, lam_init=lam_init, n_ctx=n_ctx),
        grid=(nb, A_HEADS, nq + 1),
        in_specs=[qin, kv, kv, _full((4, A_HD)), _full((A_VD, 1))],
        out_specs=out,
        out_shape=jax.ShapeDtypeStruct((nb, t, A_HEADS * A_VD), BF16),
        scratch_shapes=[pltpu.VMEM((A_VD + ONES_ROWS, t), BF16), pltpu.VMEM((t, 2 * TQA), F32),
                        pltpu.VMEM((t, 2 * TQA), F32), pltpu.VMEM((8, 2 * TQA), F32), pltpu.VMEM((8, 2 * TQA), F32)],
        compiler_params=_cparams(("parallel", "parallel", "arbitrary")),
        name="diff_attn",
    )(q, k, v, lam_vec, subln)


def _gdn_tile_index(j, nt, rev):
    return jnp.where(j == 0, 0, nt - j) if rev else j


def _gdn_kernel(qkv_ref, bg_ref, o_ref, s_ref, *, rev, nt):
    j = pl.program_id(1)
    dirn = 1 if rev else 0
    nch = TM // B_CHUNK

    @pl.when(j == 0)
    def _():
        s_ref[...] = jnp.zeros_like(s_ref)

    ri = lax.broadcasted_iota(jnp.int32, (TM, TM), 0)
    ci = lax.broadcasted_iota(jnp.int32, (TM, TM), 1)
    same = (ri // B_CHUNK) == (ci // B_CHUNK)
    incl = jnp.logical_and(same, (ri <= ci) if rev else (ri >= ci))
    strict = jnp.logical_and(same, (ri < ci) if rev else (ri > ci))
    eye = (ri == ci).astype(F32)
    inclb = incl.astype(BF16)
    rchunk = lax.broadcasted_iota(jnp.int32, (TM, B_DK), 0) // B_CHUNK

    def by_chunk(z):
        return jnp.concatenate([jnp.where(rchunk == c, z, 0.0) for c in range(nch)], axis=1).astype(BF16)

    chains = [(bi, h) for bi in range(GDN_NB) for h in range(B_HEADS)]
    nchain = len(chains)
    lasts = [c * B_CHUNK if rev else (c + 1) * B_CHUNK - 1 for c in range(nch)]
    q, k, v, beta, gcol, eg, qkm, p, pw = ([None] * nchain for _ in range(9))
    for bi in range(GDN_NB):
        bg = bg_ref[bi]
        b1 = bg.astype(BF16)
        r1 = bg - b1.astype(F32)
        b2 = r1.astype(BF16)
        b3 = (r1 - b2.astype(F32)).astype(BF16)
        gcum = _dot(inclb, b1) + (_dot(inclb, b2) + _dot(inclb, b3))
        gcum_t = gcum.T
        for h in range(B_HEADS):
            n = bi * B_HEADS + h
            q[n] = qkv_ref[bi, :, h * 128:(h + 1) * 128]
            k[n] = qkv_ref[bi, :, B_W + h * 128:B_W + (h + 1) * 128]
            v[n] = qkv_ref[bi, :, 2 * B_W + h * 128:2 * B_W + (h + 1) * 128]
            cb = dirn * B_HEADS + h
            cg = 2 * B_HEADS + cb
            beta[n] = bg[:, cb:cb + 1]
            gcol[n] = gcum[:, cg:cg + 1]
            grow = gcum_t[cg:cg + 1, :]
            eg[n] = jnp.exp(gcol[n])
            decay = jnp.where(incl, jnp.exp(jnp.where(incl, gcol[n] - grow, 0.0)), 0.0)
            kb = k[n].astype(BF16)
            qkm[n] = _dot_nt(q[n].astype(BF16), kb) * decay
            pw[n] = jnp.where(strict, beta[n] * _dot_nt(kb, kb) * decay, 0.0)

    xr = ri ^ ci
    for lvl in range(6):
        joins = (xr >> lvl) == 1
        for n in range(nchain):
            l_s = jnp.where(joins, pw[n], 0.0)
            if lvl == 0:
                p[n] = eye - l_s
            else:
                pb = p[n].astype(BF16)
                p[n] = p[n] - _dot(pb, _dot(l_s.astype(BF16), pb).astype(BF16))

    qku, qeff, mn = ([None] * nchain for _ in range(3))
    for n in range(nchain):
        rhs = jnp.concatenate([beta[n] * v[n], (beta[n] * eg[n]) * k[n]], axis=1)
        uw = _dot(p[n].astype(BF16), rhs.astype(BF16))
        qkuw = _dot(qkm[n].astype(BF16), uw.astype(BF16))
        qku[n] = qkuw[:, 0:128]
        qeff[n] = (q[n] * eg[n] - qkuw[:, 128:256]).astype(BF16)
        glast = jnp.concatenate(
            [jnp.broadcast_to(gcol[n][r:r + 1, :], (B_CHUNK, 1)) for r in lasts], axis=0)
        kdec = (k[n] * jnp.exp(glast - gcol[n])).astype(BF16)
        mn[n] = _dot_tn(kdec, jnp.concatenate([by_chunk(uw[:, 128:256]), by_chunk(uw[:, 0:128])], axis=1))

    for step in range(nch):
        c = nch - 1 - step if rev else step
        r0, r1 = c * B_CHUNK, (c + 1) * B_CHUNK
        for n, (bi, h) in enumerate(chains):
            sh = s_ref[n]
            shb = sh.astype(BF16)
            o_ref[bi, r0:r1, h * 128:(h + 1) * 128] = (_dot(qeff[n][r0:r1], shb) + qku[n][r0:r1]).astype(o_ref.dtype)
            mc = mn[n][:, c * 128:(c + 1) * 128].astype(BF16)
            nc = mn[n][:, (nch + c) * 128:(nch + c + 1) * 128]
            gl = jnp.exp(gcol[n][lasts[c]:lasts[c] + 1, :])
            s_ref[n] = sh * gl - _dot(mc, shb) + nc


def _gdn(qkv, bg, rev):
    nb, t, w = qkv.shape
    nt = t // TM
    main = lambda ww: pl.BlockSpec((GDN_NB, TM, ww), lambda b, j: (b, _gdn_tile_index(j, nt, rev), 0))
    return pl.pallas_call(
        functools.partial(_gdn_kernel, rev=rev, nt=nt),
        grid=(nb // GDN_NB, nt),
        in_specs=[main(w), main(128)],
        out_specs=main(B_W),
        out_shape=jax.ShapeDtypeStruct((nb, t, B_W), BF16),
        scratch_shapes=[pltpu.VMEM((GDN_NB * B_HEADS, B_DK, B_DK), F32)],
        compiler_params=_cparams(("parallel", "arbitrary")),
        name="gdn_rev" if rev else "gdn_fwd",
    )(qkv, bg)


def _even_out_kernel(x_ref, mod_ref, ya_ref, of_ref, or_ref, gate_ref, og_ref, w_ref, o_ref):
    d = D_MODEL
    ob = of_ref[...].astype(F32) + or_ref[...].astype(F32)
    gate = gate_ref[...].astype(F32)
    yb = jnp.concatenate(
        [_rms(ob[:, h * 128:(h + 1) * 128], og_ref[...]) * _silu(gate[:, h * 128:(h + 1) * 128])
         for h in range(B_HEADS)], axis=1)
    y = _dot(ya_ref[...], w_ref[0:512, :]) + _dot(yb.astype(BF16), w_ref[512:1024, :])
    o_ref[...] = x_ref[...] + mod_ref[:, 2 * d:3 * d] * y


def _even_out(x, mod, ya, of, orv, gate, og, w):
    nb, t, d = x.shape
    tile = lambda ww: pl.BlockSpec((None, TM, ww), lambda b, j: (b, j, 0))
    return pl.pallas_call(
        _even_out_kernel,
        grid=(nb, t // TM),
        in_specs=[tile(d), _mod_spec(nb), tile(512), tile(512), tile(512), tile(512), _full((1, 128)),
                  _full(w.shape)],
        out_specs=tile(d),
        out_shape=jax.ShapeDtypeStruct(x.shape, F32),
        compiler_params=_cparams(("parallel", "parallel")),
        name="even_out",
    )(x, mod, ya, of, orv, gate, og, w)


def _ffn_kernel(x_ref, prev_ref, next_ref, mod_ref, g_ref, wup_ref, cw_ref, wdn_ref, fg_ref, o_ref, act_ref,
                *, ntt, nct, final):
    d = D_MODEL
    nb = x_ref.shape[0]
    j = pl.program_id(0)

    def body():
        mod = mod_ref[...]
        shift, scale, gate = mod[:, 3 * d:4 * d], mod[:, 4 * d:5 * d], mod[:, 5 * d:6 * d]
        g = g_ref[...]
        lflag = jnp.logical_and(j != 0, j != nct).astype(F32)
        rflag = jnp.logical_and(j != nct - 1, j != ntt - 1).astype(F32)
        x3 = pltpu.einshape("btd->tbd", x_ref[...])
        xp = pltpu.einshape("btd->tbd", prev_ref[...])[HALO - 1]
        xn = pltpu.einshape("btd->tbd", next_ref[...])[0]
        u3 = jnp.concatenate([(_modulate(xp, g, shift, scale) * lflag)[None],
                              _modulate(x3, g, shift, scale),
                              (_modulate(xn, g, shift, scale) * rflag)[None]], axis=0)
        u = u3.reshape((TTF + 2) * nb, d).astype(BF16)
        rows = TTF * nb
        for c in range(FFN // FC):
            c0 = c * FC
            hg = _dot(u, wup_ref[:, c0:c0 + FC])
            hv = _dot(u, wup_ref[:, FFN + c0:FFN + c0 + FC])
            cg = None
            cv = None
            for kk in range(FFN_CONV):
                tg = cw_ref[kk:kk + 1, c0:c0 + FC] * hg[kk * nb:kk * nb + rows]
                tv = cw_ref[kk:kk + 1, FFN + c0:FFN + c0 + FC] * hv[kk * nb:kk * nb + rows]
                cg = tg if cg is None else cg + tg
                cv = tv if cv is None else cv + tv
            act_ref[:, c0:c0 + FC] = (_silu(cg) * cv).astype(BF16)
        out3 = x3 + gate * _dot(act_ref[...], wdn_ref[...]).reshape(TTF, nb, d)
        if final:
            out3 = _rms(out3, fg_ref[...])
        o_ref[...] = pltpu.einshape("tbd->btd", out3)

    if final:
        pl.when(j >= nct)(body)
    else:
        body()


def _ffn(x, mod2, g, wup, cw, wdn, n_ctx, final_g=None):
    nb, t, d = x.shape
    ntt, nct = t // TTF, n_ctx // TTF
    hb = TTF // HALO
    tile = pl.BlockSpec((nb, TTF, d), lambda j: (0, j, 0))
    prev = pl.BlockSpec((nb, HALO, d), lambda j: (0, jnp.maximum(j * hb - 1, 0), 0))
    nxt = pl.BlockSpec((nb, HALO, d), lambda j: (0, jnp.minimum((j + 1) * hb, t // HALO - 1), 0))
    modspec = pl.BlockSpec((None, nb, N_MOD * d), lambda j: (jnp.where(j < nct, 0, 1), 0, 0))
    resident = lambda shape: pl.BlockSpec(shape, lambda j: (0, 0), pipeline_mode=pl.Buffered(1))
    final = final_g is not None
    t_out = t - n_ctx if final else t
    out_tile = pl.BlockSpec((nb, TTF, d), lambda j: (0, jnp.maximum(j - nct, 0), 0)) if final else tile
    out = pl.pallas_call(
        functools.partial(_ffn_kernel, ntt=ntt, nct=nct, final=final),
        grid=(ntt,),
        in_specs=[tile, prev, nxt, modspec, _full((1, d)), resident(wup.shape), _full(cw.shape),
                  resident(wdn.shape), _full((1, d))],
        out_specs=out_tile,
        out_shape=jax.ShapeDtypeStruct((nb, t_out, d), F32),
        scratch_shapes=[pltpu.VMEM((TTF * nb, FFN), BF16)],
        compiler_params=_cparams(("arbitrary" if final else "parallel",)),
        name="ffn",
    )(x, x, x, mod2, g, wup, cw, wdn, final_g if final else g)
    return out


def _odd_in_kernel(x_ref, mod_ref, g_ref, c_ref, sa_ref, sb_ref, wr_ref, wq_ref, wkv_ref,
                   xr_ref, gate_ref, q_ref, k_ref, v_ref):
    d = D_MODEL
    mod = mod_ref[...]
    u = _modulate(x_ref[...], g_ref[...], mod[:, 0:d], mod[:, d:2 * d]).astype(BF16)
    c, sa, sb = c_ref[...], sa_ref[...], sb_ref[...]
    xr_ref[...] = _dot(u, wr_ref[:, 0:C_WIDTH])
    gate_ref[...] = _dot(u, wr_ref[:, C_WIDTH:2 * C_WIDTH])
    q_ref[...] = (_rope(_dot(u, wq_ref[...]), c, sa, sb) * (D_HD ** -0.5 * LOG2E)).astype(BF16)
    k_ref[...] = _rope(_dot(u, wkv_ref[:, 0:256]), c, sa, sb).astype(BF16)
    v = _dot(u, wkv_ref[:, 256:512])
    vlane = lax.broadcasted_iota(jnp.int32, v.shape, 1)
    v_ref[...] = jnp.where(vlane % 128 == D_HD, 1.0, v).astype(BF16)


def _odd_in(x, mod, g, tabs, wr, wq, wkv):
    nb, t, d = x.shape
    tile = lambda w: pl.BlockSpec((None, TM, w), lambda b, j: (b, j, 0))
    tab = pl.BlockSpec((TM, 128), lambda b, j: (j, 0))
    outs = [jax.ShapeDtypeStruct((nb, t, C_WIDTH), F32)] * 2 + [
        jax.ShapeDtypeStruct((nb, t, 512), BF16), jax.ShapeDtypeStruct((nb, t, 256), BF16),
        jax.ShapeDtypeStruct((nb, t, 256), BF16)]
    return pl.pallas_call(
        _odd_in_kernel,
        grid=(nb, t // TM),
        in_specs=[tile(d), _mod_spec(nb), _full((1, d)), tab, tab, tab, _full(wr.shape), _full(wq.shape),
                  _full(wkv.shape)],
        out_specs=[tile(C_WIDTH), tile(C_WIDTH), tile(512), tile(256), tile(256)],
        out_shape=outs,
        compiler_params=_cparams(("parallel", "parallel")),
        name="odd_in",
    )(x, mod, g, *tabs, wr, wq, wkv)


def _lru_tile_index(j, ntt, nct, rev):
    return jnp.where(j < nct, nct - 1 - j, ntt + nct - 1 - j) if rev else j


def _lru_kernel(x_ref, prev_ref, next_ref, cw_ref, cb_ref, w_ref, b_ref, lam_ref, o_ref,
                h_ref, a_ref, bc_ref, *, rev, ntt, nct):
    j = pl.program_id(0)
    jj = _lru_tile_index(j, ntt, nct, rev)
    nb = x_ref.shape[0]
    tmajor = lambda ref: pltpu.einshape("btc->tbc", ref[...])

    @pl.when(j == 0)
    def _():
        h_ref[...] = jnp.zeros_like(h_ref)

    lflag = jnp.logical_and(jj != 0, jj != nct).astype(F32)
    rflag = jnp.logical_and(jj != nct - 1, jj != ntt - 1).astype(F32)
    ext = jnp.concatenate([tmajor(prev_ref)[HALO - 2:HALO] * lflag, tmajor(x_ref),
                           tmajor(next_ref)[0:1] * rflag], axis=0)
    xc = cb_ref[...]
    for kk in range(C_CONV):
        xc = xc + cw_ref[kk:kk + 1, :] * ext[kk:kk + TT]
    xc2 = xc.reshape(TT * nb, C_WIDTH)
    z = _dot(xc2.astype(BF16), w_ref[...]) + b_ref[...]
    r = _sigmoid_t(z[:, 0:C_WIDTH])
    gi = _sigmoid_t(z[:, C_WIDTH:2 * C_WIDTH])
    log_a = -C_POW * r * _softplus(-lam_ref[...])
    a = jnp.exp(log_a)
    bc = jnp.sqrt(1.0 - a * a) * (gi * xc2)
    a_ref[...] = a.reshape(TT, nb, C_WIDTH)
    bc_ref[...] = bc.reshape(TT, nb, C_WIDTH)

    def body(s, h):
        t = TT - 1 - s if rev else s
        h = a_ref[t] * h + bc_ref[t]
        a_ref[t] = h
        return h

    h_ref[...] = lax.fori_loop(0, TT, body, h_ref[...], unroll=8)
    o_ref[...] = pltpu.einshape("tbc->btc", a_ref[...])


def _lru(xr, conv_w, conv_b, w, b, lam, n_ctx, rev):
    nb, t, c = xr.shape
    ntt, nct = t // TT, n_ctx // TT
    hb = TT // HALO
    idx = lambda j: _lru_tile_index(j, ntt, nct, rev)
    main = pl.BlockSpec((nb, TT, c), lambda j: (0, idx(j), 0))
    prev = pl.BlockSpec((nb, HALO, c), lambda j: (0, jnp.maximum(idx(j) * hb - 1, 0), 0))
    nxt = pl.BlockSpec((nb, HALO, c), lambda j: (0, jnp.minimum((idx(j) + 1) * hb, t // HALO - 1), 0))
    return pl.pallas_call(
        functools.partial(_lru_kernel, rev=rev, ntt=ntt, nct=nct),
        grid=(ntt,),
        in_specs=[main, prev, nxt, _full(conv_w.shape), _full(conv_b.shape), _full(w.shape), _full(b.shape),
                  _full(lam.shape)],
        out_specs=main,
        out_shape=jax.ShapeDtypeStruct(xr.shape, F32),
        scratch_shapes=[pltpu.VMEM((nb, c), F32), pltpu.VMEM((TT, nb, c), F32), pltpu.VMEM((TT, nb, c), F32)],
        compiler_params=_cparams(("arbitrary",)),
        name="lru_rev" if rev else "lru_fwd",
    )(xr, xr, xr, conv_w, conv_b, w, b, lam)


def _win_attn_kernel(q_ref, k_ref, v_ref, sink_ref, o_ref, *, n_ctx, n_lat):
    i = pl.program_id(1)
    nctx_tiles = n_ctx // TQ
    nlb = n_lat // TQ
    grp = D_HEADS // D_KV
    width = grp * TQ
    lane = lax.broadcasted_iota(jnp.int32, (TQ, 128), 1)
    low = lane < D_HD
    srow = lax.broadcasted_iota(jnp.int32, (SINK_ROWS, width), 0)
    vrow = lax.broadcasted_iota(jnp.int32, (SINK_ROWS, 128), 0)
    vlane = lax.broadcasted_iota(jnp.int32, (SINK_ROWS, 128), 1)
    v_sink = jnp.where(jnp.logical_and(vrow == 0, vlane == D_HD), 1.0, 0.0).astype(BF16)

    def stacked_q(g):
        parts = []
        for sl in range(grp // 2):
            slab = q_ref[:, (g * (grp // 2) + sl) * 128:(g * (grp // 2) + sl + 1) * 128]
            zero = jnp.zeros_like(slab)
            parts += [jnp.where(low, slab, zero), jnp.where(low, zero, slab)]
        return jnp.concatenate(parts, axis=0)

    def sink_row(g):
        return jnp.concatenate(
            [jnp.broadcast_to(sink_ref[g * grp + hh:g * grp + hh + 1, 0:1], (1, TQ)) for hh in range(grp)], axis=1)

    def finish(g, o):
        out = o[:, 0:D_HD] / o[:, D_HD:D_HD + 1]
        out = jnp.concatenate([out, jnp.zeros_like(out)], axis=1)
        for sl in range(grp // 2):
            a = out[(2 * sl) * TQ:(2 * sl + 1) * TQ]
            b = pltpu.roll(out[(2 * sl + 1) * TQ:(2 * sl + 2) * TQ], D_HD, 1)
            col = (g * (grp // 2) + sl) * 128
            o_ref[:, col:col + 128] = jnp.where(low, a, b).astype(o_ref.dtype)

    def attend(local):
        groups = range(D_KV)
        qz = [stacked_q(g) for g in groups]
        sk = [sink_row(g) for g in groups]
        sc, sl_, m = [None] * D_KV, [None] * D_KV, [None] * D_KV
        for g in groups:
            sc[g] = _dot_nt(k_ref[0:n_ctx, g * 128:(g + 1) * 128], qz[g])
            m[g] = jnp.maximum(jnp.max(sc[g], axis=0, keepdims=True), sk[g])
            if local is not None:
                start, mask = local
                kl = k_ref[pl.ds(start, 3 * TQ), g * 128:(g + 1) * 128]
                sl_[g] = jnp.where(mask, _dot_nt(kl, qz[g]), NEG_INF)
                m[g] = jnp.maximum(m[g], jnp.max(sl_[g], axis=0, keepdims=True))
        for g in groups:
            e_sink = jnp.where(srow == 0, jnp.exp2(sk[g] - m[g]), 0.0).astype(BF16)
            o = (_dot_tn(jnp.exp2(sc[g] - m[g]).astype(BF16), v_ref[0:n_ctx, g * 128:(g + 1) * 128])
                 + _dot_tn(e_sink, v_sink))
            if local is not None:
                vl = v_ref[pl.ds(local[0], 3 * TQ), g * 128:(g + 1) * 128]
                o = o + _dot_tn(jnp.exp2(sl_[g] - m[g]).astype(BF16), vl)
            finish(g, o)

    @pl.when(i < nctx_tiles)
    def _():
        attend(None)

    @pl.when(i >= nctx_tiles)
    def _():
        il = i - nctx_tiles
        kb = jnp.clip(il - 1, 0, nlb - 3)
        start = pl.multiple_of(n_ctx + kb * TQ, TQ)
        kpos = kb * TQ + lax.broadcasted_iota(jnp.int32, (3 * TQ, width), 0)
        qpos = il * TQ + (lax.broadcasted_iota(jnp.int32, (3 * TQ, width), 1) % TQ)
        mask = jnp.abs(kpos - qpos) <= WINDOW
        attend((start, mask))


def _win_attn(q, k, v, sink, n_ctx):
    nb, t, _ = q.shape
    kv = pl.BlockSpec((None, t, 256), lambda b, i: (b, 0, 0))
    qo = pl.BlockSpec((None, TQ, 512), lambda b, i: (b, i, 0))
    return pl.pallas_call(
        functools.partial(_win_attn_kernel, n_ctx=n_ctx, n_lat=t - n_ctx),
        grid=(nb, t // TQ),
        in_specs=[qo, kv, kv, _full(sink.shape)],
        out_specs=qo,
        out_shape=jax.ShapeDtypeStruct((nb, t, D_HEADS * D_HD), BF16),
        compiler_params=_cparams(("parallel", "arbitrary")),
        name="win_attn",
    )(q, k, v, sink)


def _odd_out_kernel(x_ref, mod_ref, hf_ref, hr_ref, gate_ref, od_ref, w_ref, o_ref):
    d = D_MODEL
    yc = (hf_ref[...] + hr_ref[...]) * _gelu_tanh(gate_ref[...])
    y = _dot(yc.astype(BF16), w_ref[0:512, :]) + _dot(od_ref[...], w_ref[512:1024, :])
    o_ref[...] = x_ref[...] + mod_ref[:, 2 * d:3 * d] * y


def _odd_out(x, mod, hf, hr, gate, od, w):
    nb, t, d = x.shape
    tile = lambda ww: pl.BlockSpec((None, TM, ww), lambda b, j: (b, j, 0))
    return pl.pallas_call(
        _odd_out_kernel,
        grid=(nb, t // TM),
        in_specs=[tile(d), _mod_spec(nb), tile(C_WIDTH), tile(C_WIDTH), tile(C_WIDTH), tile(512), _full(w.shape)],
        out_specs=tile(d),
        out_shape=jax.ShapeDtypeStruct(x.shape, F32),
        compiler_params=_cparams(("parallel", "parallel")),
        name="odd_out",
    )(x, mod, hf, hr, gate, od, w)


def _rope_tables(n_ctx, n_lat):
    pos = np.arange(n_lat)
    inv = ROPE_THETA ** (-np.arange(0, ROT_AXIS, 2, dtype=np.float64) / ROT_AXIS)
    ang_r = (pos // GRID_W)[:, None] * inv
    ang_c = (pos % GRID_W)[:, None] * inv
    lane = np.arange(128) % 64
    seg, f = lane // 16, lane % 16
    ang = np.where(seg[None, :] < 2, ang_r[:, f], ang_c[:, f])
    c = np.cos(ang)
    s = np.sin(ang)
    sa = np.where((seg % 2 == 0)[None, :], -s, 0.0)
    sb = np.where((seg % 2 == 1)[None, :], s, 0.0)
    pad = lambda a, fill: np.concatenate([np.full((n_ctx, 128), fill), a], axis=0).astype(np.float32)
    return jnp.asarray(pad(c, 1.0)), jnp.asarray(pad(sa, 0.0)), jnp.asarray(pad(sb, 0.0))


def _block_diag(w):
    eye = jnp.eye(C_BLOCKS, dtype=w.dtype)
    return jnp.einsum('hij,hg->higj', w, eye).reshape(C_WIDTH, C_WIDTH)


def kernel(x, c, ctx, c_ctx, w_ada, b_ada, norm_mix, norm_ffn, ffn_w_up, ffn_conv, ffn_w_down, final_norm,
           ev_w_in, ev_w_out, diff_lambda, diff_subln, gdn_conv, gdn_a_log, gdn_dt_bias, gdn_norm,
           od_w_in, od_w_out, lru_conv, lru_conv_b, lru_wa, lru_ba, lru_wx, lru_bx, lru_lambda, swa_sink):
    nb, n_lat, d = x.shape
    n_ctx = ctx.shape[1]
    depth = w_ada.shape[0]
    assert d == D_MODEL and n_ctx == TM and n_lat % TM == 0 and n_lat // TQ >= 3 and nb < MOD_ROWS
    assert nb % GDN_NB == 0
    t = n_ctx + n_lat

    xa = jnp.concatenate([ctx, x], axis=1)
    c_all = jnp.zeros((MOD_ROWS, d), F32).at[0:nb].set(c).at[nb].set(c_ctx)
    mod_all = _modulation(c_all, w_ada, b_ada).reshape(depth, MOD_ROWS, 1, N_MOD * d)
    tabs = _rope_tables(n_ctx, n_lat)
    row = lambda v: v.reshape(1, -1).astype(F32)

    for layer in range(depth):
        jx = layer // 2
        mod = mod_all[layer]
        mod2 = jnp.stack([jnp.broadcast_to(mod[nb], (nb, N_MOD * d)), mod[0:nb, 0]], axis=0)
        if layer % 2 == 0:
            lam_init = 0.8 - 0.6 * math.exp(-0.3 * layer)
            w_in = ev_w_in[jx]
            wa = w_in[:, 0:1536].astype(BF16)
            wg = w_in[:, 1536:3072].astype(BF16)
            wgate = w_in[:, 3072:3584].astype(BF16)
            wba = jnp.pad(w_in[:, 3584:3600], ((0, 0), (0, 112))).astype(BF16)
            pad16 = lambda v: jnp.pad(v.reshape(1, 8).astype(F32), ((0, 0), (8, 112)))
            q, k, v, qkv, gate, bg = _even_in(xa, mod, row(norm_mix[layer]), tabs, wa, wg, wgate, wba,
                                              gdn_conv[jx].astype(F32), pad16(gdn_a_log[jx]), pad16(gdn_dt_bias[jx]))
            ya = _diff_attn(q, k, v, diff_lambda[jx].astype(F32), diff_subln[jx].astype(F32).reshape(-1, 1),
                            lam_init, n_ctx)
            of = _gdn(qkv, bg, rev=False)
            orv = _gdn(qkv, bg, rev=True)
            xa = _even_out(xa, mod, ya, of, orv, gate, row(gdn_norm[jx]), ev_w_out[jx].astype(BF16))
        else:
            w_in = od_w_in[jx]
            wr = w_in[:, 0:1024].astype(BF16)
            wq = w_in[:, 1024:1536].astype(BF16)
            dup = lambda w: jnp.concatenate([w[:, 0:64], w[:, 0:64], w[:, 64:128], w[:, 64:128]], axis=1)
            zpad = lambda w: jnp.concatenate([w[:, 0:64], jnp.zeros_like(w[:, 0:64]), w[:, 64:128], jnp.zeros_like(w[:, 0:64])], axis=1)
            wkv = jnp.concatenate([dup(w_in[:, 1536:1664]), zpad(w_in[:, 1664:1792])], axis=1).astype(BF16)
            xr, gate, q, k, v = _odd_in(xa, mod, row(norm_mix[layer]), tabs, wr, wq, wkv)
            hs = []
            for dd in range(2):
                wbig = jnp.concatenate([_block_diag(lru_wa[jx, dd]), _block_diag(lru_wx[jx, dd])], axis=1)
                bbig = jnp.concatenate([lru_ba[jx, dd], lru_bx[jx, dd]]).reshape(1, -1).astype(F32)
                hs.append(_lru(xr, lru_conv[jx].astype(F32), row(lru_conv_b[jx]), wbig.astype(BF16), bbig,
                               row(lru_lambda[jx, dd]), n_ctx, rev=(dd == 1)))
            sink = jnp.broadcast_to(swa_sink[jx].astype(F32)[:, None] * LOG2E, (D_HEADS, 128))
            od = _win_attn(q, k, v, sink, n_ctx)
            xa = _odd_out(xa, mod, hs[0], hs[1], gate, od, od_w_out[jx].astype(BF16))
        xa = _ffn(xa, mod2, row(norm_ffn[layer]), ffn_w_up[layer].astype(BF16), ffn_conv[layer].astype(F32),
                  ffn_w_down[layer].astype(BF16), n_ctx, row(final_norm) if layer == depth - 1 else None)
    return xa
```

```python
import functools
import math

import jax
import jax.numpy as jnp
import numpy as np
from jax import lax
from jax.experimental import pallas as pl
from jax.experimental.pallas import tpu as pltpu

F32 = jnp.float32
BF16 = jnp.bfloat16
HIGHEST = lax.Precision.HIGHEST

D_MODEL = 1024
GRID_W = 64
EPS = 1e-6
NEG_INF = -1e30
N_MOD = 6
ROPE_THETA = 10000.0
ROT_AXIS = 32
A_HEADS = 4
A_HD = 64
A_VD = 128
B_HEADS = 4
B_DK = 128
B_W = 512
B_CONV = 4
B_CHUNK = 64
C_WIDTH = 512
C_BLOCKS = 8
C_BD = 64
C_CONV = 4
C_POW = 8.0
D_HEADS = 8
D_KV = 2
D_HD = 64
WINDOW = 128
FFN = 2816
FFN_CONV = 3

TM = 256
TQ = 128
TQA = 256
KB = 256
ONES_ROWS = 16
SINK_ROWS = 16
LOG2E = math.log2(math.e)
TT = 64
GDN_NB = 4
WIN_NB = 2
HALO = 8
FC = 256
TTF = 64
MOD_ROWS = 16
VMEM_LIMIT = 56 * 1024 * 1024


def _cparams(sem):
    return pltpu.CompilerParams(dimension_semantics=sem, vmem_limit_bytes=VMEM_LIMIT)


def _sigmoid(x):
    return 1.0 / (1.0 + jnp.exp(-x))


def _sigmoid_t(x):
    return 0.5 * (1.0 + jnp.tanh(0.5 * x))


def _silu(x):
    return x * _sigmoid(x)


def _softplus(x):
    return jnp.maximum(x, 0.0) + jnp.log(1.0 + jnp.exp(-jnp.abs(x)))


def _gelu_tanh(x):
    return 0.5 * x * (1.0 + jnp.tanh(math.sqrt(2.0 / math.pi) * (x + 0.044715 * (x * x * x))))


def _dot(a, b):
    return jnp.dot(a, b, preferred_element_type=F32)


def _dot_hi(a, b):
    return jnp.dot(a, b, preferred_element_type=F32, precision=HIGHEST)


def _dot3(a, b):
    ah = a.astype(BF16)
    al = (a - ah.astype(F32)).astype(BF16)
    bh = b.astype(BF16)
    bl = (b - bh.astype(F32)).astype(BF16)
    return _dot(ah, bh) + (_dot(ah, bl) + _dot(al, bh))


def _dot_nt(a, b):
    return lax.dot_general(a, b, (((1,), (1,)), ((), ())), preferred_element_type=F32)


def _dot_tn(a, b):
    return lax.dot_general(a, b, (((0,), (0,)), ((), ())), preferred_element_type=F32)


def _rms(x, g):
    return x * lax.rsqrt(jnp.mean(x * x, axis=-1, keepdims=True) + EPS) * g


def _modulate(x, g, shift, scale):
    return _rms(x, g) * (1.0 + scale) + shift


def _rope128(z, c, sa, sb):
    return z * c + pltpu.roll(z, 112, 1) * sa + pltpu.roll(z, 16, 1) * sb


def _rope(z, c, sa, sb):
    n = z.shape[1] // 128
    return jnp.concatenate([_rope128(z[:, i * 128:(i + 1) * 128], c, sa, sb) for i in range(n)], axis=1)


def _mod_kernel(s_ref, w_ref, b_ref, o_ref):
    s = _silu(s_ref[...])
    o_ref[...] = _dot_hi(s, w_ref[...]) + b_ref[...]


def _modulation(c_all, w_ada, b_ada):
    depth, d, n = w_ada.shape
    tn = 1536
    return pl.pallas_call(
        _mod_kernel,
        grid=(depth, n // tn),
        in_specs=[pl.BlockSpec((MOD_ROWS, d), lambda l, j: (0, 0)),
                  pl.BlockSpec((None, d, tn), lambda l, j: (l, 0, j)),
                  pl.BlockSpec((None, 1, tn), lambda l, j: (l, 0, j))],
        out_specs=pl.BlockSpec((None, MOD_ROWS, tn), lambda l, j: (l, 0, j)),
        out_shape=jax.ShapeDtypeStruct((depth, MOD_ROWS, n), F32),
        compiler_params=_cparams(("arbitrary", "arbitrary")),
        name="modulation",
    )(c_all, w_ada, b_ada.reshape(depth, 1, n))


def _mod_spec(nb):
    return pl.BlockSpec((None, 1, N_MOD * D_MODEL), lambda b, j: (jnp.where(j == 0, nb, b), 0, 0))


def _full(shape):
    nd = len(shape)
    return pl.BlockSpec(shape, lambda *_: (0,) * nd)


def _even_in_kernel(x_ref, prev_ref, next_ref, mod_ref, g_ref, c_ref, sa_ref, sb_ref, wa_ref, wg_ref, wgate_ref,
                    wba_ref, cw_ref, alog_ref, dtb_ref, q_ref, k_ref, v_ref, qkv_ref, gate_ref, bg_ref, ext_ref,
                    *, nt):
    d = D_MODEL
    j = pl.program_id(1)
    mod = mod_ref[...]
    g, shift, scale = g_ref[...], mod[:, 0:d], mod[:, d:2 * d]
    uf = _modulate(x_ref[...], g, shift, scale)
    u = uf.astype(BF16)
    lflag = (j >= 2).astype(F32)
    rflag = jnp.logical_and(j >= 1, j <= nt - 2).astype(F32)
    halo = jnp.concatenate([_modulate(prev_ref[...], g, shift, scale) * lflag,
                            _modulate(next_ref[...], g, shift, scale) * rflag], axis=0).astype(BF16)
    zh = _dot(halo, wg_ref[...])
    ext_ref[0:HALO, :] = zh[0:HALO]
    ext_ref[HALO:HALO + TM, :] = _dot(u, wg_ref[...])
    ext_ref[HALO + TM:2 * HALO + TM, :] = zh[HALO:2 * HALO]
    acc = None
    for kk in range(B_CONV):
        term = cw_ref[kk:kk + 1, :] * ext_ref[pl.ds(HALO - 2 + kk, TM), :]
        acc = term if acc is None else acc + term
    act = _silu(acc)

    def l2n(zz):
        return zz * lax.rsqrt(jnp.sum(zz * zz, axis=-1, keepdims=True) + EPS)

    for h in range(B_HEADS):
        qkv_ref[:, h * 128:(h + 1) * 128] = l2n(act[:, h * 128:(h + 1) * 128]) * (B_DK ** -0.5)
        qkv_ref[:, B_W + h * 128:B_W + (h + 1) * 128] = l2n(act[:, B_W + h * 128:B_W + (h + 1) * 128])
    qkv_ref[:, 2 * B_W:3 * B_W] = act[:, 2 * B_W:3 * B_W]

    c, sa, sb = c_ref[...], sa_ref[...], sb_ref[...]
    q = _dot(u, wa_ref[:, 0:512])
    q_ref[...] = (_rope(q, c, sa, sb) * (A_HD ** -0.5 * LOG2E)).astype(BF16)
    k = _dot(u, wa_ref[:, 512:1024])
    k_ref[...] = _rope(k, c, sa, sb).astype(BF16)
    v_ref[...] = _dot(u, wa_ref[:, 1024:1536]).astype(BF16)
    gate_ref[...] = _dot(u, wgate_ref[...]).astype(gate_ref.dtype)
    z = _dot(u, wba_ref[...])
    lane = lax.broadcasted_iota(jnp.int32, z.shape, 1)
    beta = _sigmoid(z)
    gdec = -jnp.exp(alog_ref[...]) * _softplus(z + dtb_ref[...])
    bg_ref[...] = jnp.where(lane < 2 * B_HEADS, beta, jnp.where(lane < 4 * B_HEADS, gdec, 0.0))


def _even_in(x, mod, g, tabs, wa, wg, wgate, wba, cw, alog, dtb):
    nb, t, d = x.shape
    nt = t // TM
    hb = TM // HALO
    tile = lambda w: pl.BlockSpec((None, TM, w), lambda b, j: (b, j, 0))
    prev = pl.BlockSpec((None, HALO, d), lambda b, j: (b, jnp.maximum(j * hb - 1, 0), 0))
    nxt = pl.BlockSpec((None, HALO, d), lambda b, j: (b, jnp.minimum((j + 1) * hb, t // HALO - 1), 0))
    tab = pl.BlockSpec((TM, 128), lambda b, j: (j, 0))
    outs = [jax.ShapeDtypeStruct((nb, t, 512), BF16)] * 3 + [
        jax.ShapeDtypeStruct((nb, t, 1536), F32), jax.ShapeDtypeStruct((nb, t, 512), BF16),
        jax.ShapeDtypeStruct((nb, t, 128), F32)]
    return pl.pallas_call(
        functools.partial(_even_in_kernel, nt=nt),
        grid=(nb, nt),
        in_specs=[tile(d), prev, nxt, _mod_spec(nb), _full((1, d)), tab, tab, tab, _full(wa.shape), _full(wg.shape),
                  _full(wgate.shape), _full(wba.shape), _full(cw.shape), _full((1, 128)), _full((1, 128))],
        out_specs=[tile(512), tile(512), tile(512), tile(1536), tile(512), tile(128)],
        out_shape=outs,
        scratch_shapes=[pltpu.VMEM((TM + 2 * HALO, 1536), F32)],
        compiler_params=_cparams(("parallel", "parallel")),
        name="even_in",
    )(x, x, x, mod, g, *tabs, wa, wg, wgate, wba, cw, alog, dtb)


def _diff_attn_kernel(q_ref, k_ref, v_ref, lv_ref, g_ref, o_ref, vt_ref, sa_ref, sb_ref, ma_ref, mb_ref,
                      *, lam_init, n_ctx):
    i = pl.program_id(2)
    t = k_ref.shape[0]
    nblk = t // KB

    @pl.when(i == 0)
    def _():
        vt_ref[0:A_VD, :] = v_ref[...].astype(F32).T.astype(BF16)
        orow = lax.broadcasted_iota(jnp.int32, (ONES_ROWS, t), 0)
        vt_ref[A_VD:A_VD + ONES_ROWS, :] = jnp.where(orow == 0, 1.0, 0.0).astype(BF16)
        sb_ref[...] = jnp.zeros_like(sb_ref)
        mb_ref[...] = jnp.zeros_like(mb_ref)

    def step(s_new, m_new, s_old, m_old):
        lv = lv_ref[...]
        lam = (jnp.exp(jnp.sum(lv[0:1] * lv[1:2], axis=-1, keepdims=True))
               - jnp.exp(jnp.sum(lv[2:3] * lv[3:4], axis=-1, keepdims=True)) + lam_init)
        m = jnp.max(m_old[...], axis=0, keepdims=True)
        is_ctx = jnp.minimum(i, pl.num_programs(2) - 2) < n_ctx // TQA
        q = q_ref[...]
        lane = lax.broadcasted_iota(jnp.int32, q.shape, 1)
        zero = jnp.zeros_like(q)
        qz = jnp.concatenate([jnp.where(lane < A_HD, q, zero), jnp.where(lane >= A_HD, q, zero)], axis=0)
        mrun = None
        oe = None
        for kb in range(nblk):
            rows = slice(kb * KB, (kb + 1) * KB)
            e = jnp.exp2(s_old[rows, :] - m).astype(BF16)
            part = _dot(vt_ref[:, rows], e)
            oe = part if oe is None else oe + part
            sblk = _dot_nt(k_ref[rows, :], qz)
            if kb >= n_ctx // KB:
                sblk = jnp.where(is_ctx, NEG_INF, sblk)
            s_new[rows, :] = sblk
            part = jnp.max(sblk.reshape(KB // 8, 8, 2 * TQA), axis=0)
            mrun = part if mrun is None else jnp.maximum(mrun, part)
        m_new[...] = mrun
        on = oe[0:A_VD] / oe[A_VD:A_VD + 1]
        od = on[:, 0:TQA] - lam * on[:, TQA:2 * TQA]
        y = od * lax.rsqrt(jnp.mean(od * od, axis=0, keepdims=True) + EPS) * (g_ref[...] * (1.0 - lam_init))
        o_ref[...] = y.T.astype(o_ref.dtype)

    @pl.when(i % 2 == 0)
    def _():
        step(sa_ref, ma_ref, sb_ref, mb_ref)

    @pl.when(i % 2 == 1)
    def _():
        step(sb_ref, mb_ref, sa_ref, ma_ref)


def _diff_attn(q, k, v, lam_vec, subln, lam_init, n_ctx):
    nb, t, _ = q.shape
    nq = t // TQA
    kv = pl.BlockSpec((None, t, 128), lambda b, h, i: (b, 0, h))
    qin = pl.BlockSpec((None, TQA, 128), lambda b, h, i: (b, jnp.minimum(i, nq - 1), h))
    out = pl.BlockSpec((None, TQA, 128), lambda b, h, i: (b, jnp.maximum(i - 1, 0), h))
    return pl.pallas_call(
        functools.partial(_diff_attn_kernel, lam_init=lam_init, n_ctx=n_ctx),
        grid=(nb, A_HEADS, nq + 1),
        in_specs=[qin, kv, kv, _full((4, A_HD)), _full((A_VD, 1))],
        out_specs=out,
        out_shape=jax.ShapeDtypeStruct((nb, t, A_HEADS * A_VD), BF16),
        scratch_shapes=[pltpu.VMEM((A_VD + ONES_ROWS, t), BF16), pltpu.VMEM((t, 2 * TQA), F32),
                        pltpu.VMEM((t, 2 * TQA), F32), pltpu.VMEM((8, 2 * TQA), F32), pltpu.VMEM((8, 2 * TQA), F32)],
        compiler_params=_cparams(("parallel", "parallel", "arbitrary")),
        name="diff_attn",
    )(q, k, v, lam_vec, subln)


def _gdn_tile_index(j, nt, rev):
    return jnp.where(j == 0, 0, nt - j) if rev else j


def _gdn_kernel(qkv_ref, bg_ref, o_ref, s_ref, *, rev, nt):
    j = pl.program_id(1)
    dirn = 1 if rev else 0
    nch = TM // B_CHUNK

    @pl.when(j == 0)
    def _():
        s_ref[...] = jnp.zeros_like(s_ref)

    ri = lax.broadcasted_iota(jnp.int32, (TM, TM), 0)
    ci = lax.broadcasted_iota(jnp.int32, (TM, TM), 1)
    same = (ri // B_CHUNK) == (ci // B_CHUNK)
    incl = jnp.logical_and(same, (ri <= ci) if rev else (ri >= ci))
    strict = jnp.logical_and(same, (ri < ci) if rev else (ri > ci))
    eye = (ri == ci).astype(F32)
    inclb = incl.astype(BF16)
    rchunk = lax.broadcasted_iota(jnp.int32, (TM, B_DK), 0) // B_CHUNK

    def by_chunk(z):
        return jnp.concatenate([jnp.where(rchunk == c, z, 0.0) for c in range(nch)], axis=1).astype(BF16)

    chains = [(bi, h) for bi in range(GDN_NB) for h in range(B_HEADS)]
    nchain = len(chains)
    lasts = [c * B_CHUNK if rev else (c + 1) * B_CHUNK - 1 for c in range(nch)]
    q, k, v, beta, gcol, eg, qkm, p, pw = ([None] * nchain for _ in range(9))
    for bi in range(GDN_NB):
        bg = bg_ref[bi]
        b1 = bg.astype(BF16)
        r1 = bg - b1.astype(F32)
        b2 = r1.astype(BF16)
        b3 = (r1 - b2.astype(F32)).astype(BF16)
        gcum = _dot(inclb, b1) + (_dot(inclb, b2) + _dot(inclb, b3))
        gcum_t = gcum.T
        for h in range(B_HEADS):
            n = bi * B_HEADS + h
            q[n] = qkv_ref[bi, :, h * 128:(h + 1) * 128]
            k[n] = qkv_ref[bi, :, B_W + h * 128:B_W + (h + 1) * 128]
            v[n] = qkv_ref[bi, :, 2 * B_W + h * 128:2 * B_W + (h + 1) * 128]
            cb = dirn * B_HEADS + h
            cg = 2 * B_HEADS + cb
            beta[n] = bg[:, cb:cb + 1]
            gcol[n] = gcum[:, cg:cg + 1]
            grow = gcum_t[cg:cg + 1, :]
            eg[n] = jnp.exp(gcol[n])
            decay = jnp.where(incl, jnp.exp(jnp.where(incl, gcol[n] - grow, 0.0)), 0.0)
            kb = k[n].astype(BF16)
            qkm[n] = _dot_nt(q[n].astype(BF16), kb) * decay
            pw[n] = jnp.where(strict, beta[n] * _dot_nt(kb, kb) * decay, 0.0)

    xr = ri ^ ci
    for lvl in range(6):
        joins = (xr >> lvl) == 1
        for n in range(nchain):
            l_s = jnp.where(joins, pw[n], 0.0)
            if lvl == 0:
                p[n] = eye - l_s
            else:
                pb = p[n].astype(BF16)
                p[n] = p[n] - _dot(pb, _dot(l_s.astype(BF16), pb).astype(BF16))

    qku, qeff, mn = ([None] * nchain for _ in range(3))
    for n in range(nchain):
        rhs = jnp.concatenate([beta[n] * v[n], (beta[n] * eg[n]) * k[n]], axis=1)
        uw = _dot(p[n].astype(BF16), rhs.astype(BF16))
        qkuw = _dot(qkm[n].astype(BF16), uw.astype(BF16))
        qku[n] = qkuw[:, 0:128]
        qeff[n] = (q[n] * eg[n] - qkuw[:, 128:256]).astype(BF16)
        glast = jnp.concatenate(
            [jnp.broadcast_to(gcol[n][r:r + 1, :], (B_CHUNK, 1)) for r in lasts], axis=0)
        kdec = (k[n] * jnp.exp(glast - gcol[n])).astype(BF16)
        mn[n] = _dot_tn(kdec, jnp.concatenate([by_chunk(uw[:, 128:256]), by_chunk(uw[:, 0:128])], axis=1))

    for step in range(nch):
        c = nch - 1 - step if rev else step
        r0, r1 = c * B_CHUNK, (c + 1) * B_CHUNK
        for n, (bi, h) in enumerate(chains):
            sh = s_ref[n]
            shb = sh.astype(BF16)
            o_ref[bi, r0:r1, h * 128:(h + 1) * 128] = (_dot(qeff[n][r0:r1], shb) + qku[n][r0:r1]).astype(o_ref.dtype)
            mc = mn[n][:, c * 128:(c + 1) * 128].astype(BF16)
            nc = mn[n][:, (nch + c) * 128:(nch + c + 1) * 128]
            gl = jnp.exp(gcol[n][lasts[c]:lasts[c] + 1, :])
            s_ref[n] = sh * gl - _dot(mc, shb) + nc


def _gdn(qkv, bg, rev):
    nb, t, w = qkv.shape
    nt = t // TM
    main = lambda ww: pl.BlockSpec((GDN_NB, TM, ww), lambda b, j: (b, _gdn_tile_index(j, nt, rev), 0))
    return pl.pallas_call(
        functools.partial(_gdn_kernel, rev=rev, nt=nt),
        grid=(nb // GDN_NB, nt),
        in_specs=[main(w), main(128)],
        out_specs=main(B_W),
        out_shape=jax.ShapeDtypeStruct((nb, t, B_W), BF16),
        scratch_shapes=[pltpu.VMEM((GDN_NB * B_HEADS, B_DK, B_DK), F32)],
        compiler_params=_cparams(("parallel", "arbitrary")),
        name="gdn_rev" if rev else "gdn_fwd",
    )(qkv, bg)


def _even_out_kernel(x_ref, mod_ref, ya_ref, of_ref, or_ref, gate_ref, og_ref, w_ref, o_ref):
    d = D_MODEL
    ob = of_ref[...].astype(F32) + or_ref[...].astype(F32)
    gate = gate_ref[...].astype(F32)
    yb = jnp.concatenate(
        [_rms(ob[:, h * 128:(h + 1) * 128], og_ref[...]) * _silu(gate[:, h * 128:(h + 1) * 128])
         for h in range(B_HEADS)], axis=1)
    y = _dot(ya_ref[...], w_ref[0:512, :]) + _dot(yb.astype(BF16), w_ref[512:1024, :])
    o_ref[...] = x_ref[...] + mod_ref[:, 2 * d:3 * d] * y


def _even_out(x, mod, ya, of, orv, gate, og, w):
    nb, t, d = x.shape
    tile = lambda ww: pl.BlockSpec((None, TM, ww), lambda b, j: (b, j, 0))
    return pl.pallas_call(
        _even_out_kernel,
        grid=(nb, t // TM),
        in_specs=[tile(d), _mod_spec(nb), tile(512), tile(512), tile(512), tile(512), _full((1, 128)),
                  _full(w.shape)],
        out_specs=tile(d),
        out_shape=jax.ShapeDtypeStruct(x.shape, F32),
        compiler_params=_cparams(("parallel", "parallel")),
        name="even_out",
    )(x, mod, ya, of, orv, gate, og, w)


def _ffn_kernel(x_ref, prev_ref, next_ref, mod_ref, g_ref, wup_ref, cw_ref, wdn_ref, fg_ref, o_ref, act_ref,
                *, ntt, nct, final):
    d = D_MODEL
    nb = x_ref.shape[0]
    j = pl.program_id(0)

    def body():
        mod = mod_ref[...]
        shift, scale, gate = mod[:, 3 * d:4 * d], mod[:, 4 * d:5 * d], mod[:, 5 * d:6 * d]
        g = g_ref[...]
        lflag = jnp.logical_and(j != 0, j != nct).astype(F32)
        rflag = jnp.logical_and(j != nct - 1, j != ntt - 1).astype(F32)
        x3 = pltpu.einshape("btd->tbd", x_ref[...])
        xp = pltpu.einshape("btd->tbd", prev_ref[...])[HALO - 1]
        xn = pltpu.einshape("btd->tbd", next_ref[...])[0]
        u3 = jnp.concatenate([(_modulate(xp, g, shift, scale) * lflag)[None],
                              _modulate(x3, g, shift, scale),
                              (_modulate(xn, g, shift, scale) * rflag)[None]], axis=0)
        u = u3.reshape((TTF + 2) * nb, d).astype(BF16)
        rows = TTF * nb
        for c in range(FFN // FC):
            c0 = c * FC
            hg = _dot(u, wup_ref[:, c0:c0 + FC])
            hv = _dot(u, wup_ref[:, FFN + c0:FFN + c0 + FC])
            cg = None
            cv = None
            for kk in range(FFN_CONV):
                tg = cw_ref[kk:kk + 1, c0:c0 + FC] * hg[kk * nb:kk * nb + rows]
                tv = cw_ref[kk:kk + 1, FFN + c0:FFN + c0 + FC] * hv[kk * nb:kk * nb + rows]
                cg = tg if cg is None else cg + tg
                cv = tv if cv is None else cv + tv
            act_ref[:, c0:c0 + FC] = (_silu(cg) * cv).astype(BF16)
        out3 = x3 + gate * _dot(act_ref[...], wdn_ref[...]).reshape(TTF, nb, d)
        if final:
            out3 = _rms(out3, fg_ref[...])
        o_ref[...] = pltpu.einshape("tbd->btd", out3)

    if final:
        pl.when(j >= nct)(body)
    else:
        body()


def _ffn(x, mod2, g, wup, cw, wdn, n_ctx, final_g=None):
    nb, t, d = x.shape
    ntt, nct = t // TTF, n_ctx // TTF
    hb = TTF // HALO
    tile = pl.BlockSpec((nb, TTF, d), lambda j: (0, j, 0))
    prev = pl.BlockSpec((nb, HALO, d), lambda j: (0, jnp.maximum(j * hb - 1, 0), 0))
    nxt = pl.BlockSpec((nb, HALO, d), lambda j: (0, jnp.minimum((j + 1) * hb, t // HALO - 1), 0))
    modspec = pl.BlockSpec((None, nb, N_MOD * d), lambda j: (jnp.where(j < nct, 0, 1), 0, 0))
    resident = lambda shape: pl.BlockSpec(shape, lambda j: (0, 0), pipeline_mode=pl.Buffered(1))
    final = final_g is not None
    t_out = t - n_ctx if final else t
    out_tile = pl.BlockSpec((nb, TTF, d), lambda j: (0, jnp.maximum(j - nct, 0), 0)) if final else tile
    out = pl.pallas_call(
        functools.partial(_ffn_kernel, ntt=ntt, nct=nct, final=final),
        grid=(ntt,),
        in_specs=[tile, prev, nxt, modspec, _full((1, d)), resident(wup.shape), _full(cw.shape),
                  resident(wdn.shape), _full((1, d))],
        out_specs=out_tile,
        out_shape=jax.ShapeDtypeStruct((nb, t_out, d), F32),
        scratch_shapes=[pltpu.VMEM((TTF * nb, FFN), BF16)],
        compiler_params=_cparams(("arbitrary" if final else "parallel",)),
        name="ffn",
    )(x, x, x, mod2, g, wup, cw, wdn, final_g if final else g)
    return out


def _odd_in_kernel(x_ref, mod_ref, g_ref, c_ref, sa_ref, sb_ref, wr_ref, wq_ref, wkv_ref,
                   xr_ref, gate_ref, q_ref, k_ref, v_ref):
    d = D_MODEL
    mod = mod_ref[...]
    u = _modulate(x_ref[...], g_ref[...], mod[:, 0:d], mod[:, d:2 * d]).astype(BF16)
    c, sa, sb = c_ref[...], sa_ref[...], sb_ref[...]
    xr_ref[...] = _dot(u, wr_ref[:, 0:C_WIDTH])
    gate_ref[...] = _dot(u, wr_ref[:, C_WIDTH:2 * C_WIDTH])
    q_ref[...] = (_rope(_dot(u, wq_ref[...]), c, sa, sb) * (D_HD ** -0.5 * LOG2E)).astype(BF16)
    k_ref[...] = _rope(_dot(u, wkv_ref[:, 0:256]), c, sa, sb).astype(BF16)
    v = _dot(u, wkv_ref[:, 256:512])
    vlane = lax.broadcasted_iota(jnp.int32, v.shape, 1)
    v_ref[...] = jnp.where(vlane % 128 == D_HD, 1.0, v).astype(BF16)


def _odd_in(x, mod, g, tabs, wr, wq, wkv):
    nb, t, d = x.shape
    tile = lambda w: pl.BlockSpec((None, TM, w), lambda b, j: (b, j, 0))
    tab = pl.BlockSpec((TM, 128), lambda b, j: (j, 0))
    outs = [jax.ShapeDtypeStruct((nb, t, C_WIDTH), F32)] * 2 + [
        jax.ShapeDtypeStruct((nb, t, 512), BF16), jax.ShapeDtypeStruct((nb, t, 256), BF16),
        jax.ShapeDtypeStruct((nb, t, 256), BF16)]
    return pl.pallas_call(
        _odd_in_kernel,
        grid=(nb, t // TM),
        in_specs=[tile(d), _mod_spec(nb), _full((1, d)), tab, tab, tab, _full(wr.shape), _full(wq.shape),
                  _full(wkv.shape)],
        out_specs=[tile(C_WIDTH), tile(C_WIDTH), tile(512), tile(256), tile(256)],
        out_shape=outs,
        compiler_params=_cparams(("parallel", "parallel")),
        name="odd_in",
    )(x, mod, g, *tabs, wr, wq, wkv)


def _lru_tile_index(j, ntt, nct, rev):
    return jnp.where(j < nct, nct - 1 - j, ntt + nct - 1 - j) if rev else j


def _lru_kernel(x_ref, prev_ref, next_ref, cw_ref, cb_ref, w_ref, b_ref, lam_ref, o_ref,
                h_ref, a_ref, bc_ref, *, rev, ntt, nct):
    j = pl.program_id(0)
    jj = _lru_tile_index(j, ntt, nct, rev)
    nb = x_ref.shape[0]
    tmajor = lambda ref: pltpu.einshape("btc->tbc", ref[...])

    @pl.when(j == 0)
    def _():
        h_ref[...] = jnp.zeros_like(h_ref)

    lflag = jnp.logical_and(jj != 0, jj != nct).astype(F32)
    rflag = jnp.logical_and(jj != nct - 1, jj != ntt - 1).astype(F32)
    ext = jnp.concatenate([tmajor(prev_ref)[HALO - 2:HALO] * lflag, tmajor(x_ref),
                           tmajor(next_ref)[0:1] * rflag], axis=0)
    xc = cb_ref[...]
    for kk in range(C_CONV):
        xc = xc + cw_ref[kk:kk + 1, :] * ext[kk:kk + TT]
    xc2 = xc.reshape(TT * nb, C_WIDTH)
    z = _dot(xc2.astype(BF16), w_ref[...]) + b_ref[...]
    r = _sigmoid_t(z[:, 0:C_WIDTH])
    gi = _sigmoid_t(z[:, C_WIDTH:2 * C_WIDTH])
    log_a = -C_POW * r * _softplus(-lam_ref[...])
    a = jnp.exp(log_a)
    bc = jnp.sqrt(1.0 - a * a) * (gi * xc2)
    a_ref[...] = a.reshape(TT, nb, C_WIDTH)
    bc_ref[...] = bc.reshape(TT, nb, C_WIDTH)

    def body(s, h):
        t = TT - 1 - s if rev else s
        h = a_ref[t] * h + bc_ref[t]
        a_ref[t] = h
        return h

    h_ref[...] = lax.fori_loop(0, TT, body, h_ref[...], unroll=8)
    o_ref[...] = pltpu.einshape("tbc->btc", a_ref[...])


def _lru(xr, conv_w, conv_b, w, b, lam, n_ctx, rev):
    nb, t, c = xr.shape
    ntt, nct = t // TT, n_ctx // TT
    hb = TT // HALO
    idx = lambda j: _lru_tile_index(j, ntt, nct, rev)
    main = pl.BlockSpec((nb, TT, c), lambda j: (0, idx(j), 0))
    prev = pl.BlockSpec((nb, HALO, c), lambda j: (0, jnp.maximum(idx(j) * hb - 1, 0), 0))
    nxt = pl.BlockSpec((nb, HALO, c), lambda j: (0, jnp.minimum((idx(j) + 1) * hb, t // HALO - 1), 0))
    return pl.pallas_call(
        functools.partial(_lru_kernel, rev=rev, ntt=ntt, nct=nct),
        grid=(ntt,),
        in_specs=[main, prev, nxt, _full(conv_w.shape), _full(conv_b.shape), _full(w.shape), _full(b.shape),
                  _full(lam.shape)],
        out_specs=main,
        out_shape=jax.ShapeDtypeStruct(xr.shape, F32),
        scratch_shapes=[pltpu.VMEM((nb, c), F32), pltpu.VMEM((TT, nb, c), F32), pltpu.VMEM((TT, nb, c), F32)],
        compiler_params=_cparams(("arbitrary",)),
        name="lru_rev" if rev else "lru_fwd",
    )(xr, xr, xr, conv_w, conv_b, w, b, lam)


def _win_attn_kernel(q_ref, k_ref, v_ref, sink_ref, o_ref, *, n_ctx, n_lat):
    i = pl.program_id(1)
    nctx_tiles = n_ctx // TQ
    nlb = n_lat // TQ
    grp = D_HEADS // D_KV
    width = grp * TQ
    lane = lax.broadcasted_iota(jnp.int32, (TQ, 128), 1)
    low = lane < D_HD
    srow = lax.broadcasted_iota(jnp.int32, (SINK_ROWS, width), 0)
    vrow = lax.broadcasted_iota(jnp.int32, (SINK_ROWS, 128), 0)
    vlane = lax.broadcasted_iota(jnp.int32, (SINK_ROWS, 128), 1)
    v_sink = jnp.where(jnp.logical_and(vrow == 0, vlane == D_HD), 1.0, 0.0).astype(BF16)

    def stacked_q(bi, g):
        parts = []
        for sl in range(grp // 2):
            slab = q_ref[bi, :, (g * (grp // 2) + sl) * 128:(g * (grp // 2) + sl + 1) * 128]
            zero = jnp.zeros_like(slab)
            parts += [jnp.where(low, slab, zero), jnp.where(low, zero, slab)]
        return jnp.concatenate(parts, axis=0)

    def sink_row(g):
        return jnp.concatenate(
            [jnp.broadcast_to(sink_ref[g * grp + hh:g * grp + hh + 1, 0:1], (1, TQ)) for hh in range(grp)], axis=1)

    def finish(bi, g, o):
        out = o[:, 0:D_HD] / o[:, D_HD:D_HD + 1]
        out = jnp.concatenate([out, jnp.zeros_like(out)], axis=1)
        for sl in range(grp // 2):
            a = out[(2 * sl) * TQ:(2 * sl + 1) * TQ]
            b = pltpu.roll(out[(2 * sl + 1) * TQ:(2 * sl + 2) * TQ], D_HD, 1)
            col = (g * (grp // 2) + sl) * 128
            o_ref[bi, :, col:col + 128] = jnp.where(low, a, b).astype(o_ref.dtype)

    def attend(local):
        probs = [(bi, g) for bi in range(WIN_NB) for g in range(D_KV)]
        qz = [stacked_q(bi, g) for bi, g in probs]
        sk = [sink_row(g) for _, g in probs]
        sc, sl_, m = ([None] * len(probs) for _ in range(3))
        for n, (bi, g) in enumerate(probs):
            sc[n] = _dot_nt(k_ref[bi, 0:n_ctx, g * 128:(g + 1) * 128], qz[n])
            m[n] = jnp.maximum(jnp.max(sc[n], axis=0, keepdims=True), sk[n])
            if local is not None:
                start, mask = local
                kl = k_ref[bi, pl.ds(start, 3 * TQ), g * 128:(g + 1) * 128]
                sl_[n] = jnp.where(mask, _dot_nt(kl, qz[n]), NEG_INF)
                m[n] = jnp.maximum(m[n], jnp.max(sl_[n], axis=0, keepdims=True))
        for n, (bi, g) in enumerate(probs):
            e_sink = jnp.where(srow == 0, jnp.exp2(sk[n] - m[n]), 0.0).astype(BF16)
            o = (_dot_tn(jnp.exp2(sc[n] - m[n]).astype(BF16), v_ref[bi, 0:n_ctx, g * 128:(g + 1) * 128])
                 + _dot_tn(e_sink, v_sink))
            if local is not None:
                vl = v_ref[bi, pl.ds(local[0], 3 * TQ), g * 128:(g + 1) * 128]
                o = o + _dot_tn(jnp.exp2(sl_[n] - m[n]).astype(BF16), vl)
            finish(bi, g, o)

    @pl.when(i < nctx_tiles)
    def _():
        attend(None)

    @pl.when(i >= nctx_tiles)
    def _():
        il = i - nctx_tiles
        kb = jnp.clip(il - 1, 0, nlb - 3)
        start = pl.multiple_of(n_ctx + kb * TQ, TQ)
        kpos = kb * TQ + lax.broadcasted_iota(jnp.int32, (3 * TQ, width), 0)
        qpos = il * TQ + (lax.broadcasted_iota(jnp.int32, (3 * TQ, width), 1) % TQ)
        mask = jnp.abs(kpos - qpos) <= WINDOW
        attend((start, mask))


def _win_attn(q, k, v, sink, n_ctx):
    nb, t, _ = q.shape
    kv = pl.BlockSpec((WIN_NB, t, 256), lambda b, i: (b, 0, 0))
    qo = pl.BlockSpec((WIN_NB, TQ, 512), lambda b, i: (b, i, 0))
    return pl.pallas_call(
        functools.partial(_win_attn_kernel, n_ctx=n_ctx, n_lat=t - n_ctx),
        grid=(nb // WIN_NB, t // TQ),
        in_specs=[qo, kv, kv, _full(sink.shape)],
        out_specs=qo,
        out_shape=jax.ShapeDtypeStruct((nb, t, D_HEADS * D_HD), BF16),
        compiler_params=_cparams(("parallel", "arbitrary")),
        name="win_attn",
    )(q, k, v, sink)


def _odd_out_kernel(x_ref, mod_ref, hf_ref, hr_ref, gate_ref, od_ref, w_ref, o_ref):
    d = D_MODEL
    yc = (hf_ref[...] + hr_ref[...]) * _gelu_tanh(gate_ref[...])
    y = _dot(yc.astype(BF16), w_ref[0:512, :]) + _dot(od_ref[...], w_ref[512:1024, :])
    o_ref[...] = x_ref[...] + mod_ref[:, 2 * d:3 * d] * y


def _odd_out(x, mod, hf, hr, gate, od, w):
    nb, t, d = x.shape
    tile = lambda ww: pl.BlockSpec((None, TM, ww), lambda b, j: (b, j, 0))
    return pl.pallas_call(
        _odd_out_kernel,
        grid=(nb, t // TM),
        in_specs=[tile(d), _mod_spec(nb), tile(C_WIDTH), tile(C_WIDTH), tile(C_WIDTH), tile(512), _full(w.shape)],
        out_specs=tile(d),
        out_shape=jax.ShapeDtypeStruct(x.shape, F32),
        compiler_params=_cparams(("parallel", "parallel")),
        name="odd_out",
    )(x, mod, hf, hr, gate, od, w)


def _rope_tables(n_ctx, n_lat):
    pos = np.arange(n_lat)
    inv = ROPE_THETA ** (-np.arange(0, ROT_AXIS, 2, dtype=np.float64) / ROT_AXIS)
    ang_r = (pos // GRID_W)[:, None] * inv
    ang_c = (pos % GRID_W)[:, None] * inv
    lane = np.arange(128) % 64
    seg, f = lane // 16, lane % 16
    ang = np.where(seg[None, :] < 2, ang_r[:, f], ang_c[:, f])
    c = np.cos(ang)
    s = np.sin(ang)
    sa = np.where((seg % 2 == 0)[None, :], -s, 0.0)
    sb = np.where((seg % 2 == 1)[None, :], s, 0.0)
    pad = lambda a, fill: np.concatenate([np.full((n_ctx, 128), fill), a], axis=0).astype(np.float32)
    return jnp.asarray(pad(c, 1.0)), jnp.asarray(pad(sa, 0.0)), jnp.asarray(pad(sb, 0.0))


def _block_diag(w):
    eye = jnp.eye(C_BLOCKS, dtype=w.dtype)
    return jnp.einsum('hij,hg->higj', w, eye).reshape(C_WIDTH, C_WIDTH)


def kernel(x, c, ctx, c_ctx, w_ada, b_ada, norm_mix, norm_ffn, ffn_w_up, ffn_conv, ffn_w_down, final_norm,
           ev_w_in, ev_w_out, diff_lambda, diff_subln, gdn_conv, gdn_a_log, gdn_dt_bias, gdn_norm,
           od_w_in, od_w_out, lru_conv, lru_conv_b, lru_wa, lru_ba, lru_wx, lru_bx, lru_lambda, swa_sink):
    nb, n_lat, d = x.shape
    n_ctx = ctx.shape[1]
    depth = w_ada.shape[0]
    assert d == D_MODEL and n_ctx == TM and n_lat % TM == 0 and n_lat // TQ >= 3 and nb < MOD_ROWS
    assert nb % GDN_NB == 0 and nb % WIN_NB == 0
    t = n_ctx + n_lat

    xa = jnp.concatenate([ctx, x], axis=1)
    c_all = jnp.zeros((MOD_ROWS, d), F32).at[0:nb].set(c).at[nb].set(c_ctx)
    mod_all = _modulation(c_all, w_ada, b_ada).reshape(depth, MOD_ROWS, 1, N_MOD * d)
    tabs = _rope_tables(n_ctx, n_lat)
    row = lambda v: v.reshape(1, -1).astype(F32)

    for layer in range(depth):
        jx = layer // 2
        mod = mod_all[layer]
        mod2 = jnp.stack([jnp.broadcast_to(mod[nb], (nb, N_MOD * d)), mod[0:nb, 0]], axis=0)
        if layer % 2 == 0:
            lam_init = 0.8 - 0.6 * math.exp(-0.3 * layer)
            w_in = ev_w_in[jx]
            wa = w_in[:, 0:1536].astype(BF16)
            wg = w_in[:, 1536:3072].astype(BF16)
            wgate = w_in[:, 3072:3584].astype(BF16)
            wba = jnp.pad(w_in[:, 3584:3600], ((0, 0), (0, 112))).astype(BF16)
            pad16 = lambda v: jnp.pad(v.reshape(1, 8).astype(F32), ((0, 0), (8, 112)))
            q, k, v, qkv, gate, bg = _even_in(xa, mod, row(norm_mix[layer]), tabs, wa, wg, wgate, wba,
                                              gdn_conv[jx].astype(F32), pad16(gdn_a_log[jx]), pad16(gdn_dt_bias[jx]))
            ya = _diff_attn(q, k, v, diff_lambda[jx].astype(F32), diff_subln[jx].astype(F32).reshape(-1, 1),
                            lam_init, n_ctx)
            of = _gdn(qkv, bg, rev=False)
            orv = _gdn(qkv, bg, rev=True)
            xa = _even_out(xa, mod, ya, of, orv, gate, row(gdn_norm[jx]), ev_w_out[jx].astype(BF16))
        else:
            w_in = od_w_in[jx]
            wr = w_in[:, 0:1024].astype(BF16)
            wq = w_in[:, 1024:1536].astype(BF16)
            dup = lambda w: jnp.concatenate([w[:, 0:64], w[:, 0:64], w[:, 64:128], w[:, 64:128]], axis=1)
            zpad = lambda w: jnp.concatenate([w[:, 0:64], jnp.zeros_like(w[:, 0:64]), w[:, 64:128], jnp.zeros_like(w[:, 0:64])], axis=1)
            wkv = jnp.concatenate([dup(w_in[:, 1536:1664]), zpad(w_in[:, 1664:1792])], axis=1).astype(BF16)
            xr, gate, q, k, v = _odd_in(xa, mod, row(norm_mix[layer]), tabs, wr, wq, wkv)
            hs = []
            for dd in range(2):
                wbig = jnp.concatenate([_block_diag(lru_wa[jx, dd]), _block_diag(lru_wx[jx, dd])], axis=1)
                bbig = jnp.concatenate([lru_ba[jx, dd], lru_bx[jx, dd]]).reshape(1, -1).astype(F32)
                hs.append(_lru(xr, lru_conv[jx].astype(F32), row(lru_conv_b[jx]), wbig.astype(BF16), bbig,
                               row(lru_lambda[jx, dd]), n_ctx, rev=(dd == 1)))
            sink = jnp.broadcast_to(swa_sink[jx].astype(F32)[:, None] * LOG2E, (D_HEADS, 128))
            od = _win_attn(q, k, v, sink, n_ctx)
            xa = _odd_out(xa, mod, hs[0], hs[1], gate, od, od_w_out[jx].astype(BF16))
        xa = _ffn(xa, mod2, row(norm_ffn[layer]), ffn_w_up[layer].astype(BF16), ffn_conv[layer].astype(F32),
                  ffn_w_down[layer].astype(BF16), n_ctx, row(final_norm) if layer == depth - 1 else None)
    return xa
```

```python
import functools
import math

import jax
import jax.numpy as jnp
import numpy as np
from jax import lax
from jax.experimental import pallas as pl
from jax.experimental.pallas import tpu as pltpu

F32 = jnp.float32
BF16 = jnp.bfloat16
HIGHEST = lax.Precision.HIGHEST

D_MODEL = 1024
GRID_W = 64
EPS = 1e-6
NEG_INF = -1e30
N_MOD = 6
ROPE_THETA = 10000.0
ROT_AXIS = 32
A_HEADS = 4
A_HD = 64
A_VD = 128
B_HEADS = 4
B_DK = 128
B_W = 512
B_CONV = 4
B_CHUNK = 64
C_WIDTH = 512
C_BLOCKS = 8
C_BD = 64
C_CONV = 4
C_POW = 8.0
D_HEADS = 8
D_KV = 2
D_HD = 64
WINDOW = 128
FFN = 2816
FFN_CONV = 3

TM = 256
TQ = 128
TQA = 256
KB = 256
ONES_ROWS = 16
SINK_ROWS = 16
LOG2E = math.log2(math.e)
TT = 128
GDN_NB = 4
WIN_NB = 2
HALO = 8
FC = 256
TTF = 128
MOD_ROWS = 16
VMEM_LIMIT = 56 * 1024 * 1024


def _cparams(sem):
    return pltpu.CompilerParams(dimension_semantics=sem, vmem_limit_bytes=VMEM_LIMIT)


def _sigmoid(x):
    return 1.0 / (1.0 + jnp.exp(-x))


def _sigmoid_t(x):
    return 0.5 * (1.0 + jnp.tanh(0.5 * x))


def _silu(x):
    return x * _sigmoid(x)


def _softplus(x):
    return jnp.maximum(x, 0.0) + jnp.log(1.0 + jnp.exp(-jnp.abs(x)))


def _gelu_tanh(x):
    return 0.5 * x * (1.0 + jnp.tanh(math.sqrt(2.0 / math.pi) * (x + 0.044715 * (x * x * x))))


def _dot(a, b):
    return jnp.dot(a, b, preferred_element_type=F32)


def _dot_hi(a, b):
    return jnp.dot(a, b, preferred_element_type=F32, precision=HIGHEST)


def _dot3(a, b):
    ah = a.astype(BF16)
    al = (a - ah.astype(F32)).astype(BF16)
    bh = b.astype(BF16)
    bl = (b - bh.astype(F32)).astype(BF16)
    return _dot(ah, bh) + (_dot(ah, bl) + _dot(al, bh))


def _dot_nt(a, b):
    return lax.dot_general(a, b, (((1,), (1,)), ((), ())), preferred_element_type=F32)


def _dot_tn(a, b):
    return lax.dot_general(a, b, (((0,), (0,)), ((), ())), preferred_element_type=F32)


def _rms(x, g):
    return x * lax.rsqrt(jnp.mean(x * x, axis=-1, keepdims=True) + EPS) * g


def _modulate(x, g, shift, scale):
    return _rms(x, g) * (1.0 + scale) + shift


def _rope128(z, c, sa, sb):
    return z * c + pltpu.roll(z, 112, 1) * sa + pltpu.roll(z, 16, 1) * sb


def _rope(z, c, sa, sb):
    n = z.shape[1] // 128
    return jnp.concatenate([_rope128(z[:, i * 128:(i + 1) * 128], c, sa, sb) for i in range(n)], axis=1)


def _mod_kernel(s_ref, w_ref, b_ref, o_ref):
    s = _silu(s_ref[...])
    o_ref[...] = _dot_hi(s, w_ref[...]) + b_ref[...]


def _modulation(c_all, w_ada, b_ada):
    depth, d, n = w_ada.shape
    tn = 1536
    return pl.pallas_call(
        _mod_kernel,
        grid=(depth, n // tn),
        in_specs=[pl.BlockSpec((MOD_ROWS, d), lambda l, j: (0, 0)),
                  pl.BlockSpec((None, d, tn), lambda l, j: (l, 0, j)),
                  pl.BlockSpec((None, 1, tn), lambda l, j: (l, 0, j))],
        out_specs=pl.BlockSpec((None, MOD_ROWS, tn), lambda l, j: (l, 0, j)),
        out_shape=jax.ShapeDtypeStruct((depth, MOD_ROWS, n), F32),
        compiler_params=_cparams(("arbitrary", "arbitrary")),
        name="modulation",
    )(c_all, w_ada, b_ada.reshape(depth, 1, n))


def _mod_spec(nb):
    return pl.BlockSpec((None, 1, N_MOD * D_MODEL), lambda b, j: (jnp.where(j == 0, nb, b), 0, 0))


def _full(shape):
    nd = len(shape)
    return pl.BlockSpec(shape, lambda *_: (0,) * nd)


def _even_in_kernel(x_ref, prev_ref, next_ref, mod_ref, g_ref, c_ref, sa_ref, sb_ref, wa_ref, wg_ref, wgate_ref,
                    wba_ref, cw_ref, alog_ref, dtb_ref, q_ref, k_ref, v_ref, qkv_ref, gate_ref, bg_ref, ext_ref,
                    *, nt):
    d = D_MODEL
    j = pl.program_id(1)
    mod = mod_ref[...]
    g, shift, scale = g_ref[...], mod[:, 0:d], mod[:, d:2 * d]
    uf = _modulate(x_ref[...], g, shift, scale)
    u = uf.astype(BF16)
    lflag = (j >= 2).astype(F32)
    rflag = jnp.logical_and(j >= 1, j <= nt - 2).astype(F32)
    halo = jnp.concatenate([_modulate(prev_ref[...], g, shift, scale) * lflag,
                            _modulate(next_ref[...], g, shift, scale) * rflag], axis=0).astype(BF16)
    zh = _dot(halo, wg_ref[...])
    ext_ref[0:HALO, :] = zh[0:HALO]
    ext_ref[HALO:HALO + TM, :] = _dot(u, wg_ref[...])
    ext_ref[HALO + TM:2 * HALO + TM, :] = zh[HALO:2 * HALO]
    acc = None
    for kk in range(B_CONV):
        term = cw_ref[kk:kk + 1, :] * ext_ref[pl.ds(HALO - 2 + kk, TM), :]
        acc = term if acc is None else acc + term
    act = _silu(acc)

    def l2n(zz):
        return zz * lax.rsqrt(jnp.sum(zz * zz, axis=-1, keepdims=True) + EPS)

    for h in range(B_HEADS):
        qkv_ref[:, h * 128:(h + 1) * 128] = l2n(act[:, h * 128:(h + 1) * 128]) * (B_DK ** -0.5)
        qkv_ref[:, B_W + h * 128:B_W + (h + 1) * 128] = l2n(act[:, B_W + h * 128:B_W + (h + 1) * 128])
    qkv_ref[:, 2 * B_W:3 * B_W] = act[:, 2 * B_W:3 * B_W]

    c, sa, sb = c_ref[...], sa_ref[...], sb_ref[...]
    q = _dot(u, wa_ref[:, 0:512])
    q_ref[...] = (_rope(q, c, sa, sb) * (A_HD ** -0.5 * LOG2E)).astype(BF16)
    k = _dot(u, wa_ref[:, 512:1024])
    k_ref[...] = _rope(k, c, sa, sb).astype(BF16)
    v_ref[...] = _dot(u, wa_ref[:, 1024:1536]).astype(BF16)
    gate_ref[...] = _dot(u, wgate_ref[...]).astype(gate_ref.dtype)
    z = _dot(u, wba_ref[...])
    lane = lax.broadcasted_iota(jnp.int32, z.shape, 1)
    beta = _sigmoid(z)
    gdec = -jnp.exp(alog_ref[...]) * _softplus(z + dtb_ref[...])
    bg_ref[...] = jnp.where(lane < 2 * B_HEADS, beta, jnp.where(lane < 4 * B_HEADS, gdec, 0.0))


def _even_in(x, mod, g, tabs, wa, wg, wgate, wba, cw, alog, dtb):
    nb, t, d = x.shape
    nt = t // TM
    hb = TM // HALO
    tile = lambda w: pl.BlockSpec((None, TM, w), lambda b, j: (b, j, 0))
    prev = pl.BlockSpec((None, HALO, d), lambda b, j: (b, jnp.maximum(j * hb - 1, 0), 0))
    nxt = pl.BlockSpec((None, HALO, d), lambda b, j: (b, jnp.minimum((j + 1) * hb, t // HALO - 1), 0))
    tab = pl.BlockSpec((TM, 128), lambda b, j: (j, 0))
    outs = [jax.ShapeDtypeStruct((nb, t, 512), BF16)] * 3 + [
        jax.ShapeDtypeStruct((nb, t, 1536), F32), jax.ShapeDtypeStruct((nb, t, 512), BF16),
        jax.ShapeDtypeStruct((nb, t, 128), F32)]
    return pl.pallas_call(
        functools.partial(_even_in_kernel, nt=nt),
        grid=(nb, nt),
        in_specs=[tile(d), prev, nxt, _mod_spec(nb), _full((1, d)), tab, tab, tab, _full(wa.shape), _full(wg.shape),
                  _full(wgate.shape), _full(wba.shape), _full(cw.shape), _full((1, 128)), _full((1, 128))],
        out_specs=[tile(512), tile(512), tile(512), tile(1536), tile(512), tile(128)],
        out_shape=outs,
        scratch_shapes=[pltpu.VMEM((TM + 2 * HALO, 1536), F32)],
        compiler_params=_cparams(("parallel", "parallel")),
        name="even_in",
    )(x, x, x, mod, g, *tabs, wa, wg, wgate, wba, cw, alog, dtb)


def _diff_attn_kernel(q_ref, k_ref, v_ref, lv_ref, g_ref, o_ref, vt_ref, sa_ref, sb_ref, ma_ref, mb_ref,
                      *, lam_init, n_ctx):
    i = pl.program_id(2)
    t = k_ref.shape[0]
    nblk = t // KB

    @pl.when(i == 0)
    def _():
        vt_ref[0:A_VD, :] = v_ref[...].astype(F32).T.astype(BF16)
        orow = lax.broadcasted_iota(jnp.int32, (ONES_ROWS, t), 0)
        vt_ref[A_VD:A_VD + ONES_ROWS, :] = jnp.where(orow == 0, 1.0, 0.0).astype(BF16)
        sb_ref[...] = jnp.zeros_like(sb_ref)
        mb_ref[...] = jnp.zeros_like(mb_ref)

    def step(s_new, m_new, s_old, m_old):
        lv = lv_ref[...]
        lam = (jnp.exp(jnp.sum(lv[0:1] * lv[1:2], axis=-1, keepdims=True))
               - jnp.exp(jnp.sum(lv[2:3] * lv[3:4], axis=-1, keepdims=True)) + lam_init)
        m = jnp.max(m_old[...], axis=0, keepdims=True)
        is_ctx = jnp.minimum(i, pl.num_programs(2) - 2) < n_ctx // TQA
        q = q_ref[...]
        lane = lax.broadcasted_iota(jnp.int32, q.shape, 1)
        zero = jnp.zeros_like(q)
        qz = jnp.concatenate([jnp.where(lane < A_HD, q, zero), jnp.where(lane >= A_HD, q, zero)], axis=0)
        mrun = None
        oe = None
        for kb in range(nblk):
            rows = slice(kb * KB, (kb + 1) * KB)
            e = jnp.exp2(s_old[rows, :] - m).astype(BF16)
            part = _dot(vt_ref[:, rows], e)
            oe = part if oe is None else oe + part
            sblk = _dot_nt(k_ref[rows, :], qz)
            if kb >= n_ctx // KB:
                sblk = jnp.where(is_ctx, NEG_INF, sblk)
            s_new[rows, :] = sblk
            part = jnp.max(sblk.reshape(KB // 8, 8, 2 * TQA), axis=0)
            mrun = part if mrun is None else jnp.maximum(mrun, part)
        m_new[...] = mrun
        on = oe[0:A_VD] / oe[A_VD:A_VD + 1]
        od = on[:, 0:TQA] - lam * on[:, TQA:2 * TQA]
        y = od * lax.rsqrt(jnp.mean(od * od, axis=0, keepdims=True) + EPS) * (g_ref[...] * (1.0 - lam_init))
        o_ref[...] = y.T.astype(o_ref.dtype)

    @pl.when(i % 2 == 0)
    def _():
        step(sa_ref, ma_ref, sb_ref, mb_ref)

    @pl.when(i % 2 == 1)
    def _():
        step(sb_ref, mb_ref, sa_ref, ma_ref)


def _diff_attn(q, k, v, lam_vec, subln, lam_init, n_ctx):
    nb, t, _ = q.shape
    nq = t // TQA
    kv = pl.BlockSpec((None, t, 128), lambda b, h, i: (b, 0, h))
    qin = pl.BlockSpec((None, TQA, 128), lambda b, h, i: (b, jnp.minimum(i, nq - 1), h))
    out = pl.BlockSpec((None, TQA, 128), lambda b, h, i: (b, jnp.maximum(i - 1, 0), h))
    return pl.pallas_call(
        functools.partial(_diff_attn_kernel, lam_init=lam_init, n_ctx=n_ctx),
        grid=(nb, A_HEADS, nq + 1),
        in_specs=[qin, kv, kv, _full((4, A_HD)), _full((A_VD, 1))],
        out_specs=out,
        out_shape=jax.ShapeDtypeStruct((nb, t, A_HEADS * A_VD), BF16),
        scratch_shapes=[pltpu.VMEM((A_VD + ONES_ROWS, t), BF16), pltpu.VMEM((t, 2 * TQA), F32),
                        pltpu.VMEM((t, 2 * TQA), F32), pltpu.VMEM((8, 2 * TQA), F32), pltpu.VMEM((8, 2 * TQA), F32)],
        compiler_params=_cparams(("parallel", "parallel", "arbitrary")),
        name="diff_attn",
    )(q, k, v, lam_vec, subln)


def _gdn_tile_index(j, nt, rev):
    return jnp.where(j == 0, 0, nt - j) if rev else j


def _gdn_kernel(qkv_ref, bg_ref, o_ref, s_ref, *, rev, nt):
    j = pl.program_id(1)
    dirn = 1 if rev else 0
    nch = TM // B_CHUNK

    @pl.when(j == 0)
    def _():
        s_ref[...] = jnp.zeros_like(s_ref)

    ri = lax.broadcasted_iota(jnp.int32, (TM, TM), 0)
    ci = lax.broadcasted_iota(jnp.int32, (TM, TM), 1)
    same = (ri // B_CHUNK) == (ci // B_CHUNK)
    incl = jnp.logical_and(same, (ri <= ci) if rev else (ri >= ci))
    strict = jnp.logical_and(same, (ri < ci) if rev else (ri > ci))
    eye = (ri == ci).astype(F32)
    inclb = incl.astype(BF16)
    rchunk = lax.broadcasted_iota(jnp.int32, (TM, B_DK), 0) // B_CHUNK

    def by_chunk(z):
        return jnp.concatenate([jnp.where(rchunk == c, z, 0.0) for c in range(nch)], axis=1).astype(BF16)

    chains = [(bi, h) for bi in range(GDN_NB) for h in range(B_HEADS)]
    nchain = len(chains)
    lasts = [c * B_CHUNK if rev else (c + 1) * B_CHUNK - 1 for c in range(nch)]
    q, k, v, beta, gcol, eg, qkm, p, pw = ([None] * nchain for _ in range(9))
    for bi in range(GDN_NB):
        bg = bg_ref[bi]
        b1 = bg.astype(BF16)
        r1 = bg - b1.astype(F32)
        b2 = r1.astype(BF16)
        b3 = (r1 - b2.astype(F32)).astype(BF16)
        gcum = _dot(inclb, b1) + (_dot(inclb, b2) + _dot(inclb, b3))
        gcum_t = gcum.T
        for h in range(B_HEADS):
            n = bi * B_HEADS + h
            q[n] = qkv_ref[bi, :, h * 128:(h + 1) * 128]
            k[n] = qkv_ref[bi, :, B_W + h * 128:B_W + (h + 1) * 128]
            v[n] = qkv_ref[bi, :, 2 * B_W + h * 128:2 * B_W + (h + 1) * 128]
            cb = dirn * B_HEADS + h
            cg = 2 * B_HEADS + cb
            beta[n] = bg[:, cb:cb + 1]
            gcol[n] = gcum[:, cg:cg + 1]
            grow = gcum_t[cg:cg + 1, :]
            eg[n] = jnp.exp(gcol[n])
            decay = jnp.where(incl, jnp.exp(jnp.where(incl, gcol[n] - grow, 0.0)), 0.0)
            kb = k[n].astype(BF16)
            qkm[n] = _dot_nt(q[n].astype(BF16), kb) * decay
            pw[n] = jnp.where(strict, beta[n] * _dot_nt(kb, kb) * decay, 0.0)

    xr = ri ^ ci
    for lvl in range(6):
        joins = (xr >> lvl) == 1
        for n in range(nchain):
            l_s = jnp.where(joins, pw[n], 0.0)
            if lvl == 0:
                p[n] = eye - l_s
            else:
                pb = p[n].astype(BF16)
                p[n] = p[n] - _dot(pb, _dot(l_s.astype(BF16), pb).astype(BF16))

    qku, qeff, mn = ([None] * nchain for _ in range(3))
    for n in range(nchain):
        rhs = jnp.concatenate([beta[n] * v[n], (beta[n] * eg[n]) * k[n]], axis=1)
        uw = _dot(p[n].astype(BF16), rhs.astype(BF16))
        qkuw = _dot(qkm[n].astype(BF16), uw.astype(BF16))
        qku[n] = qkuw[:, 0:128]
        qeff[n] = (q[n] * eg[n] - qkuw[:, 128:256]).astype(BF16)
        glast = jnp.concatenate(
            [jnp.broadcast_to(gcol[n][r:r + 1, :], (B_CHUNK, 1)) for r in lasts], axis=0)
        kdec = (k[n] * jnp.exp(glast - gcol[n])).astype(BF16)
        mn[n] = _dot_tn(kdec, jnp.concatenate([by_chunk(uw[:, 128:256]), by_chunk(uw[:, 0:128])], axis=1))

    for step in range(nch):
        c = nch - 1 - step if rev else step
        r0, r1 = c * B_CHUNK, (c + 1) * B_CHUNK
        for n, (bi, h) in enumerate(chains):
            sh = s_ref[n]
            shb = sh.astype(BF16)
            o_ref[bi, r0:r1, h * 128:(h + 1) * 128] = (_dot(qeff[n][r0:r1], shb) + qku[n][r0:r1]).astype(o_ref.dtype)
            mc = mn[n][:, c * 128:(c + 1) * 128].astype(BF16)
            nc = mn[n][:, (nch + c) * 128:(nch + c + 1) * 128]
            gl = jnp.exp(gcol[n][lasts[c]:lasts[c] + 1, :])
            s_ref[n] = sh * gl - _dot(mc, shb) + nc


def _gdn(qkv, bg, rev):
    nb, t, w = qkv.shape
    nt = t // TM
    main = lambda ww: pl.BlockSpec((GDN_NB, TM, ww), lambda b, j: (b, _gdn_tile_index(j, nt, rev), 0))
    return pl.pallas_call(
        functools.partial(_gdn_kernel, rev=rev, nt=nt),
        grid=(nb // GDN_NB, nt),
        in_specs=[main(w), main(128)],
        out_specs=main(B_W),
        out_shape=jax.ShapeDtypeStruct((nb, t, B_W), BF16),
        scratch_shapes=[pltpu.VMEM((GDN_NB * B_HEADS, B_DK, B_DK), F32)],
        compiler_params=_cparams(("parallel", "arbitrary")),
        name="gdn_rev" if rev else "gdn_fwd",
    )(qkv, bg)


def _even_out_kernel(x_ref, mod_ref, ya_ref, of_ref, or_ref, gate_ref, og_ref, w_ref, o_ref):
    d = D_MODEL
    ob = of_ref[...].astype(F32) + or_ref[...].astype(F32)
    gate = gate_ref[...].astype(F32)
    yb = jnp.concatenate(
        [_rms(ob[:, h * 128:(h + 1) * 128], og_ref[...]) * _silu(gate[:, h * 128:(h + 1) * 128])
         for h in range(B_HEADS)], axis=1)
    y = _dot(ya_ref[...], w_ref[0:512, :]) + _dot(yb.astype(BF16), w_ref[512:1024, :])
    o_ref[...] = x_ref[...] + mod_ref[:, 2 * d:3 * d] * y


def _even_out(x, mod, ya, of, orv, gate, og, w):
    nb, t, d = x.shape
    tile = lambda ww: pl.BlockSpec((None, TM, ww), lambda b, j: (b, j, 0))
    return pl.pallas_call(
        _even_out_kernel,
        grid=(nb, t // TM),
        in_specs=[tile(d), _mod_spec(nb), tile(512), tile(512), tile(512), tile(512), _full((1, 128)),
                  _full(w.shape)],
        out_specs=tile(d),
        out_shape=jax.ShapeDtypeStruct(x.shape, F32),
        compiler_params=_cparams(("parallel", "parallel")),
        name="even_out",
    )(x, mod, ya, of, orv, gate, og, w)


def _ffn_kernel(x_ref, prev_ref, next_ref, mod_ref, g_ref, wup_ref, cw_ref, wdn_ref, fg_ref, o_ref, act_ref,
                *, ntt, nct, final):
    d = D_MODEL
    nb = x_ref.shape[0]
    j = pl.program_id(0)

    def body():
        mod = mod_ref[...]
        shift, scale, gate = mod[:, 3 * d:4 * d], mod[:, 4 * d:5 * d], mod[:, 5 * d:6 * d]
        g = g_ref[...]
        lflag = jnp.logical_and(j != 0, j != nct).astype(F32)
        rflag = jnp.logical_and(j != nct - 1, j != ntt - 1).astype(F32)
        x3 = pltpu.einshape("btd->tbd", x_ref[...])
        xp = pltpu.einshape("btd->tbd", prev_ref[...])[HALO - 1]
        xn = pltpu.einshape("btd->tbd", next_ref[...])[0]
        u3 = jnp.concatenate([(_modulate(xp, g, shift, scale) * lflag)[None],
                              _modulate(x3, g, shift, scale),
                              (_modulate(xn, g, shift, scale) * rflag)[None]], axis=0)
        u = u3.reshape((TTF + 2) * nb, d).astype(BF16)
        rows = TTF * nb
        for c in range(FFN // FC):
            c0 = c * FC
            hg = _dot(u, wup_ref[:, c0:c0 + FC])
            hv = _dot(u, wup_ref[:, FFN + c0:FFN + c0 + FC])
            cg = None
            cv = None
            for kk in range(FFN_CONV):
                tg = cw_ref[kk:kk + 1, c0:c0 + FC] * hg[kk * nb:kk * nb + rows]
                tv = cw_ref[kk:kk + 1, FFN + c0:FFN + c0 + FC] * hv[kk * nb:kk * nb + rows]
                cg = tg if cg is None else cg + tg
                cv = tv if cv is None else cv + tv
            act_ref[:, c0:c0 + FC] = (_silu(cg) * cv).astype(BF16)
        out3 = x3 + gate * _dot(act_ref[...], wdn_ref[...]).reshape(TTF, nb, d)
        if final:
            out3 = _rms(out3, fg_ref[...])
        o_ref[...] = pltpu.einshape("tbd->btd", out3)

    if final:
        pl.when(j >= nct)(body)
    else:
        body()


def _ffn(x, mod2, g, wup, cw, wdn, n_ctx, final_g=None):
    nb, t, d = x.shape
    ntt, nct = t // TTF, n_ctx // TTF
    hb = TTF // HALO
    tile = pl.BlockSpec((nb, TTF, d), lambda j: (0, j, 0))
    prev = pl.BlockSpec((nb, HALO, d), lambda j: (0, jnp.maximum(j * hb - 1, 0), 0))
    nxt = pl.BlockSpec((nb, HALO, d), lambda j: (0, jnp.minimum((j + 1) * hb, t // HALO - 1), 0))
    modspec = pl.BlockSpec((None, nb, N_MOD * d), lambda j: (jnp.where(j < nct, 0, 1), 0, 0))
    resident = lambda shape: pl.BlockSpec(shape, lambda j: (0, 0), pipeline_mode=pl.Buffered(1))
    final = final_g is not None
    t_out = t - n_ctx if final else t
    out_tile = pl.BlockSpec((nb, TTF, d), lambda j: (0, jnp.maximum(j - nct, 0), 0)) if final else tile
    out = pl.pallas_call(
        functools.partial(_ffn_kernel, ntt=ntt, nct=nct, final=final),
        grid=(ntt,),
        in_specs=[tile, prev, nxt, modspec, _full((1, d)), resident(wup.shape), _full(cw.shape),
                  resident(wdn.shape), _full((1, d))],
        out_specs=out_tile,
        out_shape=jax.ShapeDtypeStruct((nb, t_out, d), F32),
        scratch_shapes=[pltpu.VMEM((TTF * nb, FFN), BF16)],
        compiler_params=_cparams(("arbitrary" if final else "parallel",)),
        name="ffn",
    )(x, x, x, mod2, g, wup, cw, wdn, final_g if final else g)
    return out


def _odd_in_kernel(x_ref, mod_ref, g_ref, c_ref, sa_ref, sb_ref, wr_ref, wq_ref, wkv_ref,
                   xr_ref, gate_ref, q_ref, k_ref, v_ref):
    d = D_MODEL
    mod = mod_ref[...]
    u = _modulate(x_ref[...], g_ref[...], mod[:, 0:d], mod[:, d:2 * d]).astype(BF16)
    c, sa, sb = c_ref[...], sa_ref[...], sb_ref[...]
    xr_ref[...] = _dot(u, wr_ref[:, 0:C_WIDTH])
    gate_ref[...] = _dot(u, wr_ref[:, C_WIDTH:2 * C_WIDTH])
    q_ref[...] = (_rope(_dot(u, wq_ref[...]), c, sa, sb) * (D_HD ** -0.5 * LOG2E)).astype(BF16)
    k_ref[...] = _rope(_dot(u, wkv_ref[:, 0:256]), c, sa, sb).astype(BF16)
    v = _dot(u, wkv_ref[:, 256:512])
    vlane = lax.broadcasted_iota(jnp.int32, v.shape, 1)
    v_ref[...] = jnp.where(vlane % 128 == D_HD, 1.0, v).astype(BF16)


def _odd_in(x, mod, g, tabs, wr, wq, wkv):
    nb, t, d = x.shape
    tile = lambda w: pl.BlockSpec((None, TM, w), lambda b, j: (b, j, 0))
    tab = pl.BlockSpec((TM, 128), lambda b, j: (j, 0))
    outs = [jax.ShapeDtypeStruct((nb, t, C_WIDTH), F32)] * 2 + [
        jax.ShapeDtypeStruct((nb, t, 512), BF16), jax.ShapeDtypeStruct((nb, t, 256), BF16),
        jax.ShapeDtypeStruct((nb, t, 256), BF16)]
    return pl.pallas_call(
        _odd_in_kernel,
        grid=(nb, t // TM),
        in_specs=[tile(d), _mod_spec(nb), _full((1, d)), tab, tab, tab, _full(wr.shape), _full(wq.shape),
                  _full(wkv.shape)],
        out_specs=[tile(C_WIDTH), tile(C_WIDTH), tile(512), tile(256), tile(256)],
        out_shape=outs,
        compiler_params=_cparams(("parallel", "parallel")),
        name="odd_in",
    )(x, mod, g, *tabs, wr, wq, wkv)


def _lru_tile_index(j, ntt, nct, rev):
    return jnp.where(j < nct, nct - 1 - j, ntt + nct - 1 - j) if rev else j


def _lru_kernel(x_ref, prev_ref, next_ref, cw_ref, cb_ref, w_ref, b_ref, lam_ref, o_ref,
                h_ref, a_ref, bc_ref, *, rev, ntt, nct):
    j = pl.program_id(0)
    jj = _lru_tile_index(j, ntt, nct, rev)
    nb = x_ref.shape[0]
    tmajor = lambda ref: pltpu.einshape("btc->tbc", ref[...])

    @pl.when(j == 0)
    def _():
        h_ref[...] = jnp.zeros_like(h_ref)

    lflag = jnp.logical_and(jj != 0, jj != nct).astype(F32)
    rflag = jnp.logical_and(jj != nct - 1, jj != ntt - 1).astype(F32)
    ext = jnp.concatenate([tmajor(prev_ref)[HALO - 2:HALO] * lflag, tmajor(x_ref),
                           tmajor(next_ref)[0:1] * rflag], axis=0)
    xc = cb_ref[...]
    for kk in range(C_CONV):
        xc = xc + cw_ref[kk:kk + 1, :] * ext[kk:kk + TT]
    xc2 = xc.reshape(TT * nb, C_WIDTH)
    z = _dot(xc2.astype(BF16), w_ref[...]) + b_ref[...]
    r = _sigmoid_t(z[:, 0:C_WIDTH])
    gi = _sigmoid_t(z[:, C_WIDTH:2 * C_WIDTH])
    log_a = -C_POW * r * _softplus(-lam_ref[...])
    a = jnp.exp(log_a)
    bc = jnp.sqrt(1.0 - a * a) * (gi * xc2)
    a_ref[...] = a.reshape(TT, nb, C_WIDTH)
    bc_ref[...] = bc.reshape(TT, nb, C_WIDTH)

    def body(s, h):
        t = TT - 1 - s if rev else s
        h = a_ref[t] * h + bc_ref[t]
        a_ref[t] = h
        return h

    h_ref[...] = lax.fori_loop(0, TT, body, h_ref[...], unroll=8)
    o_ref[...] = pltpu.einshape("tbc->btc", a_ref[...])


def _lru(xr, conv_w, conv_b, w, b, lam, n_ctx, rev):
    nb, t, c = xr.shape
    ntt, nct = t // TT, n_ctx // TT
    hb = TT // HALO
    idx = lambda j: _lru_tile_index(j, ntt, nct, rev)
    main = pl.BlockSpec((nb, TT, c), lambda j: (0, idx(j), 0))
    prev = pl.BlockSpec((nb, HALO, c), lambda j: (0, jnp.maximum(idx(j) * hb - 1, 0), 0))
    nxt = pl.BlockSpec((nb, HALO, c), lambda j: (0, jnp.minimum((idx(j) + 1) * hb, t // HALO - 1), 0))
    return pl.pallas_call(
        functools.partial(_lru_kernel, rev=rev, ntt=ntt, nct=nct),
        grid=(ntt,),
        in_specs=[main, prev, nxt, _full(conv_w.shape), _full(conv_b.shape), _full(w.shape), _full(b.shape),
                  _full(lam.shape)],
        out_specs=main,
        out_shape=jax.ShapeDtypeStruct(xr.shape, F32),
        scratch_shapes=[pltpu.VMEM((nb, c), F32), pltpu.VMEM((TT, nb, c), F32), pltpu.VMEM((TT, nb, c), F32)],
        compiler_params=_cparams(("arbitrary",)),
        name="lru_rev" if rev else "lru_fwd",
    )(xr, xr, xr, conv_w, conv_b, w, b, lam)


def _win_attn_kernel(q_ref, k_ref, v_ref, sink_ref, o_ref, *, n_ctx, n_lat):
    i = pl.program_id(1)
    nctx_tiles = n_ctx // TQ
    nlb = n_lat // TQ
    grp = D_HEADS // D_KV
    width = grp * TQ
    lane = lax.broadcasted_iota(jnp.int32, (TQ, 128), 1)
    low = lane < D_HD
    srow = lax.broadcasted_iota(jnp.int32, (SINK_ROWS, width), 0)
    vrow = lax.broadcasted_iota(jnp.int32, (SINK_ROWS, 128), 0)
    vlane = lax.broadcasted_iota(jnp.int32, (SINK_ROWS, 128), 1)
    v_sink = jnp.where(jnp.logical_and(vrow == 0, vlane == D_HD), 1.0, 0.0).astype(BF16)

    def stacked_q(bi, g):
        parts = []
        for sl in range(grp // 2):
            slab = q_ref[bi, :, (g * (grp // 2) + sl) * 128:(g * (grp // 2) + sl + 1) * 128]
            zero = jnp.zeros_like(slab)
            parts += [jnp.where(low, slab, zero), jnp.where(low, zero, slab)]
        return jnp.concatenate(parts, axis=0)

    def sink_row(g):
        return jnp.concatenate(
            [jnp.broadcast_to(sink_ref[g * grp + hh:g * grp + hh + 1, 0:1], (1, TQ)) for hh in range(grp)], axis=1)

    def finish(bi, g, o):
        out = o[:, 0:D_HD] / o[:, D_HD:D_HD + 1]
        out = jnp.concatenate([out, jnp.zeros_like(out)], axis=1)
        for sl in range(grp // 2):
            a = out[(2 * sl) * TQ:(2 * sl + 1) * TQ]
            b = pltpu.roll(out[(2 * sl + 1) * TQ:(2 * sl + 2) * TQ], D_HD, 1)
            col = (g * (grp // 2) + sl) * 128
            o_ref[bi, :, col:col + 128] = jnp.where(low, a, b).astype(o_ref.dtype)

    def attend(local):
        probs = [(bi, g) for bi in range(WIN_NB) for g in range(D_KV)]
        qz = [stacked_q(bi, g) for bi, g in probs]
        sk = [sink_row(g) for _, g in probs]
        sc, sl_, m = ([None] * len(probs) for _ in range(3))
        for n, (bi, g) in enumerate(probs):
            sc[n] = _dot_nt(k_ref[bi, 0:n_ctx, g * 128:(g + 1) * 128], qz[n])
            m[n] = jnp.maximum(jnp.max(sc[n], axis=0, keepdims=True), sk[n])
            if local is not None:
                start, mask = local
                kl = k_ref[bi, pl.ds(start, 3 * TQ), g * 128:(g + 1) * 128]
                sl_[n] = jnp.where(mask, _dot_nt(kl, qz[n]), NEG_INF)
                m[n] = jnp.maximum(m[n], jnp.max(sl_[n], axis=0, keepdims=True))
        for n, (bi, g) in enumerate(probs):
            e_sink = jnp.where(srow == 0, jnp.exp2(sk[n] - m[n]), 0.0).astype(BF16)
            o = (_dot_tn(jnp.exp2(sc[n] - m[n]).astype(BF16), v_ref[bi, 0:n_ctx, g * 128:(g + 1) * 128])
                 + _dot_tn(e_sink, v_sink))
            if local is not None:
                vl = v_ref[bi, pl.ds(local[0], 3 * TQ), g * 128:(g + 1) * 128]
                o = o + _dot_tn(jnp.exp2(sl_[n] - m[n]).astype(BF16), vl)
            finish(bi, g, o)

    @pl.when(i < nctx_tiles)
    def _():
        attend(None)

    @pl.when(i >= nctx_tiles)
    def _():
        il = i - nctx_tiles
        kb = jnp.clip(il - 1, 0, nlb - 3)
        start = pl.multiple_of(n_ctx + kb * TQ, TQ)
        kpos = kb * TQ + lax.broadcasted_iota(jnp.int32, (3 * TQ, width), 0)
        qpos = il * TQ + (lax.broadcasted_iota(jnp.int32, (3 * TQ, width), 1) % TQ)
        mask = jnp.abs(kpos - qpos) <= WINDOW
        attend((start, mask))


def _win_attn(q, k, v, sink, n_ctx):
    nb, t, _ = q.shape
    kv = pl.BlockSpec((WIN_NB, t, 256), lambda b, i: (b, 0, 0))
    qo = pl.BlockSpec((WIN_NB, TQ, 512), lambda b, i: (b, i, 0))
    return pl.pallas_call(
        functools.partial(_win_attn_kernel, n_ctx=n_ctx, n_lat=t - n_ctx),
        grid=(nb // WIN_NB, t // TQ),
        in_specs=[qo, kv, kv, _full(sink.shape)],
        out_specs=qo,
        out_shape=jax.ShapeDtypeStruct((nb, t, D_HEADS * D_HD), BF16),
        compiler_params=_cparams(("parallel", "arbitrary")),
        name="win_attn",
    )(q, k, v, sink)


def _odd_out_kernel(x_ref, mod_ref, hf_ref, hr_ref, gate_ref, od_ref, w_ref, o_ref):
    d = D_MODEL
    yc = (hf_ref[...] + hr_ref[...]) * _gelu_tanh(gate_ref[...])
    y = _dot(yc.astype(BF16), w_ref[0:512, :]) + _dot(od_ref[...], w_ref[512:1024, :])
    o_ref[...] = x_ref[...] + mod_ref[:, 2 * d:3 * d] * y


def _odd_out(x, mod, hf, hr, gate, od, w):
    nb, t, d = x.shape
    tile = lambda ww: pl.BlockSpec((None, TM, ww), lambda b, j: (b, j, 0))
    return pl.pallas_call(
        _odd_out_kernel,
        grid=(nb, t // TM),
        in_specs=[tile(d), _mod_spec(nb), tile(C_WIDTH), tile(C_WIDTH), tile(C_WIDTH), tile(512), _full(w.shape)],
        out_specs=tile(d),
        out_shape=jax.ShapeDtypeStruct(x.shape, F32),
        compiler_params=_cparams(("parallel", "parallel")),
        name="odd_out",
    )(x, mod, hf, hr, gate, od, w)


def _rope_tables(n_ctx, n_lat):
    pos = np.arange(n_lat)
    inv = ROPE_THETA ** (-np.arange(0, ROT_AXIS, 2, dtype=np.float64) / ROT_AXIS)
    ang_r = (pos // GRID_W)[:, None] * inv
    ang_c = (pos % GRID_W)[:, None] * inv
    lane = np.arange(128) % 64
    seg, f = lane // 16, lane % 16
    ang = np.where(seg[None, :] < 2, ang_r[:, f], ang_c[:, f])
    c = np.cos(ang)
    s = np.sin(ang)
    sa = np.where((seg % 2 == 0)[None, :], -s, 0.0)
    sb = np.where((seg % 2 == 1)[None, :], s, 0.0)
    pad = lambda a, fill: np.concatenate([np.full((n_ctx, 128), fill), a], axis=0).astype(np.float32)
    return jnp.asarray(pad(c, 1.0)), jnp.asarray(pad(sa, 0.0)), jnp.asarray(pad(sb, 0.0))


def _block_diag(w):
    eye = jnp.eye(C_BLOCKS, dtype=w.dtype)
    return jnp.einsum('hij,hg->higj', w, eye).reshape(C_WIDTH, C_WIDTH)


def kernel(x, c, ctx, c_ctx, w_ada, b_ada, norm_mix, norm_ffn, ffn_w_up, ffn_conv, ffn_w_down, final_norm,
           ev_w_in, ev_w_out, diff_lambda, diff_subln, gdn_conv, gdn_a_log, gdn_dt_bias, gdn_norm,
           od_w_in, od_w_out, lru_conv, lru_conv_b, lru_wa, lru_ba, lru_wx, lru_bx, lru_lambda, swa_sink):
    nb, n_lat, d = x.shape
    n_ctx = ctx.shape[1]
    depth = w_ada.shape[0]
    assert d == D_MODEL and n_ctx == TM and n_lat % TM == 0 and n_lat // TQ >= 3 and nb < MOD_ROWS
    assert nb % GDN_NB == 0 and nb % WIN_NB == 0
    t = n_ctx + n_lat

    xa = jnp.concatenate([ctx, x], axis=1)
    c_all = jnp.zeros((MOD_ROWS, d), F32).at[0:nb].set(c).at[nb].set(c_ctx)
    mod_all = _modulation(c_all, w_ada, b_ada).reshape(depth, MOD_ROWS, 1, N_MOD * d)
    tabs = _rope_tables(n_ctx, n_lat)
    row = lambda v: v.reshape(1, -1).astype(F32)

    for layer in range(depth):
        jx = layer // 2
        mod = mod_all[layer]
        mod2 = jnp.stack([jnp.broadcast_to(mod[nb], (nb, N_MOD * d)), mod[0:nb, 0]], axis=0)
        if layer % 2 == 0:
            lam_init = 0.8 - 0.6 * math.exp(-0.3 * layer)
            w_in = ev_w_in[jx]
            wa = w_in[:, 0:1536].astype(BF16)
            wg = w_in[:, 1536:3072].astype(BF16)
            wgate = w_in[:, 3072:3584].astype(BF16)
            wba = jnp.pad(w_in[:, 3584:3600], ((0, 0), (0, 112))).astype(BF16)
            pad16 = lambda v: jnp.pad(v.reshape(1, 8).astype(F32), ((0, 0), (8, 112)))
            q, k, v, qkv, gate, bg = _even_in(xa, mod, row(norm_mix[layer]), tabs, wa, wg, wgate, wba,
                                              gdn_conv[jx].astype(F32), pad16(gdn_a_log[jx]), pad16(gdn_dt_bias[jx]))
            ya = _diff_attn(q, k, v, diff_lambda[jx].astype(F32), diff_subln[jx].astype(F32).reshape(-1, 1),
                            lam_init, n_ctx)
            of = _gdn(qkv, bg, rev=False)
            orv = _gdn(qkv, bg, rev=True)
            xa = _even_out(xa, mod, ya, of, orv, gate, row(gdn_norm[jx]), ev_w_out[jx].astype(BF16))
        else:
            w_in = od_w_in[jx]
            wr = w_in[:, 0:1024].astype(BF16)
            wq = w_in[:, 1024:1536].astype(BF16)
            dup = lambda w: jnp.concatenate([w[:, 0:64], w[:, 0:64], w[:, 64:128], w[:, 64:128]], axis=1)
            zpad = lambda w: jnp.concatenate([w[:, 0:64], jnp.zeros_like(w[:, 0:64]), w[:, 64:128], jnp.zeros_like(w[:, 0:64])], axis=1)
            wkv = jnp.concatenate([dup(w_in[:, 1536:1664]), zpad(w_in[:, 1664:1792])], axis=1).astype(BF16)
            xr, gate, q, k, v = _odd_in(xa, mod, row(norm_mix[layer]), tabs, wr, wq, wkv)
            hs = []
            for dd in range(2):
                wbig = jnp.concatenate([_block_diag(lru_wa[jx, dd]), _block_diag(lru_wx[jx, dd])], axis=1)
                bbig = jnp.concatenate([lru_ba[jx, dd], lru_bx[jx, dd]]).reshape(1, -1).astype(F32)
                hs.append(_lru(xr, lru_conv[jx].astype(F32), row(lru_conv_b[jx]), wbig.astype(BF16), bbig,
                               row(lru_lambda[jx, dd]), n_ctx, rev=(dd == 1)))
            sink = jnp.broadcast_to(swa_sink[jx].astype(F32)[:, None] * LOG2E, (D_HEADS, 128))
            od = _win_attn(q, k, v, sink, n_ctx)
            xa = _odd_out(xa, mod, hs[0], hs[1], gate, od, od_w_out[jx].astype(BF16))
        xa = _ffn(xa, mod2, row(norm_ffn[layer]), ffn_w_up[layer].astype(BF16), ffn_conv[layer].astype(F32),
                  ffn_w_down[layer].astype(BF16), n_ctx, row(final_norm) if layer == depth - 1 else None)
    return xa
```

```python
import functools
import math

import jax
import jax.numpy as jnp
import numpy as np
from jax import lax
from jax.experimental import pallas as pl
from jax.experimental.pallas import tpu as pltpu

F32 = jnp.float32
BF16 = jnp.bfloat16
HIGHEST = lax.Precision.HIGHEST

D_MODEL = 1024
GRID_W = 64
EPS = 1e-6
NEG_INF = -1e30
N_MOD = 6
ROPE_THETA = 10000.0
ROT_AXIS = 32
A_HEADS = 4
A_HD = 64
A_VD = 128
B_HEADS = 4
B_DK = 128
B_W = 512
B_CONV = 4
B_CHUNK = 64
C_WIDTH = 512
C_BLOCKS = 8
C_BD = 64
C_CONV = 4
C_POW = 8.0
D_HEADS = 8
D_KV = 2
D_HD = 64
WINDOW = 128
FFN = 2816
FFN_CONV = 3

TM = 256
TQ = 128
TQA = 256
KB = 256
ONES_ROWS = 16
SINK_ROWS = 16
LOG2E = math.log2(math.e)
TT = 128
GDN_NB = 4
WIN_NB = 2
HALO = 8
FC = 256
TTF = 128
MOD_ROWS = 16
VMEM_LIMIT = 56 * 1024 * 1024


def _cparams(sem):
    return pltpu.CompilerParams(dimension_semantics=sem, vmem_limit_bytes=VMEM_LIMIT)


def _sigmoid(x):
    return 1.0 / (1.0 + jnp.exp(-x))


def _sigmoid_t(x):
    return 0.5 * (1.0 + jnp.tanh(0.5 * x))


def _silu(x):
    return x * _sigmoid(x)


def _softplus(x):
    return jnp.maximum(x, 0.0) + jnp.log(1.0 + jnp.exp(-jnp.abs(x)))


def _gelu_tanh(x):
    return 0.5 * x * (1.0 + jnp.tanh(math.sqrt(2.0 / math.pi) * (x + 0.044715 * (x * x * x))))


def _dot(a, b):
    return jnp.dot(a, b, preferred_element_type=F32)


def _dot_hi(a, b):
    return jnp.dot(a, b, preferred_element_type=F32, precision=HIGHEST)


def _dot_nt(a, b):
    return lax.dot_general(a, b, (((1,), (1,)), ((), ())), preferred_element_type=F32)


def _dot_tn(a, b):
    return lax.dot_general(a, b, (((0,), (0,)), ((), ())), preferred_element_type=F32)


def _rms(x, g):
    return x * lax.rsqrt(jnp.mean(x * x, axis=-1, keepdims=True) + EPS) * g


def _modulate(x, g, shift, scale):
    return _rms(x, g) * (1.0 + scale) + shift


def _rope128(z, c, sa, sb):
    return z * c + pltpu.roll(z, 112, 1) * sa + pltpu.roll(z, 16, 1) * sb


def _rope(z, c, sa, sb):
    n = z.shape[1] // 128
    return jnp.concatenate([_rope128(z[:, i * 128:(i + 1) * 128], c, sa, sb) for i in range(n)], axis=1)


def _mod_kernel(s_ref, w_ref, b_ref, o_ref):
    s = _silu(s_ref[...])
    o_ref[...] = _dot_hi(s, w_ref[...]) + b_ref[...]


def _modulation(c_all, w_ada, b_ada):
    depth, d, n = w_ada.shape
    tn = 1536
    return pl.pallas_call(
        _mod_kernel,
        grid=(depth, n // tn),
        in_specs=[pl.BlockSpec((MOD_ROWS, d), lambda l, j: (0, 0)),
                  pl.BlockSpec((None, d, tn), lambda l, j: (l, 0, j)),
                  pl.BlockSpec((None, 1, tn), lambda l, j: (l, 0, j))],
        out_specs=pl.BlockSpec((None, MOD_ROWS, tn), lambda l, j: (l, 0, j)),
        out_shape=jax.ShapeDtypeStruct((depth, MOD_ROWS, n), F32),
        compiler_params=_cparams(("arbitrary", "arbitrary")),
        name="modulation",
    )(c_all, w_ada, b_ada.reshape(depth, 1, n))


def _mod_spec(nb):
    return pl.BlockSpec((None, 1, N_MOD * D_MODEL), lambda b, j: (jnp.where(j == 0, nb, b), 0, 0))


def _full(shape):
    nd = len(shape)
    return pl.BlockSpec(shape, lambda *_: (0,) * nd)


def _even_in_kernel(x_ref, prev_ref, next_ref, mod_ref, g_ref, c_ref, sa_ref, sb_ref, wa_ref, wg_ref, wgate_ref,
                    wba_ref, cw_ref, alog_ref, dtb_ref, q_ref, k_ref, v_ref, qkv_ref, gate_ref, bg_ref, ext_ref,
                    *, nt):
    d = D_MODEL
    j = pl.program_id(1)
    mod = mod_ref[...]
    g, shift, scale = g_ref[...], mod[:, 0:d], mod[:, d:2 * d]
    uf = _modulate(x_ref[...], g, shift, scale)
    u = uf.astype(BF16)
    lflag = (j >= 2).astype(F32)
    rflag = jnp.logical_and(j >= 1, j <= nt - 2).astype(F32)
    halo = jnp.concatenate([_modulate(prev_ref[...], g, shift, scale) * lflag,
                            _modulate(next_ref[...], g, shift, scale) * rflag], axis=0).astype(BF16)
    zh = _dot(halo, wg_ref[...])
    ext_ref[0:HALO, :] = zh[0:HALO]
    ext_ref[HALO:HALO + TM, :] = _dot(u, wg_ref[...])
    ext_ref[HALO + TM:2 * HALO + TM, :] = zh[HALO:2 * HALO]
    acc = None
    for kk in range(B_CONV):
        term = cw_ref[kk:kk + 1, :] * ext_ref[pl.ds(HALO - 2 + kk, TM), :]
        acc = term if acc is None else acc + term
    act = _silu(acc)

    def l2n(zz):
        return zz * lax.rsqrt(jnp.sum(zz * zz, axis=-1, keepdims=True) + EPS)

    for h in range(B_HEADS):
        qkv_ref[:, h * 128:(h + 1) * 128] = l2n(act[:, h * 128:(h + 1) * 128]) * (B_DK ** -0.5)
        qkv_ref[:, B_W + h * 128:B_W + (h + 1) * 128] = l2n(act[:, B_W + h * 128:B_W + (h + 1) * 128])
    qkv_ref[:, 2 * B_W:3 * B_W] = act[:, 2 * B_W:3 * B_W]

    c, sa, sb = c_ref[...], sa_ref[...], sb_ref[...]
    q = _dot(u, wa_ref[:, 0:512])
    q_ref[...] = (_rope(q, c, sa, sb) * (A_HD ** -0.5 * LOG2E)).astype(BF16)
    k = _dot(u, wa_ref[:, 512:1024])
    k_ref[...] = _rope(k, c, sa, sb).astype(BF16)
    v_ref[...] = _dot(u, wa_ref[:, 1024:1536]).astype(BF16)
    gate_ref[...] = _dot(u, wgate_ref[...]).astype(gate_ref.dtype)
    z = _dot(u, wba_ref[...])
    lane = lax.broadcasted_iota(jnp.int32, z.shape, 1)
    beta = _sigmoid(z)
    gdec = -jnp.exp(alog_ref[...]) * _softplus(z + dtb_ref[...])
    bg_ref[...] = jnp.where(lane < 2 * B_HEADS, beta, jnp.where(lane < 4 * B_HEADS, gdec, 0.0))


def _even_in(x, mod, g, tabs, wa, wg, wgate, wba, cw, alog, dtb):
    nb, t, d = x.shape
    nt = t // TM
    hb = TM // HALO
    tile = lambda w: pl.BlockSpec((None, TM, w), lambda b, j: (b, j, 0))
    prev = pl.BlockSpec((None, HALO, d), lambda b, j: (b, jnp.maximum(j * hb - 1, 0), 0))
    nxt = pl.BlockSpec((None, HALO, d), lambda b, j: (b, jnp.minimum((j + 1) * hb, t // HALO - 1), 0))
    tab = pl.BlockSpec((TM, 128), lambda b, j: (j, 0))
    outs = [jax.ShapeDtypeStruct((nb, t, 512), BF16)] * 3 + [
        jax.ShapeDtypeStruct((nb, t, 1536), F32), jax.ShapeDtypeStruct((nb, t, 512), BF16),
        jax.ShapeDtypeStruct((nb, t, 128), F32)]
    return pl.pallas_call(
        functools.partial(_even_in_kernel, nt=nt),
        grid=(nb, nt),
        in_specs=[tile(d), prev, nxt, _mod_spec(nb), _full((1, d)), tab, tab, tab, _full(wa.shape), _full(wg.shape),
                  _full(wgate.shape), _full(wba.shape), _full(cw.shape), _full((1, 128)), _full((1, 128))],
        out_specs=[tile(512), tile(512), tile(512), tile(1536), tile(512), tile(128)],
        out_shape=outs,
        scratch_shapes=[pltpu.VMEM((TM + 2 * HALO, 1536), F32)],
        compiler_params=_cparams(("parallel", "parallel")),
        name="even_in",
    )(x, x, x, mod, g, *tabs, wa, wg, wgate, wba, cw, alog, dtb)


def _diff_attn_kernel(q_ref, k_ref, v_ref, lv_ref, g_ref, o_ref, vt_ref, sa_ref, sb_ref, ma_ref, mb_ref,
                      *, lam_init, n_ctx):
    i = pl.program_id(2)
    t = k_ref.shape[0]
    nblk = t // KB

    @pl.when(i == 0)
    def _():
        vt_ref[0:A_VD, :] = v_ref[...].astype(F32).T.astype(BF16)
        orow = lax.broadcasted_iota(jnp.int32, (ONES_ROWS, t), 0)
        vt_ref[A_VD:A_VD + ONES_ROWS, :] = jnp.where(orow == 0, 1.0, 0.0).astype(BF16)
        sb_ref[...] = jnp.zeros_like(sb_ref)
        mb_ref[...] = jnp.zeros_like(mb_ref)

    def step(s_new, m_new, s_old, m_old):
        lv = lv_ref[...]
        lam = (jnp.exp(jnp.sum(lv[0:1] * lv[1:2], axis=-1, keepdims=True))
               - jnp.exp(jnp.sum(lv[2:3] * lv[3:4], axis=-1, keepdims=True)) + lam_init)
        m = jnp.max(m_old[...], axis=0, keepdims=True)
        is_ctx = jnp.minimum(i, pl.num_programs(2) - 2) < n_ctx // TQA
        q = q_ref[...]
        lane = lax.broadcasted_iota(jnp.int32, q.shape, 1)
        zero = jnp.zeros_like(q)
        qz = jnp.concatenate([jnp.where(lane < A_HD, q, zero), jnp.where(lane >= A_HD, q, zero)], axis=0)
        mrun = None
        oe = None
        for kb in range(nblk):
            rows = slice(kb * KB, (kb + 1) * KB)
            e = jnp.exp2(s_old[rows, :] - m).astype(BF16)
            part = _dot(vt_ref[:, rows], e)
            oe = part if oe is None else oe + part
            sblk = _dot_nt(k_ref[rows, :], qz)
            if kb >= n_ctx // KB:
                sblk = jnp.where(is_ctx, NEG_INF, sblk)
            s_new[rows, :] = sblk
            part = jnp.max(sblk.reshape(KB // 8, 8, 2 * TQA), axis=0)
            mrun = part if mrun is None else jnp.maximum(mrun, part)
        m_new[...] = mrun
        on = oe[0:A_VD] / oe[A_VD:A_VD + 1]
        od = on[:, 0:TQA] - lam * on[:, TQA:2 * TQA]
        y = od * lax.rsqrt(jnp.mean(od * od, axis=0, keepdims=True) + EPS) * (g_ref[...] * (1.0 - lam_init))
        o_ref[...] = y.T.astype(o_ref.dtype)

    @pl.when(i % 2 == 0)
    def _():
        step(sa_ref, ma_ref, sb_ref, mb_ref)

    @pl.when(i % 2 == 1)
    def _():
        step(sb_ref, mb_ref, sa_ref, ma_ref)


def _diff_attn(q, k, v, lam_vec, subln, lam_init, n_ctx):
    nb, t, _ = q.shape
    nq = t // TQA
    kv = pl.BlockSpec((None, t, 128), lambda b, h, i: (b, 0, h))
    qin = pl.BlockSpec((None, TQA, 128), lambda b, h, i: (b, jnp.minimum(i, nq - 1), h))
    out = pl.BlockSpec((None, TQA, 128), lambda b, h, i: (b, jnp.maximum(i - 1, 0), h))
    return pl.pallas_call(
        functools.partial(_diff_attn_kernel, lam_init=lam_init, n_ctx=n_ctx),
        grid=(nb, A_HEADS, nq + 1),
        in_specs=[qin, kv, kv, _full((4, A_HD)), _full((A_VD, 1))],
        out_specs=out,
        out_shape=jax.ShapeDtypeStruct((nb, t, A_HEADS * A_VD), BF16),
        scratch_shapes=[pltpu.VMEM((A_VD + ONES_ROWS, t), BF16), pltpu.VMEM((t, 2 * TQA), F32),
                        pltpu.VMEM((t, 2 * TQA), F32), pltpu.VMEM((8, 2 * TQA), F32), pltpu.VMEM((8, 2 * TQA), F32)],
        compiler_params=_cparams(("parallel", "parallel", "arbitrary")),
        name="diff_attn",
    )(q, k, v, lam_vec, subln)


def _gdn_tile_index(j, nt, rev):
    return jnp.where(j == 0, 0, nt - j) if rev else j


def _gdn_kernel(qkv_ref, bg_ref, o_ref, s_ref, *, rev, nt):
    j = pl.program_id(1)
    dirn = 1 if rev else 0
    nch = TM // B_CHUNK

    @pl.when(j == 0)
    def _():
        s_ref[...] = jnp.zeros_like(s_ref)

    ri = lax.broadcasted_iota(jnp.int32, (TM, TM), 0)
    ci = lax.broadcasted_iota(jnp.int32, (TM, TM), 1)
    same = (ri // B_CHUNK) == (ci // B_CHUNK)
    incl = jnp.logical_and(same, (ri <= ci) if rev else (ri >= ci))
    strict = jnp.logical_and(same, (ri < ci) if rev else (ri > ci))
    eye = (ri == ci).astype(F32)
    inclb = incl.astype(BF16)
    rchunk = lax.broadcasted_iota(jnp.int32, (TM, B_DK), 0) // B_CHUNK

    def by_chunk(z):
        return jnp.concatenate([jnp.where(rchunk == c, z, 0.0) for c in range(nch)], axis=1).astype(BF16)

    chains = [(bi, h) for bi in range(GDN_NB) for h in range(B_HEADS)]
    nchain = len(chains)
    lasts = [c * B_CHUNK if rev else (c + 1) * B_CHUNK - 1 for c in range(nch)]
    q, k, v, beta, gcol, eg, qkm, p, pw = ([None] * nchain for _ in range(9))
    for bi in range(GDN_NB):
        bg = bg_ref[bi]
        b1 = bg.astype(BF16)
        r1 = bg - b1.astype(F32)
        b2 = r1.astype(BF16)
        b3 = (r1 - b2.astype(F32)).astype(BF16)
        gcum = _dot(inclb, b1) + (_dot(inclb, b2) + _dot(inclb, b3))
        gcum_t = gcum.T
        for h in range(B_HEADS):
            n = bi * B_HEADS + h
            q[n] = qkv_ref[bi, :, h * 128:(h + 1) * 128]
            k[n] = qkv_ref[bi, :, B_W + h * 128:B_W + (h + 1) * 128]
            v[n] = qkv_ref[bi, :, 2 * B_W + h * 128:2 * B_W + (h + 1) * 128]
            cb = dirn * B_HEADS + h
            cg = 2 * B_HEADS + cb
            beta[n] = bg[:, cb:cb + 1]
            gcol[n] = gcum[:, cg:cg + 1]
            grow = gcum_t[cg:cg + 1, :]
            eg[n] = jnp.exp(gcol[n])
            decay = jnp.where(incl, jnp.exp(jnp.where(incl, gcol[n] - grow, 0.0)), 0.0)
            kb = k[n].astype(BF16)
            qkm[n] = _dot_nt(q[n].astype(BF16), kb) * decay
            pw[n] = jnp.where(strict, beta[n] * _dot_nt(kb, kb) * decay, 0.0)

    xr = ri ^ ci
    for lvl in range(6):
        joins = (xr >> lvl) == 1
        for n in range(nchain):
            l_s = jnp.where(joins, pw[n], 0.0)
            if lvl == 0:
                p[n] = eye - l_s
            else:
                pb = p[n].astype(BF16)
                p[n] = p[n] - _dot(pb, _dot(l_s.astype(BF16), pb).astype(BF16))

    qku, qeff, mn = ([None] * nchain for _ in range(3))
    for n in range(nchain):
        rhs = jnp.concatenate([beta[n] * v[n], (beta[n] * eg[n]) * k[n]], axis=1)
        uw = _dot(p[n].astype(BF16), rhs.astype(BF16))
        qkuw = _dot(qkm[n].astype(BF16), uw.astype(BF16))
        qku[n] = qkuw[:, 0:128]
        qeff[n] = (q[n] * eg[n] - qkuw[:, 128:256]).astype(BF16)
        glast = jnp.concatenate(
            [jnp.broadcast_to(gcol[n][r:r + 1, :], (B_CHUNK, 1)) for r in lasts], axis=0)
        kdec = (k[n] * jnp.exp(glast - gcol[n])).astype(BF16)
        mn[n] = _dot_tn(kdec, jnp.concatenate([by_chunk(uw[:, 128:256]), by_chunk(uw[:, 0:128])], axis=1))

    for step in range(nch):
        c = nch - 1 - step if rev else step
        r0, r1 = c * B_CHUNK, (c + 1) * B_CHUNK
        for n, (bi, h) in enumerate(chains):
            sh = s_ref[n]
            shb = sh.astype(BF16)
            o_ref[bi, r0:r1, h * 128:(h + 1) * 128] = (_dot(qeff[n][r0:r1], shb) + qku[n][r0:r1]).astype(o_ref.dtype)
            mc = mn[n][:, c * 128:(c + 1) * 128].astype(BF16)
            nc = mn[n][:, (nch + c) * 128:(nch + c + 1) * 128]
            gl = jnp.exp(gcol[n][lasts[c]:lasts[c] + 1, :])
            s_ref[n] = sh * gl - _dot(mc, shb) + nc


def _gdn(qkv, bg, rev):
    nb, t, w = qkv.shape
    nt = t // TM
    main = lambda ww: pl.BlockSpec((GDN_NB, TM, ww), lambda b, j: (b, _gdn_tile_index(j, nt, rev), 0))
    return pl.pallas_call(
        functools.partial(_gdn_kernel, rev=rev, nt=nt),
        grid=(nb // GDN_NB, nt),
        in_specs=[main(w), main(128)],
        out_specs=main(B_W),
        out_shape=jax.ShapeDtypeStruct((nb, t, B_W), BF16),
        scratch_shapes=[pltpu.VMEM((GDN_NB * B_HEADS, B_DK, B_DK), F32)],
        compiler_params=_cparams(("parallel", "arbitrary")),
        name="gdn_rev" if rev else "gdn_fwd",
    )(qkv, bg)


def _even_out_kernel(x_ref, mod_ref, ya_ref, of_ref, or_ref, gate_ref, og_ref, w_ref, o_ref):
    d = D_MODEL
    ob = of_ref[...].astype(F32) + or_ref[...].astype(F32)
    gate = gate_ref[...].astype(F32)
    yb = jnp.concatenate(
        [_rms(ob[:, h * 128:(h + 1) * 128], og_ref[...]) * _silu(gate[:, h * 128:(h + 1) * 128])
         for h in range(B_HEADS)], axis=1)
    y = _dot(ya_ref[...], w_ref[0:512, :]) + _dot(yb.astype(BF16), w_ref[512:1024, :])
    o_ref[...] = x_ref[...] + mod_ref[:, 2 * d:3 * d] * y


def _even_out(x, mod, ya, of, orv, gate, og, w):
    nb, t, d = x.shape
    tile = lambda ww: pl.BlockSpec((None, TM, ww), lambda b, j: (b, j, 0))
    return pl.pallas_call(
        _even_out_kernel,
        grid=(nb, t // TM),
        in_specs=[tile(d), _mod_spec(nb), tile(512), tile(512), tile(512), tile(512), _full((1, 128)),
                  _full(w.shape)],
        out_specs=tile(d),
        out_shape=jax.ShapeDtypeStruct(x.shape, F32),
        compiler_params=_cparams(("parallel", "parallel")),
        name="even_out",
    )(x, mod, ya, of, orv, gate, og, w)


def _ffn_kernel(x_ref, prev_ref, next_ref, mod_ref, g_ref, wup_ref, cw_ref, wdn_ref, fg_ref, o_ref, act_ref,
                *, ntt, nct, final):
    d = D_MODEL
    nb = x_ref.shape[0]
    j = pl.program_id(0)

    def body():
        mod = mod_ref[...]
        shift, scale, gate = mod[:, 3 * d:4 * d], mod[:, 4 * d:5 * d], mod[:, 5 * d:6 * d]
        g = g_ref[...]
        lflag = jnp.logical_and(j != 0, j != nct).astype(F32)
        rflag = jnp.logical_and(j != nct - 1, j != ntt - 1).astype(F32)
        x3 = pltpu.einshape("btd->tbd", x_ref[...])
        xp = pltpu.einshape("btd->tbd", prev_ref[...])[HALO - 1]
        xn = pltpu.einshape("btd->tbd", next_ref[...])[0]
        u3 = jnp.concatenate([(_modulate(xp, g, shift, scale) * lflag)[None],
                              _modulate(x3, g, shift, scale),
                              (_modulate(xn, g, shift, scale) * rflag)[None]], axis=0)
        u = u3.reshape((TTF + 2) * nb, d).astype(BF16)
        rows = TTF * nb
        for c in range(FFN // FC):
            c0 = c * FC
            hg = _dot(u, wup_ref[:, c0:c0 + FC])
            hv = _dot(u, wup_ref[:, FFN + c0:FFN + c0 + FC])
            cg = None
            cv = None
            for kk in range(FFN_CONV):
                tg = cw_ref[kk:kk + 1, c0:c0 + FC] * hg[kk * nb:kk * nb + rows]
                tv = cw_ref[kk:kk + 1, FFN + c0:FFN + c0 + FC] * hv[kk * nb:kk * nb + rows]
                cg = tg if cg is None else cg + tg
                cv = tv if cv is None else cv + tv
            act_ref[:, c0:c0 + FC] = (_silu(cg) * cv).astype(BF16)
        out3 = x3 + gate * _dot(act_ref[...], wdn_ref[...]).reshape(TTF, nb, d)
        if final:
            out3 = _rms(out3, fg_ref[...])
        o_ref[...] = pltpu.einshape("tbd->btd", out3)

    if final:
        pl.when(j >= nct)(body)
    else:
        body()


def _ffn(x, mod2, g, wup, cw, wdn, n_ctx, final_g=None):
    nb, t, d = x.shape
    ntt, nct = t // TTF, n_ctx // TTF
    hb = TTF // HALO
    tile = pl.BlockSpec((nb, TTF, d), lambda j: (0, j, 0))
    prev = pl.BlockSpec((nb, HALO, d), lambda j: (0, jnp.maximum(j * hb - 1, 0), 0))
    nxt = pl.BlockSpec((nb, HALO, d), lambda j: (0, jnp.minimum((j + 1) * hb, t // HALO - 1), 0))
    modspec = pl.BlockSpec((None, nb, N_MOD * d), lambda j: (jnp.where(j < nct, 0, 1), 0, 0))
    resident = lambda shape: pl.BlockSpec(shape, lambda j: (0, 0), pipeline_mode=pl.Buffered(1))
    final = final_g is not None
    t_out = t - n_ctx if final else t
    out_tile = pl.BlockSpec((nb, TTF, d), lambda j: (0, jnp.maximum(j - nct, 0), 0)) if final else tile
    out = pl.pallas_call(
        functools.partial(_ffn_kernel, ntt=ntt, nct=nct, final=final),
        grid=(ntt,),
        in_specs=[tile, prev, nxt, modspec, _full((1, d)), resident(wup.shape), _full(cw.shape),
                  resident(wdn.shape), _full((1, d))],
        out_specs=out_tile,
        out_shape=jax.ShapeDtypeStruct((nb, t_out, d), F32),
        scratch_shapes=[pltpu.VMEM((TTF * nb, FFN), BF16)],
        compiler_params=_cparams(("arbitrary" if final else "parallel",)),
        name="ffn",
    )(x, x, x, mod2, g, wup, cw, wdn, final_g if final else g)
    return out


def _odd_in_kernel(x_ref, mod_ref, g_ref, c_ref, sa_ref, sb_ref, wr_ref, wq_ref, wkv_ref,
                   xr_ref, gate_ref, q_ref, k_ref, v_ref):
    d = D_MODEL
    mod = mod_ref[...]
    u = _modulate(x_ref[...], g_ref[...], mod[:, 0:d], mod[:, d:2 * d]).astype(BF16)
    c, sa, sb = c_ref[...], sa_ref[...], sb_ref[...]
    xr_ref[...] = _dot(u, wr_ref[:, 0:C_WIDTH])
    gate_ref[...] = _dot(u, wr_ref[:, C_WIDTH:2 * C_WIDTH])
    q_ref[...] = (_rope(_dot(u, wq_ref[...]), c, sa, sb) * (D_HD ** -0.5 * LOG2E)).astype(BF16)
    k_ref[...] = _rope(_dot(u, wkv_ref[:, 0:256]), c, sa, sb).astype(BF16)
    v = _dot(u, wkv_ref[:, 256:512])
    vlane = lax.broadcasted_iota(jnp.int32, v.shape, 1)
    v_ref[...] = jnp.where(vlane % 128 == D_HD, 1.0, v).astype(BF16)


def _odd_in(x, mod, g, tabs, wr, wq, wkv):
    nb, t, d = x.shape
    tile = lambda w: pl.BlockSpec((None, TM, w), lambda b, j: (b, j, 0))
    tab = pl.BlockSpec((TM, 128), lambda b, j: (j, 0))
    outs = [jax.ShapeDtypeStruct((nb, t, C_WIDTH), F32)] * 2 + [
        jax.ShapeDtypeStruct((nb, t, 512), BF16), jax.ShapeDtypeStruct((nb, t, 256), BF16),
        jax.ShapeDtypeStruct((nb, t, 256), BF16)]
    return pl.pallas_call(
        _odd_in_kernel,
        grid=(nb, t // TM),
        in_specs=[tile(d), _mod_spec(nb), _full((1, d)), tab, tab, tab, _full(wr.shape), _full(wq.shape),
                  _full(wkv.shape)],
        out_specs=[tile(C_WIDTH), tile(C_WIDTH), tile(512), tile(256), tile(256)],
        out_shape=outs,
        compiler_params=_cparams(("parallel", "parallel")),
        name="odd_in",
    )(x, mod, g, *tabs, wr, wq, wkv)


def _lru_tile_index(j, ntt, nct, rev):
    return jnp.where(j < nct, nct - 1 - j, ntt + nct - 1 - j) if rev else j


def _lru_kernel(x_ref, prev_ref, next_ref, cw_ref, cb_ref, w_ref, b_ref, lam_ref, o_ref,
                h_ref, a_ref, bc_ref, *, rev, ntt, nct):
    j = pl.program_id(0)
    jj = _lru_tile_index(j, ntt, nct, rev)
    nb = x_ref.shape[0]
    tmajor = lambda ref: pltpu.einshape("btc->tbc", ref[...])

    @pl.when(j == 0)
    def _():
        h_ref[...] = jnp.zeros_like(h_ref)

    lflag = jnp.logical_and(jj != 0, jj != nct).astype(F32)
    rflag = jnp.logical_and(jj != nct - 1, jj != ntt - 1).astype(F32)
    ext = jnp.concatenate([tmajor(prev_ref)[HALO - 2:HALO] * lflag, tmajor(x_ref),
                           tmajor(next_ref)[0:1] * rflag], axis=0)
    xc = cb_ref[...]
    for kk in range(C_CONV):
        xc = xc + cw_ref[kk:kk + 1, :] * ext[kk:kk + TT]
    xc2 = xc.reshape(TT * nb, C_WIDTH)
    z = _dot(xc2.astype(BF16), w_ref[...]) + b_ref[...]
    r = _sigmoid_t(z[:, 0:C_WIDTH])
    gi = _sigmoid_t(z[:, C_WIDTH:2 * C_WIDTH])
    log_a = -C_POW * r * _softplus(-lam_ref[...])
    a = jnp.exp(log_a)
    bc = jnp.sqrt(1.0 - a * a) * (gi * xc2)
    a_ref[...] = a.reshape(TT, nb, C_WIDTH)
    bc_ref[...] = bc.reshape(TT, nb, C_WIDTH)

    def body(s, h):
        t = TT - 1 - s if rev else s
        h = a_ref[t] * h + bc_ref[t]
        a_ref[t] = h
        return h

    h_ref[...] = lax.fori_loop(0, TT, body, h_ref[...], unroll=8)
    o_ref[...] = pltpu.einshape("tbc->btc", a_ref[...])


def _lru(xr, conv_w, conv_b, w, b, lam, n_ctx, rev):
    nb, t, c = xr.shape
    ntt, nct = t // TT, n_ctx // TT
    hb = TT // HALO
    idx = lambda j: _lru_tile_index(j, ntt, nct, rev)
    main = pl.BlockSpec((nb, TT, c), lambda j: (0, idx(j), 0))
    prev = pl.BlockSpec((nb, HALO, c), lambda j: (0, jnp.maximum(idx(j) * hb - 1, 0), 0))
    nxt = pl.BlockSpec((nb, HALO, c), lambda j: (0, jnp.minimum((idx(j) + 1) * hb, t // HALO - 1), 0))
    return pl.pallas_call(
        functools.partial(_lru_kernel, rev=rev, ntt=ntt, nct=nct),
        grid=(ntt,),
        in_specs=[main, prev, nxt, _full(conv_w.shape), _full(conv_b.shape), _full(w.shape), _full(b.shape),
                  _full(lam.shape)],
        out_specs=main,
        out_shape=jax.ShapeDtypeStruct(xr.shape, F32),
        scratch_shapes=[pltpu.VMEM((nb, c), F32), pltpu.VMEM((TT, nb, c), F32), pltpu.VMEM((TT, nb, c), F32)],
        compiler_params=_cparams(("arbitrary",)),
        name="lru_rev" if rev else "lru_fwd",
    )(xr, xr, xr, conv_w, conv_b, w, b, lam)


def _win_attn_kernel(q_ref, k_ref, v_ref, sink_ref, o_ref, *, n_ctx, n_lat):
    i = pl.program_id(1)
    nctx_tiles = n_ctx // TQ
    nlb = n_lat // TQ
    grp = D_HEADS // D_KV
    width = grp * TQ
    lane = lax.broadcasted_iota(jnp.int32, (TQ, 128), 1)
    low = lane < D_HD
    srow = lax.broadcasted_iota(jnp.int32, (SINK_ROWS, width), 0)
    vrow = lax.broadcasted_iota(jnp.int32, (SINK_ROWS, 128), 0)
    vlane = lax.broadcasted_iota(jnp.int32, (SINK_ROWS, 128), 1)
    v_sink = jnp.where(jnp.logical_and(vrow == 0, vlane == D_HD), 1.0, 0.0).astype(BF16)

    def stacked_q(bi, g):
        parts = []
        for sl in range(grp // 2):
            slab = q_ref[bi, :, (g * (grp // 2) + sl) * 128:(g * (grp // 2) + sl + 1) * 128]
            zero = jnp.zeros_like(slab)
            parts += [jnp.where(low, slab, zero), jnp.where(low, zero, slab)]
        return jnp.concatenate(parts, axis=0)

    def sink_row(g):
        return jnp.concatenate(
            [jnp.broadcast_to(sink_ref[g * grp + hh:g * grp + hh + 1, 0:1], (1, TQ)) for hh in range(grp)], axis=1)

    def finish(bi, g, o):
        out = o[:, 0:D_HD] / o[:, D_HD:D_HD + 1]
        out = jnp.concatenate([out, jnp.zeros_like(out)], axis=1)
        for sl in range(grp // 2):
            a = out[(2 * sl) * TQ:(2 * sl + 1) * TQ]
            b = pltpu.roll(out[(2 * sl + 1) * TQ:(2 * sl + 2) * TQ], D_HD, 1)
            col = (g * (grp // 2) + sl) * 128
            o_ref[bi, :, col:col + 128] = jnp.where(low, a, b).astype(o_ref.dtype)

    def attend(local):
        probs = [(bi, g) for bi in range(WIN_NB) for g in range(D_KV)]
        qz = [stacked_q(bi, g) for bi, g in probs]
        sk = [sink_row(g) for _, g in probs]
        sc, sl_, m = ([None] * len(probs) for _ in range(3))
        for n, (bi, g) in enumerate(probs):
            sc[n] = _dot_nt(k_ref[bi, 0:n_ctx, g * 128:(g + 1) * 128], qz[n])
            m[n] = jnp.maximum(jnp.max(sc[n], axis=0, keepdims=True), sk[n])
            if local is not None:
                start, mask = local
                kl = k_ref[bi, pl.ds(start, 3 * TQ), g * 128:(g + 1) * 128]
                sl_[n] = jnp.where(mask, _dot_nt(kl, qz[n]), NEG_INF)
                m[n] = jnp.maximum(m[n], jnp.max(sl_[n], axis=0, keepdims=True))
        for n, (bi, g) in enumerate(probs):
            e_sink = jnp.where(srow == 0, jnp.exp2(sk[n] - m[n]), 0.0).astype(BF16)
            o = (_dot_tn(jnp.exp2(sc[n] - m[n]).astype(BF16), v_ref[bi, 0:n_ctx, g * 128:(g + 1) * 128])
                 + _dot_tn(e_sink, v_sink))
            if local is not None:
                vl = v_ref[bi, pl.ds(local[0], 3 * TQ), g * 128:(g + 1) * 128]
                o = o + _dot_tn(jnp.exp2(sl_[n] - m[n]).astype(BF16), vl)
            finish(bi, g, o)

    @pl.when(i < nctx_tiles)
    def _():
        attend(None)

    @pl.when(i >= nctx_tiles)
    def _():
        il = i - nctx_tiles
        kb = jnp.clip(il - 1, 0, nlb - 3)
        start = pl.multiple_of(n_ctx + kb * TQ, TQ)
        kpos = kb * TQ + lax.broadcasted_iota(jnp.int32, (3 * TQ, width), 0)
        qpos = il * TQ + (lax.broadcasted_iota(jnp.int32, (3 * TQ, width), 1) % TQ)
        mask = jnp.abs(kpos - qpos) <= WINDOW
        attend((start, mask))


def _win_attn(q, k, v, sink, n_ctx):
    nb, t, _ = q.shape
    kv = pl.BlockSpec((WIN_NB, t, 256), lambda b, i: (b, 0, 0))
    qo = pl.BlockSpec((WIN_NB, TQ, 512), lambda b, i: (b, i, 0))
    return pl.pallas_call(
        functools.partial(_win_attn_kernel, n_ctx=n_ctx, n_lat=t - n_ctx),
        grid=(nb // WIN_NB, t // TQ),
        in_specs=[qo, kv, kv, _full(sink.shape)],
        out_specs=qo,
        out_shape=jax.ShapeDtypeStruct((nb, t, D_HEADS * D_HD), BF16),
        compiler_params=_cparams(("parallel", "arbitrary")),
        name="win_attn",
    )(q, k, v, sink)


def _odd_out_kernel(x_ref, mod_ref, hf_ref, hr_ref, gate_ref, od_ref, w_ref, o_ref):
    d = D_MODEL
    yc = (hf_ref[...] + hr_ref[...]) * _gelu_tanh(gate_ref[...])
    y = _dot(yc.astype(BF16), w_ref[0:512, :]) + _dot(od_ref[...], w_ref[512:1024, :])
    o_ref[...] = x_ref[...] + mod_ref[:, 2 * d:3 * d] * y


def _odd_out(x, mod, hf, hr, gate, od, w):
    nb, t, d = x.shape
    tile = lambda ww: pl.BlockSpec((None, TM, ww), lambda b, j: (b, j, 0))
    return pl.pallas_call(
        _odd_out_kernel,
        grid=(nb, t // TM),
        in_specs=[tile(d), _mod_spec(nb), tile(C_WIDTH), tile(C_WIDTH), tile(C_WIDTH), tile(512), _full(w.shape)],
        out_specs=tile(d),
        out_shape=jax.ShapeDtypeStruct(x.shape, F32),
        compiler_params=_cparams(("parallel", "parallel")),
        name="odd_out",
    )(x, mod, hf, hr, gate, od, w)


def _rope_tables(n_ctx, n_lat):
    pos = np.arange(n_lat)
    inv = ROPE_THETA ** (-np.arange(0, ROT_AXIS, 2, dtype=np.float64) / ROT_AXIS)
    ang_r = (pos // GRID_W)[:, None] * inv
    ang_c = (pos % GRID_W)[:, None] * inv
    lane = np.arange(128) % 64
    seg, f = lane // 16, lane % 16
    ang = np.where(seg[None, :] < 2, ang_r[:, f], ang_c[:, f])
    c = np.cos(ang)
    s = np.sin(ang)
    sa = np.where((seg % 2 == 0)[None, :], -s, 0.0)
    sb = np.where((seg % 2 == 1)[None, :], s, 0.0)
    pad = lambda a, fill: np.concatenate([np.full((n_ctx, 128), fill), a], axis=0).astype(np.float32)
    return jnp.asarray(pad(c, 1.0)), jnp.asarray(pad(sa, 0.0)), jnp.asarray(pad(sb, 0.0))


def _block_diag(w):
    eye = jnp.eye(C_BLOCKS, dtype=w.dtype)
    return jnp.einsum('hij,hg->higj', w, eye).reshape(C_WIDTH, C_WIDTH)


def kernel(x, c, ctx, c_ctx, w_ada, b_ada, norm_mix, norm_ffn, ffn_w_up, ffn_conv, ffn_w_down, final_norm,
           ev_w_in, ev_w_out, diff_lambda, diff_subln, gdn_conv, gdn_a_log, gdn_dt_bias, gdn_norm,
           od_w_in, od_w_out, lru_conv, lru_conv_b, lru_wa, lru_ba, lru_wx, lru_bx, lru_lambda, swa_sink):
    nb, n_lat, d = x.shape
    n_ctx = ctx.shape[1]
    depth = w_ada.shape[0]
    assert d == D_MODEL and n_ctx == TM and n_lat % TM == 0 and n_lat // TQ >= 3 and nb < MOD_ROWS
    assert nb % GDN_NB == 0 and nb % WIN_NB == 0
    t = n_ctx + n_lat

    xa = jnp.concatenate([ctx, x], axis=1)
    c_all = jnp.zeros((MOD_ROWS, d), F32).at[0:nb].set(c).at[nb].set(c_ctx)
    mod_all = _modulation(c_all, w_ada, b_ada).reshape(depth, MOD_ROWS, 1, N_MOD * d)
    tabs = _rope_tables(n_ctx, n_lat)
    row = lambda v: v.reshape(1, -1).astype(F32)

    for layer in range(depth):
        jx = layer // 2
        mod = mod_all[layer]
        mod2 = jnp.stack([jnp.broadcast_to(mod[nb], (nb, N_MOD * d)), mod[0:nb, 0]], axis=0)
        if layer % 2 == 0:
            lam_init = 0.8 - 0.6 * math.exp(-0.3 * layer)
            w_in = ev_w_in[jx]
            wa = w_in[:, 0:1536].astype(BF16)
            wg = w_in[:, 1536:3072].astype(BF16)
            wgate = w_in[:, 3072:3584].astype(BF16)
            wba = jnp.pad(w_in[:, 3584:3600], ((0, 0), (0, 112))).astype(BF16)
            pad16 = lambda v: jnp.pad(v.reshape(1, 8).astype(F32), ((0, 0), (8, 112)))
            q, k, v, qkv, gate, bg = _even_in(xa, mod, row(norm_mix[layer]), tabs, wa, wg, wgate, wba,
                                              gdn_conv[jx].astype(F32), pad16(gdn_a_log[jx]), pad16(gdn_dt_bias[jx]))
            ya = _diff_attn(q, k, v, diff_lambda[jx].astype(F32), diff_subln[jx].astype(F32).reshape(-1, 1),
                            lam_init, n_ctx)
            of = _gdn(qkv, bg, rev=False)
            orv = _gdn(qkv, bg, rev=True)
            xa = _even_out(xa, mod, ya, of, orv, gate, row(gdn_norm[jx]), ev_w_out[jx].astype(BF16))
        else:
            w_in = od_w_in[jx]
            wr = w_in[:, 0:1024].astype(BF16)
            wq = w_in[:, 1024:1536].astype(BF16)
            dup = lambda w: jnp.concatenate([w[:, 0:64], w[:, 0:64], w[:, 64:128], w[:, 64:128]], axis=1)
            zpad = lambda w: jnp.concatenate([w[:, 0:64], jnp.zeros_like(w[:, 0:64]), w[:, 64:128], jnp.zeros_like(w[:, 0:64])], axis=1)
            wkv = jnp.concatenate([dup(w_in[:, 1536:1664]), zpad(w_in[:, 1664:1792])], axis=1).astype(BF16)
            xr, gate, q, k, v = _odd_in(xa, mod, row(norm_mix[layer]), tabs, wr, wq, wkv)
            hs = []
            for dd in range(2):
                wbig = jnp.concatenate([_block_diag(lru_wa[jx, dd]), _block_diag(lru_wx[jx, dd])], axis=1)
                bbig = jnp.concatenate([lru_ba[jx, dd], lru_bx[jx, dd]]).reshape(1, -1).astype(F32)
                hs.append(_lru(xr, lru_conv[jx].astype(F32), row(lru_conv_b[jx]), wbig.astype(BF16), bbig,
                               row(lru_lambda[jx, dd]), n_ctx, rev=(dd == 1)))
            sink = jnp.broadcast_to(swa_sink[jx].astype(F32)[:, None] * LOG2E, (D_HEADS, 128))
            od = _win_attn(q, k, v, sink, n_ctx)
            xa = _odd_out(xa, mod, hs[0], hs[1], gate, od, od_w_out[jx].astype(BF16))
        xa = _ffn(xa, mod2, row(norm_ffn[layer]), ffn_w_up[layer].astype(BF16), ffn_conv[layer].astype(F32),
                  ffn_w_down[layer].astype(BF16), n_ctx, row(final_norm) if layer == depth - 1 else None)
    return xa
```
